```python
import math
import jax
import jax.numpy as jnp
from jax import lax
import numpy as np

D_MODEL = 2048
BATCH = 16
SEQ = 2048
DEPTH = 2

GRID_W = 64
CTX_LEN = 256
HEAD_DIM = 128
MIX_HEADS = D_MODEL // HEAD_DIM
ROPE_THETA = 10000.0
SCALE = HEAD_DIM ** -0.5
NEG_INF = -1e30
A_HEADS = MIX_HEADS // 2
A_KV_HEADS = A_HEADS // 4
A_WINDOW = 128
A_BLOCK = 128
B_HEADS = MIX_HEADS - A_HEADS
NA_ROWS = 8
NA_COLS = 16
C_HEADS = MIX_HEADS // 2
C_KV_HEADS = C_HEADS // 4
D_HEADS = (MIX_HEADS - C_HEADS) // 2
Q_BLOCK = 128
EVEN_SPLITS = (A_HEADS * HEAD_DIM, A_KV_HEADS * HEAD_DIM, A_KV_HEADS * HEAD_DIM,
               B_HEADS * HEAD_DIM, B_HEADS * HEAD_DIM, B_HEADS * HEAD_DIM)
ODD_SPLITS = (C_HEADS * HEAD_DIM, C_KV_HEADS * HEAD_DIM, C_KV_HEADS * HEAD_DIM,
              D_HEADS * 2 * HEAD_DIM, D_HEADS * 2 * HEAD_DIM, D_HEADS * 2 * HEAD_DIM)
EVEN_OUT = (A_HEADS + B_HEADS) * HEAD_DIM
ODD_OUT = (C_HEADS + 2 * D_HEADS) * HEAD_DIM
N_EXPERTS = 16
N_GROUPS = 4
EXPERTS_PER_GROUP = N_EXPERTS // N_GROUPS
TOP_K = 2
EXPERT_FF = D_MODEL // 2
ALPHA = (2 * DEPTH) ** 0.25
BETA = (8 * DEPTH) ** -0.25

kernel_name = 'hybrid_diffusion_prefix_block'


def _cuts(sizes):
    return [int(v) for v in np.cumsum(sizes)[:-1]]


def layer_norm(x, g, b, eps=1e-5):
    xf = x.astype(jnp.float32)
    mu = jnp.mean(xf, -1, keepdims=True)
    var = jnp.mean(jnp.square(xf - mu), -1, keepdims=True)
    return ((xf - mu) * lax.rsqrt(var + eps)).astype(x.dtype) * g + b


def rms_norm(x, g, eps=1e-6):
    xf = x.astype(jnp.float32)
    return (xf * lax.rsqrt(jnp.mean(jnp.square(xf), -1, keepdims=True) + eps)).astype(x.dtype) * g


def axial_rope(n_tokens):
    t = jnp.arange(n_tokens, dtype=jnp.int32)
    row = (t // GRID_W).astype(jnp.float32)
    col = (t % GRID_W).astype(jnp.float32)
    n_freq = HEAD_DIM // 4
    inv = ROPE_THETA ** (-jnp.arange(n_freq, dtype=jnp.float32) / n_freq)
    ang = jnp.concatenate([row[:, None] * inv[None], col[:, None] * inv[None]], -1)
    return jnp.cos(ang), jnp.sin(ang)


def apply_rope(x, cos, sin):
    shp = (1, x.shape[1]) + (1,) * (x.ndim - 3) + (cos.shape[-1],)
    c = cos.reshape(shp).astype(x.dtype)
    s = sin.reshape(shp).astype(x.dtype)
    x1, x2 = x[..., 0::2], x[..., 1::2]
    return jnp.stack([x1 * c - x2 * s, x1 * s + x2 * c], -1).reshape(x.shape)


def softmax_with_sink(s, sink):
    sk = jnp.broadcast_to(sink.astype(jnp.float32)[None, :, :, None, None], s.shape[:-1] + (1,))
    return jax.nn.softmax(jnp.concatenate([s, sk], -1), -1)[..., :-1]


def ctx_gqa(q, k, v, sink=None):
    B, L = q.shape[:2]
    s = jnp.einsum('bqhgd,bkhd->bhgqk', q, k, preferred_element_type=jnp.float32) * SCALE
    p = jax.nn.softmax(s, -1) if sink is None else softmax_with_sink(s, sink)
    o = jnp.einsum('bhgqk,bkhd->bqhgd', p.astype(v.dtype), v)
    return o.reshape(B, L, -1)


def windowed_gqa_sink(q, q_nopos, k, v, k_ctx, v_ctx, sink):
    B, S, KV, G, Dh = q.shape
    nblk = S // A_BLOCK
    span = 3 * A_BLOCK
    pad = jnp.zeros((B, A_BLOCK, KV, Dh), k.dtype)
    k_pad = jnp.concatenate([pad, k, pad], 1)
    v_pad = jnp.concatenate([pad, v, pad], 1)
    rel = np.arange(span)[None, :] - np.arange(A_BLOCK)[:, None]
    band = jnp.asarray((rel >= A_BLOCK - A_WINDOW) & (rel <= A_BLOCK + A_WINDOW))
    to_blocks = lambda t: jnp.moveaxis(t.reshape(B, nblk, A_BLOCK, KV, G, Dh), 1, 0)

    def block(args):
        b, qb, qnb = args
        kb = lax.dynamic_slice_in_dim(k_pad, b * A_BLOCK, span, axis=1)
        vb = lax.dynamic_slice_in_dim(v_pad, b * A_BLOCK, span, axis=1)
        kpos = b * A_BLOCK - A_BLOCK + jnp.arange(span)
        valid = band & ((kpos >= 0) & (kpos < S))[None, :]
        s_loc = jnp.einsum('bqhgd,bkhd->bhgqk', qb, kb, preferred_element_type=jnp.float32) * SCALE
        s_loc = jnp.where(valid, s_loc, NEG_INF)
        s_ctx = jnp.einsum('bqhgd,bkhd->bhgqk', qnb, k_ctx, preferred_element_type=jnp.float32) * SCALE
        p = softmax_with_sink(jnp.concatenate([s_loc, s_ctx], -1), sink).astype(v.dtype)
        o = (jnp.einsum('bhgqk,bkhd->bqhgd', p[..., :span], vb)
             + jnp.einsum('bhgqk,bkhd->bqhgd', p[..., span:], v_ctx))
        return o.reshape(B, A_BLOCK, KV * G * Dh)

    o = lax.map(block, (jnp.arange(nblk), to_blocks(q), to_blocks(q_nopos)))
    return jnp.moveaxis(o, 0, 1).reshape(B, S, KV * G * Dh)


def neighbourhood_attn(q, k, v, k_ctx, v_ctx, rpb):
    B, S, H, Dh = q.shape
    rows = S // GRID_W
    kh, kw = min(NA_ROWS, rows), NA_COLS
    q_rows = jnp.moveaxis(q.reshape(B, rows, GRID_W, H, Dh), 1, 0)
    k_grid = k.reshape(B, rows, GRID_W, H, Dh)
    v_grid = v.reshape(B, rows, GRID_W, H, Dh)
    col = np.arange(GRID_W)
    col_start = np.clip(col - kw // 2, 0, GRID_W - kw)
    col_ok = jnp.asarray((col[None, :] >= col_start[:, None]) & (col[None, :] < col_start[:, None] + kw))
    dcol = jnp.asarray(np.clip(col[None, :] - col[:, None] + NA_COLS - 1, 0, 2 * NA_COLS - 2))
    n_loc = kh * GRID_W

    def row_block(args):
        r, qr = args
        r0 = jnp.clip(r - kh // 2, 0, rows - kh)
        kr = lax.dynamic_slice_in_dim(k_grid, r0, kh, axis=1)
        vr = lax.dynamic_slice_in_dim(v_grid, r0, kh, axis=1)
        drow = r0 + jnp.arange(kh) - r + NA_ROWS - 1
        bias = rpb[:, drow[None, :, None], dcol[:, None, :]].astype(jnp.float32)
        s = jnp.einsum('bqhd,bijhd->bhqij', qr, kr, preferred_element_type=jnp.float32) * SCALE + bias[None]
        s = jnp.where(col_ok[:, None, :], s, NEG_INF).reshape(B, H, GRID_W, n_loc)
        s_ctx = jnp.einsum('bqhd,bkhd->bhqk', qr, k_ctx, preferred_element_type=jnp.float32) * SCALE
        p = jax.nn.softmax(jnp.concatenate([s, s_ctx], -1), -1).astype(v.dtype)
        o = (jnp.einsum('bhqn,bnhd->bqhd', p[..., :n_loc], vr.reshape(B, n_loc, H, Dh))
             + jnp.einsum('bhqk,bkhd->bqhd', p[..., n_loc:], v_ctx))
        return o.reshape(B, GRID_W, H * Dh)

    o = lax.map(row_block, (jnp.arange(rows), q_rows))
    return jnp.moveaxis(o, 0, 1).reshape(B, S, H * Dh)


def dense_gqa_blocks(q, q_nopos, k, v, k_ctx, v_ctx):
    B, S, KV, G, Dh = q.shape
    nblk = S // Q_BLOCK
    to_blocks = lambda t: jnp.moveaxis(t.reshape(B, nblk, Q_BLOCK, KV, G, Dh), 1, 0)

    def block(args):
        qb, qnb = args
        s = jnp.einsum('bqhgd,bkhd->bhgqk', qb, k, preferred_element_type=jnp.float32) * SCALE
        s_ctx = jnp.einsum('bqhgd,bkhd->bhgqk', qnb, k_ctx, preferred_element_type=jnp.float32) * SCALE
        p = jax.nn.softmax(jnp.concatenate([s, s_ctx], -1), -1).astype(v.dtype)
        o = (jnp.einsum('bhgqk,bkhd->bqhgd', p[..., :S], v)
             + jnp.einsum('bhgqk,bkhd->bqhgd', p[..., S:], v_ctx))
        return o.reshape(B, Q_BLOCK, KV * G * Dh)

    o = lax.map(block, (to_blocks(q), to_blocks(q_nopos)))
    return jnp.moveaxis(o, 0, 1).reshape(B, S, KV * G * Dh)


def diff_attn_blocks(q, q_nopos, k, v, k_ctx, v_ctx, lam):
    B, S, H, _, Dh = q.shape
    nblk = S // Q_BLOCK
    to_blocks = lambda t: jnp.moveaxis(t.reshape(B, nblk, Q_BLOCK, H, 2, Dh), 1, 0)

    def block(args):
        qb, qnb = args
        s = jnp.einsum('bqhtd,bkhtd->bhtqk', qb, k, preferred_element_type=jnp.float32) * SCALE
        s_ctx = jnp.einsum('bqhtd,bkhtd->bhtqk', qnb, k_ctx, preferred_element_type=jnp.float32) * SCALE
        p = jax.nn.softmax(jnp.concatenate([s, s_ctx], -1), -1)
        a = (p[:, :, 0] - lam * p[:, :, 1]).astype(v.dtype)
        return (jnp.einsum('bhqk,bkhe->bqhe', a[..., :S], v)
                + jnp.einsum('bhqk,bkhe->bqhe', a[..., S:], v_ctx))

    o = lax.map(block, (to_blocks(q), to_blocks(q_nopos)))
    return jnp.moveaxis(o, 0, 1).reshape(B, S, H, 2 * Dh)


def ctx_diff(q, k, v, lam):
    s = jnp.einsum('bqhtd,bkhtd->bhtqk', q, k, preferred_element_type=jnp.float32) * SCALE
    p = jax.nn.softmax(s, -1)
    a = (p[:, :, 0] - lam * p[:, :, 1]).astype(v.dtype)
    return jnp.einsum('bhqk,bkhe->bqhe', a, v)


def even_mixer(h, hc, w_in, w_out, sink, rpb, cos, sin, ctx_out):
    B, S, _ = h.shape
    L = hc.shape[1]
    G = A_HEADS // A_KV_HEADS
    cuts = _cuts(EVEN_SPLITS)
    a_q, a_k, a_v, b_q, b_k, b_v = jnp.split(h @ w_in, cuts, axis=-1)
    a_qx, a_kx, a_vx, b_qx, b_kx, b_vx = jnp.split(hc @ w_in, cuts, axis=-1)
    sink = sink.reshape(A_KV_HEADS, G)
    a_q = a_q.reshape(B, S, A_KV_HEADS, G, HEAD_DIM)
    a_k = a_k.reshape(B, S, A_KV_HEADS, HEAD_DIM)
    a_v = a_v.reshape(B, S, A_KV_HEADS, HEAD_DIM)
    a_kx = a_kx.reshape(B, L, A_KV_HEADS, HEAD_DIM)
    a_vx = a_vx.reshape(B, L, A_KV_HEADS, HEAD_DIM)
    o_a = windowed_gqa_sink(apply_rope(a_q, cos, sin), a_q, apply_rope(a_k, cos, sin), a_v, a_kx, a_vx, sink)
    b_kx = b_kx.reshape(B, L, B_HEADS, HEAD_DIM)
    b_vx = b_vx.reshape(B, L, B_HEADS, HEAD_DIM)
    o_b = neighbourhood_attn(b_q.reshape(B, S, B_HEADS, HEAD_DIM), b_k.reshape(B, S, B_HEADS, HEAD_DIM),
                             b_v.reshape(B, S, B_HEADS, HEAD_DIM), b_kx, b_vx, rpb)
    y = jnp.concatenate([o_a, o_b], -1) @ w_out
    if not ctx_out:
        return y, None
    oc_a = ctx_gqa(a_qx.reshape(B, L, A_KV_HEADS, G, HEAD_DIM), a_kx, a_vx, sink)
    oc_b = ctx_gqa(b_qx.reshape(B, L, B_HEADS, 1, HEAD_DIM), b_kx, b_vx)
    yc = jnp.concatenate([oc_a, oc_b], -1) @ w_out
    return y, yc


def odd_mixer(h, hc, w_in, w_out, q_norm_g, k_norm_g, lambda_q1, lambda_k1, lambda_q2, lambda_k2,
              subln_g, lambda_init, cos, sin, ctx_out):
    B, S, _ = h.shape
    L = hc.shape[1]
    G = C_HEADS // C_KV_HEADS
    cuts = _cuts(ODD_SPLITS)
    c_q, c_k, c_v, d_q, d_k, d_v = jnp.split(h @ w_in, cuts, axis=-1)
    c_qx, c_kx, c_vx, d_qx, d_kx, d_vx = jnp.split(hc @ w_in, cuts, axis=-1)
    q = rms_norm(c_q.reshape(B, S, C_KV_HEADS, G, HEAD_DIM), q_norm_g)
    k = rms_norm(c_k.reshape(B, S, C_KV_HEADS, HEAD_DIM), k_norm_g)
    v = c_v.reshape(B, S, C_KV_HEADS, HEAD_DIM)
    k_x = rms_norm(c_kx.reshape(B, L, C_KV_HEADS, HEAD_DIM), k_norm_g)
    v_x = c_vx.reshape(B, L, C_KV_HEADS, HEAD_DIM)
    o_c = dense_gqa_blocks(apply_rope(q, cos, sin), q, apply_rope(k, cos, sin), v, k_x, v_x)
    lam = (jnp.exp(jnp.sum(lambda_q1.astype(jnp.float32) * lambda_k1.astype(jnp.float32)))
           - jnp.exp(jnp.sum(lambda_q2.astype(jnp.float32) * lambda_k2.astype(jnp.float32))) + lambda_init)
    dq = d_q.reshape(B, S, D_HEADS, 2, HEAD_DIM)
    dk = d_k.reshape(B, S, D_HEADS, 2, HEAD_DIM)
    dv = d_v.reshape(B, S, D_HEADS, 2 * HEAD_DIM)
    dk_x = d_kx.reshape(B, L, D_HEADS, 2, HEAD_DIM)
    dv_x = d_vx.reshape(B, L, D_HEADS, 2 * HEAD_DIM)
    o_d = diff_attn_blocks(apply_rope(dq, cos, sin), dq, apply_rope(dk, cos, sin), dv, dk_x, dv_x, lam)
    o_d = rms_norm(o_d, subln_g) * (1.0 - lambda_init)
    y = jnp.concatenate([o_c, o_d.reshape(B, S, -1)], -1) @ w_out
    if not ctx_out:
        return y, None
    q_x = rms_norm(c_qx.reshape(B, L, C_KV_HEADS, G, HEAD_DIM), q_norm_g)
    oc_c = ctx_gqa(q_x, k_x, v_x)
    oc_d = rms_norm(ctx_diff(d_qx.reshape(B, L, D_HEADS, 2, HEAD_DIM), dk_x, dv_x, lam), subln_g) * (1.0 - lambda_init)
    yc = jnp.concatenate([oc_c, oc_d.reshape(B, L, -1)], -1) @ w_out
    return y, yc


def moe(h, w_router, router_bias, w_gate, w_up, w_down):
    logits = jnp.einsum('...d,de->...e', h, w_router, preferred_element_type=jnp.float32)
    scores = jax.nn.sigmoid(logits)
    biased = scores + router_bias.astype(jnp.float32)
    grouped = biased.reshape(biased.shape[:-1] + (N_GROUPS, EXPERTS_PER_GROUP))
    group_score = lax.top_k(grouped, TOP_K)[0].sum(-1)
    best = jnp.argmax(group_score, -1)
    in_group = best[..., None] == jnp.arange(N_GROUPS)
    masked = jnp.where(in_group[..., None], grouped, -jnp.inf).reshape(biased.shape)
    _, idx = lax.top_k(masked, TOP_K)
    w = jnp.take_along_axis(scores, idx, -1)
    w = w / jnp.sum(w, -1, keepdims=True)
    gates = jnp.sum(jax.nn.one_hot(idx, N_EXPERTS, dtype=jnp.float32) * w[..., None], -2).astype(h.dtype)
    out = jnp.zeros_like(h)
    for e in range(N_EXPERTS):
        a = jax.nn.silu(h @ w_gate[e]) * (h @ w_up[e])
        out = out + gates[..., e:e + 1] * (a @ w_down[e])
    return out


def setup_inputs(seed: int = 0) -> dict:
    key = jax.random.key(seed)
    ks = jax.random.split(key, 28)

    def nrm(i, shape, scale):
        return jax.random.normal(ks[i], shape, jnp.float32) * scale

    n_even = (DEPTH + 1) // 2
    n_odd = DEPTH // 2
    return {
        'x': nrm(0, (BATCH, SEQ, D_MODEL), 1.0),
        'c': nrm(1, (BATCH, D_MODEL), 1.0),
        'ctx': nrm(2, (BATCH, CTX_LEN, D_MODEL), 1.0),
        'c_ctx': nrm(3, (D_MODEL,), 1.0),
        'w_ada': nrm(4, (DEPTH, D_MODEL, 6 * D_MODEL), 0.5 * D_MODEL ** -0.5),
        'b_ada': nrm(5, (DEPTH, 6 * D_MODEL), 0.02),
        'ln1_g': 1.0 + nrm(6, (DEPTH, D_MODEL), 0.02),
        'ln1_b': nrm(7, (DEPTH, D_MODEL), 0.02),
        'ln2_g': 1.0 + nrm(8, (DEPTH, D_MODEL), 0.02),
        'ln2_b': nrm(9, (DEPTH, D_MODEL), 0.02),
        'w_in_even': nrm(10, (n_even, D_MODEL, sum(EVEN_SPLITS)), D_MODEL ** -0.5),
        'w_out_even': nrm(11, (n_even, EVEN_OUT, D_MODEL), BETA * EVEN_OUT ** -0.5),
        'sink_logits': nrm(12, (n_even, A_HEADS), 0.5),
        'na_rpb': nrm(13, (n_even, B_HEADS, 2 * NA_ROWS - 1, 2 * NA_COLS - 1), 0.1),
        'w_in_odd': nrm(14, (n_odd, D_MODEL, sum(ODD_SPLITS)), D_MODEL ** -0.5),
        'w_out_odd': nrm(15, (n_odd, ODD_OUT, D_MODEL), BETA * ODD_OUT ** -0.5),
        'q_norm_g': 1.0 + nrm(16, (n_odd, HEAD_DIM), 0.02),
        'k_norm_g': 1.0 + nrm(17, (n_odd, HEAD_DIM), 0.02),
        'lambda_q1': nrm(18, (n_odd, HEAD_DIM), 0.1),
        'lambda_k1': nrm(19, (n_odd, HEAD_DIM), 0.1),
        'lambda_q2': nrm(20, (n_odd, HEAD_DIM), 0.1),
        'lambda_k2': nrm(21, (n_odd, HEAD_DIM), 0.1),
        'subln_g': 1.0 + nrm(22, (n_odd, 2 * HEAD_DIM), 0.02),
        'w_router': nrm(23, (D_MODEL, N_EXPERTS), D_MODEL ** -0.5),
        'router_bias': nrm(24, (N_EXPERTS,), 0.01),
        'w_exp_gate': nrm(25, (DEPTH, N_EXPERTS, D_MODEL, EXPERT_FF), D_MODEL ** -0.5),
        'w_exp_up': nrm(26, (DEPTH, N_EXPERTS, D_MODEL, EXPERT_FF), D_MODEL ** -0.5),
        'w_exp_down': nrm(27, (DEPTH, N_EXPERTS, EXPERT_FF, D_MODEL), BETA * EXPERT_FF ** -0.5),
    }


def reference(x, c, ctx, c_ctx, w_ada, b_ada, ln1_g, ln1_b, ln2_g, ln2_b,
              w_in_even, w_out_even, sink_logits, na_rpb,
              w_in_odd, w_out_odd, q_norm_g, k_norm_g,
              lambda_q1, lambda_k1, lambda_q2, lambda_k2, subln_g,
              w_router, router_bias, w_exp_gate, w_exp_up, w_exp_down):
    S = x.shape[1]
    cos, sin = axial_rope(S)
    silu_c = jax.nn.silu(c)
    silu_cc = jax.nn.silu(c_ctx)
    for l in range(DEPTH):
        last = l == DEPTH - 1
        j = l // 2
        mod = silu_c @ w_ada[l] + b_ada[l]
        sh1, sc1, g1, sh2, sc2, g2 = [m[:, None, :] for m in jnp.split(mod, 6, axis=-1)]
        modc = silu_cc @ w_ada[l] + b_ada[l]
        csh1, csc1, cg1, csh2, csc2, cg2 = jnp.split(modc, 6)
        h = x * (1.0 + sc1) + sh1
        hc = ctx * (1.0 + csc1) + csh1
        if l % 2 == 0:
            y, yc = even_mixer(h, hc, w_in_even[j], w_out_even[j], sink_logits[j], na_rpb[j], cos, sin, not last)
        else:
            lambda_init = 0.8 - 0.6 * math.exp(-0.3 * l)
            y, yc = odd_mixer(h, hc, w_in_odd[j], w_out_odd[j], q_norm_g[j], k_norm_g[j],
                              lambda_q1[j], lambda_k1[j], lambda_q2[j], lambda_k2[j], subln_g[j],
                              lambda_init, cos, sin, not last)
        x = layer_norm(ALPHA * x + g1 * y, ln1_g[l], ln1_b[l])
        f = moe(x * (1.0 + sc2) + sh2, w_router, router_bias, w_exp_gate[l], w_exp_up[l], w_exp_down[l])
        x = layer_norm(ALPHA * x + g2 * f, ln2_g[l], ln2_b[l])
        if not last:
            ctx = layer_norm(ALPHA * ctx + cg1 * yc, ln1_g[l], ln1_b[l])
            fc = moe(ctx * (1.0 + csc2) + csh2, w_router, router_bias, w_exp_gate[l], w_exp_up[l], w_exp_down[l])
            ctx = layer_norm(ALPHA * ctx + cg2 * fc, ln2_g[l], ln2_b[l])
    return x
```

```python
import functools
import math

import numpy as np
import jax
import jax.numpy as jnp
from jax import lax
from jax.experimental import pallas as pl
from jax.experimental.pallas import tpu as pltpu

F32 = jnp.float32
BF16 = jnp.bfloat16

HEAD_DIM = 128
GRID_W = 64
ROPE_THETA = 10000.0
SCALE = HEAD_DIM ** -0.5
NEG_INF = -1e30
A_WINDOW = 128
A_BLOCK = 128
NA_ROWS = 8
NA_COLS = 16
B_TILE_ROWS = 4
N_EXPERTS = 16
N_GROUPS = 4
EXPERTS_PER_GROUP = N_EXPERTS // N_GROUPS
LN_EPS = 1e-5
RMS_EPS = 1e-6

V7X_VMEM_BYTES = 64 * 1024 * 1024
VMEM_LIMIT = V7X_VMEM_BYTES - 8 * 1024 * 1024
LANES = 128

NT_DIMS = (((1,), (1,)), ((), ()))


def _params(*sem):
    return pltpu.CompilerParams(dimension_semantics=sem, vmem_limit_bytes=VMEM_LIMIT)


def _adaln_kernel(c_ref, w_ref, b_ref, o_ref):
    c = c_ref[...]
    a = (c * (1.0 / (1.0 + jnp.exp(-c)))).astype(BF16)
    o_ref[...] = jnp.dot(a, w_ref[...].astype(BF16), preferred_element_type=F32) + b_ref[...]


def _adaln(c_rows, w_ada, b_ada):
    depth, d, n6 = w_ada.shape
    rows = c_rows.shape[0]
    tn = 1024
    return pl.pallas_call(
        _adaln_kernel,
        out_shape=jax.ShapeDtypeStruct((depth, rows, n6), F32),
        grid=(depth, n6 // tn),
        in_specs=[
            pl.BlockSpec((rows, d), lambda l, j: (0, 0)),
            pl.BlockSpec((None, d, tn), lambda l, j: (l, 0, j)),
            pl.BlockSpec((None, 1, tn), lambda l, j: (l, 0, j)),
        ],
        out_specs=pl.BlockSpec((None, rows, tn), lambda l, j: (l, 0, j)),
        compiler_params=_params("parallel", "parallel"),
        name="adaln",
    )(c_rows, w_ada, b_ada.reshape(depth, 1, n6))


def _modulate_kernel(x_ref, sc_ref, sh_ref, o_ref):
    o_ref[...] = (x_ref[...] * (1.0 + sc_ref[0]) + sh_ref[0]).astype(o_ref.dtype)


def _modulate(x, sc, sh, *, seq, n_groups):
    rows, d = x.shape
    tm = 512
    grp = lambda i: (jnp.minimum(i // (seq // tm), n_groups - 1), 0, 0)
    return pl.pallas_call(
        _modulate_kernel,
        out_shape=jax.ShapeDtypeStruct((rows, d), BF16),
        grid=(rows // tm,),
        in_specs=[pl.BlockSpec((tm, d), lambda i: (i, 0)),
                  pl.BlockSpec((1, 1, d), grp), pl.BlockSpec((1, 1, d), grp)],
        out_specs=pl.BlockSpec((tm, d), lambda i: (i, 0)),
        compiler_params=_params("parallel"),
        name="modulate",
    )(x, sc, sh)


def _proj_kernel(*refs, tn, norm, rope, emit_nopos):
    it = iter(refs)
    x_ref, w_ref, cs_ref = next(it), next(it), next(it)
    g_ref = next(it) if norm else None
    cos_ref, sin_ref = (next(it), next(it)) if rope else (None, None)
    o_ref = next(it)
    n_ref = next(it) if emit_nopos else None
    acc = jnp.dot(x_ref[...], w_ref[...], preferred_element_type=F32)
    for hd in range(tn // HEAD_DIM):
        sl = slice(hd * HEAD_DIM, (hd + 1) * HEAD_DIM)
        xh = acc[:, sl]
        if norm:
            xh = xh * lax.rsqrt(jnp.mean(xh * xh, axis=-1, keepdims=True) + RMS_EPS) * g_ref[...]
        cs = cs_ref[:, sl]
        if rope:
            rot = xh * cos_ref[...] + pltpu.roll(xh, HEAD_DIM // 2, 1) * sin_ref[...]
            o_ref[:, sl] = (rot * cs).astype(o_ref.dtype)
            if emit_nopos:
                n_ref[:, sl] = (xh * cs).astype(n_ref.dtype)
        else:
            o_ref[:, sl] = (xh * cs).astype(o_ref.dtype)


def _proj(x, w, col_scale, *, n_rows, seq, n_latent, tn=None, gain=None, rope_tabs=None,
          emit_nopos=False, name="proj"):
    d = x.shape[1]
    nc = w.shape[1]
    tm = min(1024, seq)
    tn = nc if tn is None else tn
    norm, rope = gain is not None, rope_tabs is not None
    in_specs = [pl.BlockSpec((tm, d), lambda i, j: (i, 0)),
                pl.BlockSpec((d, tn), lambda i, j: (0, j)),
                pl.BlockSpec((1, tn), lambda i, j: (0, j))]
    args = [x, w, col_scale]
    if norm:
        in_specs.append(pl.BlockSpec((1, HEAD_DIM), lambda i, j: (0, 0)))
        args.append(gain)
    if rope:
        per_seq = seq // tm
        tab = lambda i, j: (jnp.where(i < n_latent // tm, i % per_seq, per_seq), 0)
        in_specs += [pl.BlockSpec((tm, HEAD_DIM), tab)] * 2
        args += list(rope_tabs)
    n_out = 2 if emit_nopos else 1
    out_shape = [jax.ShapeDtypeStruct((n_rows, nc), BF16)] * n_out
    out_specs = [pl.BlockSpec((tm, tn), lambda i, j: (i, j))] * n_out
    res = pl.pallas_call(
        functools.partial(_proj_kernel, tn=tn, norm=norm, rope=rope, emit_nopos=emit_nopos),
        out_shape=out_shape,
        grid=(n_rows // tm, nc // tn),
        in_specs=in_specs,
        out_specs=out_specs,
        compiler_params=_params("parallel", "parallel"),
        name=name,
    )(*args)
    return res if emit_nopos else res[0]


def _stack_heads(x, g):
    if g == 1:
        return x
    return jnp.concatenate([x[:, i * HEAD_DIM:(i + 1) * HEAD_DIM] for i in range(g)], axis=0)


def _unstack_store(o_ref, o, g, t):
    for i in range(g):
        o_ref[:, i * HEAD_DIM:(i + 1) * HEAD_DIM] = o[i * t:(i + 1) * t].astype(o_ref.dtype)


def _sink_column(sink_ref, first_head, g, t):
    cols = [jnp.full((t, 1), sink_ref[first_head + i], F32) for i in range(g)]
    return cols[0] if g == 1 else jnp.concatenate(cols, axis=0)


def _online_update(s, v, m, l, acc):
    m_new = jnp.maximum(m, jnp.max(s, axis=-1, keepdims=True))
    alpha = jnp.exp(m - m_new)
    p = jnp.exp(s - m_new)
    l = alpha * l + jnp.sum(p, axis=-1, keepdims=True)
    acc = alpha * acc + jnp.dot(p.astype(BF16), v, preferred_element_type=F32)
    return m_new, l, acc


def _band_kernel(*refs, g, bq, has_sink):
    it = iter(refs)
    sink_ref = next(it) if has_sink else None
    q_ref, qc_ref = next(it), next(it)
    k_refs = [next(it) for _ in range(3)]
    v_refs = [next(it) for _ in range(3)]
    kx_ref, vx_ref, bias_ref, o_ref = next(it), next(it), next(it), next(it)
    q = _stack_heads(q_ref[...], g)
    qc = _stack_heads(qc_ref[...], g)
    s = [lax.dot_general(q, k_refs[j][...], NT_DIMS, preferred_element_type=F32)
         + bias_ref[:, j * bq:(j + 1) * bq] for j in range(3)]
    s.append(lax.dot_general(qc, kx_ref[...], NT_DIMS, preferred_element_type=F32))
    vs = [r[...] for r in v_refs] + [vx_ref[...]]
    m = functools.reduce(jnp.maximum, [jnp.max(sj, axis=-1, keepdims=True) for sj in s])
    if has_sink:
        sink = _sink_column(sink_ref, pl.program_id(1) * g, g, bq)
        m = jnp.maximum(m, sink)
    p = [jnp.exp(sj - m) for sj in s]
    l = functools.reduce(jnp.add, [jnp.sum(pj, axis=-1, keepdims=True) for pj in p])
    if has_sink:
        l = l + jnp.exp(sink - m)
    o = functools.reduce(jnp.add, [jnp.dot(pj.astype(BF16), vj, preferred_element_type=F32)
                                   for pj, vj in zip(p, vs)])
    _unstack_store(o_ref, o / l, g, bq)


def _band_attn(q, qc, k, v, bias, sink, *, n_batch, seq, ctx_len, n_latent, n_kv, g, bq,
               q_col, k_col, v_col, name):
    nblk = seq // bq
    ctx_blk = n_latent // ctx_len
    per_head_bias = bias.shape[0] > 1
    has_sink = sink is not None

    def q_map(b, h, i):
        return (b * nblk + i, q_col + h)

    def kv_map(col, off):
        return lambda b, h, i: (b * nblk + jnp.clip(i + off, 0, nblk - 1), col + h)

    def ctx_map(col):
        return lambda b, h, i: (ctx_blk + b, col + h)

    def bias_map(b, h, i):
        case = jnp.where(i == 0, 0, jnp.where(i == nblk - 1, 2, 1))
        return (h if per_head_bias else 0, case, 0, 0)

    in_specs, args = [], []
    if has_sink:
        in_specs.append(pl.BlockSpec(memory_space=pltpu.SMEM))
        args.append(sink)
    in_specs += [pl.BlockSpec((bq, g * HEAD_DIM), q_map)] * 2
    args += [q, qc]
    in_specs += [pl.BlockSpec((bq, HEAD_DIM), kv_map(k_col, off)) for off in (-1, 0, 1)]
    args += [k] * 3
    in_specs += [pl.BlockSpec((bq, HEAD_DIM), kv_map(v_col, off)) for off in (-1, 0, 1)]
    args += [v] * 3
    in_specs += [pl.BlockSpec((ctx_len, HEAD_DIM), ctx_map(k_col)),
                 pl.BlockSpec((ctx_len, HEAD_DIM), ctx_map(v_col)),
                 pl.BlockSpec((None, None, g * bq, 3 * bq), bias_map)]
    args += [k, v, bias]
    return pl.pallas_call(
        functools.partial(_band_kernel, g=g, bq=bq, has_sink=has_sink),
        out_shape=jax.ShapeDtypeStruct((n_latent + n_batch * ctx_len, n_kv * g * HEAD_DIM), BF16),
        grid=(n_batch, n_kv, nblk),
        in_specs=in_specs,
        out_specs=pl.BlockSpec((bq, g * HEAD_DIM), lambda b, h, i: (b * nblk + i, h)),
        compiler_params=_params("parallel", "parallel", "parallel"),
        name=name,
    )(*args)


def _dense_kernel(*refs, g, tq, tk, n_k, has_ctx, has_sink, aliased):
    it = iter(refs)
    sink_ref = next(it) if has_sink else None
    q_ref = next(it)
    qc_ref = next(it) if has_ctx else None
    k_ref, v_ref = next(it), next(it)
    kx_ref, vx_ref = (next(it), next(it)) if has_ctx else (None, None)
    if aliased:
        next(it)
    o_ref = next(it)
    q = _stack_heads(q_ref[...], g)
    rows = g * tq
    m = jnp.full((rows, 1), NEG_INF, F32)
    l = jnp.zeros((rows, 1), F32)
    acc = jnp.zeros((rows, HEAD_DIM), F32)
    for c in range(n_k // tk):
        s = lax.dot_general(q, k_ref[c * tk:(c + 1) * tk, :], NT_DIMS, preferred_element_type=F32)
        m, l, acc = _online_update(s, v_ref[c * tk:(c + 1) * tk, :], m, l, acc)
    if has_ctx:
        qc = _stack_heads(qc_ref[...], g)
        s = lax.dot_general(qc, kx_ref[...], NT_DIMS, preferred_element_type=F32)
        m, l, acc = _online_update(s, vx_ref[...], m, l, acc)
    if has_sink:
        sink = _sink_column(sink_ref, pl.program_id(1) * g, g, tq)
        m_new = jnp.maximum(m, sink)
        alpha = jnp.exp(m - m_new)
        l = alpha * l + jnp.exp(sink - m_new)
        acc = alpha * acc
    _unstack_store(o_ref, acc / l, g, tq)


def _dense_attn(q, qc, k, v, sink, *, n_batch, n_q, q_row0, n_k, k_row0, ctx_len, ctx_row0,
                n_kv, g, q_col, k_col, v_col, out_rows, out_into=None, name):
    tq = min(256, n_q)
    tk = min(512, n_k)
    nq_blk = n_q // tq
    has_ctx, has_sink, aliased = qc is not None, sink is not None, out_into is not None

    def q_map(b, h, i):
        return (q_row0 // tq + b * nq_blk + i, q_col + h)

    in_specs, args = [], []
    if has_sink:
        in_specs.append(pl.BlockSpec(memory_space=pltpu.SMEM))
        args.append(sink)
    in_specs.append(pl.BlockSpec((tq, g * HEAD_DIM), q_map))
    args.append(q)
    if has_ctx:
        in_specs.append(pl.BlockSpec((tq, g * HEAD_DIM), q_map))
        args.append(qc)
    in_specs += [pl.BlockSpec((n_k, HEAD_DIM), lambda b, h, i: (k_row0 // n_k + b, k_col + h)),
                 pl.BlockSpec((n_k, HEAD_DIM), lambda b, h, i: (k_row0 // n_k + b, v_col + h))]
    args += [k, v]
    if has_ctx:
        in_specs += [pl.BlockSpec((ctx_len, HEAD_DIM), lambda b, h, i: (ctx_row0 // ctx_len + b, k_col + h)),
                     pl.BlockSpec((ctx_len, HEAD_DIM), lambda b, h, i: (ctx_row0 // ctx_len + b, v_col + h))]
        args += [k, v]
    aliases = {}
    if aliased:
        in_specs.append(pl.BlockSpec(memory_space=pl.ANY))
        aliases = {len(args): 0}
        args.append(out_into)
    return pl.pallas_call(
        functools.partial(_dense_kernel, g=g, tq=tq, tk=tk, n_k=n_k, has_ctx=has_ctx,
                          has_sink=has_sink, aliased=aliased),
        out_shape=jax.ShapeDtypeStruct((out_rows, n_kv * g * HEAD_DIM), BF16),
        grid=(n_batch, n_kv, nq_blk),
        in_specs=in_specs,
        out_specs=pl.BlockSpec((tq, g * HEAD_DIM), q_map_out(q_row0, tq, nq_blk)),
        input_output_aliases=aliases,
        compiler_params=_params("parallel", "parallel", "parallel"),
        name=name,
    )(*args)


def q_map_out(q_row0, tq, nq_blk):
    return lambda b, h, i: (q_row0 // tq + b * nq_blk + i, h)


def _diff_kernel(lam_ref, g_ref, q_ref, qc_ref, k_ref, v_ref, kx_ref, vx_ref, o_ref, *,
                 tq, tk, n_k, lambda_init):
    dv = 2 * HEAD_DIM
    lam = (jnp.exp(jnp.sum(lam_ref[0:1, :] * lam_ref[1:2, :], axis=1, keepdims=True))
           - jnp.exp(jnp.sum(lam_ref[2:3, :] * lam_ref[3:4, :], axis=1, keepdims=True)) + lambda_init)
    outs = []
    for t in range(2):
        sl = slice(t * HEAD_DIM, (t + 1) * HEAD_DIM)
        q = q_ref[:, sl]
        m = jnp.full((tq, 1), NEG_INF, F32)
        l = jnp.zeros((tq, 1), F32)
        acc = jnp.zeros((tq, dv), F32)
        for c in range(n_k // tk):
            rows = slice(c * tk, (c + 1) * tk)
            s = lax.dot_general(q, k_ref[rows, sl], NT_DIMS, preferred_element_type=F32)
            m, l, acc = _online_update(s, v_ref[rows, :], m, l, acc)
        s = lax.dot_general(qc_ref[:, sl], kx_ref[:, sl], NT_DIMS, preferred_element_type=F32)
        m, l, acc = _online_update(s, vx_ref[...], m, l, acc)
        outs.append(acc / l)
    o = outs[0] - lam * outs[1]
    o = o * lax.rsqrt(jnp.mean(o * o, axis=-1, keepdims=True) + RMS_EPS) * g_ref[...]
    o_ref[...] = (o * (1.0 - lambda_init)).astype(o_ref.dtype)


def _diff_attn(q, qc, k, v, lam_vecs, subln_g, *, n_batch, seq, ctx_len, n_latent, n_heads,
               v_col, lambda_init):
    tq = min(512, seq)
    tk = min(512, seq)
    dv = 2 * HEAD_DIM
    nq_blk = seq // tq
    ctx_blk = n_latent // ctx_len
    q_map = lambda b, h, i: (b * nq_blk + i, h)
    return pl.pallas_call(
        functools.partial(_diff_kernel, tq=tq, tk=tk, n_k=seq, lambda_init=lambda_init),
        out_shape=jax.ShapeDtypeStruct((n_latent, n_heads * dv), BF16),
        grid=(n_batch, n_heads, nq_blk),
        in_specs=[
            pl.BlockSpec((4, HEAD_DIM), lambda b, h, i: (0, 0)),
            pl.BlockSpec((1, dv), lambda b, h, i: (0, 0)),
            pl.BlockSpec((tq, dv), q_map),
            pl.BlockSpec((tq, dv), q_map),
            pl.BlockSpec((seq, dv), lambda b, h, i: (b, h)),
            pl.BlockSpec((seq, dv), lambda b, h, i: (b, v_col + h)),
            pl.BlockSpec((ctx_len, dv), lambda b, h, i: (ctx_blk + b, h)),
            pl.BlockSpec((ctx_len, dv), lambda b, h, i: (ctx_blk + b, v_col + h)),
        ],
        out_specs=pl.BlockSpec((tq, dv), q_map),
        compiler_params=_params("parallel", "parallel", "parallel"),
        name="diff_attn",
    )(lam_vecs, subln_g, q, qc, k, v, k, v)


def _layer_norm(z, g, b):
    mu = jnp.mean(z, axis=-1, keepdims=True)
    zc = z - mu
    var = jnp.mean(zc * zc, axis=-1, keepdims=True)
    return zc * lax.rsqrt(var + LN_EPS) * g + b


def _split_bf16(x):
    hi = x.astype(BF16)
    return hi, (x - hi.astype(F32)).astype(BF16)


def _first_argmax(vals, lane, width):
    m = jnp.max(vals, axis=-1, keepdims=True)
    idx = jnp.min(jnp.where(vals == m, lane, float(width)), axis=-1, keepdims=True)
    return m, idx


def _route(logits, bias):
    t, e = logits.shape
    scores = 1.0 / (1.0 + jnp.exp(-logits))
    biased = scores + bias
    lane_i = lax.broadcasted_iota(jnp.int32, (t, e), 1)
    grp = lax.shift_right_logical(lane_i, int(math.log2(EXPERTS_PER_GROUP)))
    lane = lane_i.astype(F32)
    neg = -jnp.inf
    best_score, best = None, None
    for gi in range(N_GROUPS):
        vg = jnp.where(grp == gi, biased, neg)
        m1, i1 = _first_argmax(vg, lane, e)
        m2 = jnp.max(jnp.where(lane == i1, neg, vg), axis=-1, keepdims=True)
        gs = m1 + m2
        if gi == 0:
            best_score, best = gs, jnp.zeros((t, 1), jnp.int32)
        else:
            upd = gs > best_score
            best = jnp.where(upd, gi, best)
            best_score = jnp.where(upd, gs, best_score)
    masked = jnp.where(grp == best, biased, neg)
    _, e1 = _first_argmax(masked, lane, e)
    _, e2 = _first_argmax(jnp.where(lane == e1, neg, masked), lane, e)
    w1 = jnp.sum(jnp.where(lane == e1, scores, 0.0), axis=-1, keepdims=True)
    w2 = jnp.sum(jnp.where(lane == e2, scores, 0.0), axis=-1, keepdims=True)
    den = w1 + w2
    two = lax.broadcasted_iota(jnp.int32, (t, 2), 1)
    return jnp.where(two == 0, e1, e2).astype(jnp.int32), jnp.where(two == 0, w1 / den, w2 / den)


def _outproj_kernel(xa_ref, xb_ref, wa_ref, wb_ref, xres_ref, g1_ref, sc2_ref, sh2_ref,
                    lng_ref, lnb_ref, wr_ref, rb_ref, x1_ref, h2_ref, idx_ref, wts_ref, *, alpha):
    y = (jnp.dot(xa_ref[...], wa_ref[...], preferred_element_type=F32)
         + jnp.dot(xb_ref[...], wb_ref[...], preferred_element_type=F32))
    x1 = _layer_norm(alpha * xres_ref[...] + g1_ref[0] * y, lng_ref[...], lnb_ref[...])
    x1_ref[...] = x1
    h2 = x1 * (1.0 + sc2_ref[0]) + sh2_ref[0]
    h2_ref[...] = h2.astype(h2_ref.dtype)
    h_hi, h_lo = _split_bf16(h2)
    w_hi, w_lo = _split_bf16(wr_ref[...])
    logits = (jnp.dot(h_hi, w_hi, preferred_element_type=F32)
              + jnp.dot(h_lo, w_hi, preferred_element_type=F32)
              + jnp.dot(h_hi, w_lo, preferred_element_type=F32))
    idx, wts = _route(logits, rb_ref[...])
    idx_ref[...] = idx
    wts_ref[...] = wts


def _outproj_ln(xa, xb, wa, wb, xres, g1, sc2, sh2, ln_g, ln_b, w_router, router_bias, *,
                n_rows, seq, n_groups, alpha):
    d = xres.shape[1]
    ka, kb = xa.shape[1], xb.shape[1]
    tm = 256
    row = lambda i: (i, 0)
    const = lambda i: (0, 0)
    grp = lambda i: (jnp.minimum(i // (seq // tm), n_groups - 1), 0, 0)
    return pl.pallas_call(
        functools.partial(_outproj_kernel, alpha=alpha),
        out_shape=[jax.ShapeDtypeStruct((n_rows, d), F32), jax.ShapeDtypeStruct((n_rows, d), BF16),
                   jax.ShapeDtypeStruct((n_rows, 2), jnp.int32), jax.ShapeDtypeStruct((n_rows, 2), F32)],
        grid=(n_rows // tm,),
        in_specs=[pl.BlockSpec((tm, ka), row), pl.BlockSpec((tm, kb), row),
                  pl.BlockSpec((ka, d), const), pl.BlockSpec((kb, d), const),
                  pl.BlockSpec((tm, d), row),
                  pl.BlockSpec((1, 1, d), grp), pl.BlockSpec((1, 1, d), grp), pl.BlockSpec((1, 1, d), grp),
                  pl.BlockSpec((1, d), const), pl.BlockSpec((1, d), const),
                  pl.BlockSpec((d, N_EXPERTS), const), pl.BlockSpec((1, N_EXPERTS), const)],
        out_specs=[pl.BlockSpec((tm, d), row), pl.BlockSpec((tm, d), row),
                   pl.BlockSpec((tm, 2), row), pl.BlockSpec((tm, 2), row)],
        compiler_params=_params("parallel"),
        name="outproj_ln",
    )(xa, xb, wa, wb, xres, g1, sc2, sh2, ln_g, ln_b, w_router, router_bias)


def _moe_kernel(te_ref, nv_ref, x_ref, gw_ref, wg_ref, wu_ref, wd_ref, o_ref):
    i = pl.program_id(0)

    @pl.when(i < nv_ref[0])
    def _():
        x = x_ref[...]
        gate = jnp.dot(x, wg_ref[...], preferred_element_type=F32)
        up = jnp.dot(x, wu_ref[...], preferred_element_type=F32)
        act = (gate * (1.0 / (1.0 + jnp.exp(-gate))) * up).astype(BF16)
        y = jnp.dot(act, wd_ref[...], preferred_element_type=F32)
        o_ref[...] = (y * gw_ref[...]).astype(o_ref.dtype)

    @pl.when(i >= nv_ref[0])
    def _():
        o_ref[...] = jnp.zeros_like(o_ref)


def _moe_ffn(xs, gw, tile_expert, n_valid, w_gate, w_up, w_down, *, tm):
    rows, d = xs.shape
    ff = w_gate.shape[2]
    return pl.pallas_call(
        _moe_kernel,
        out_shape=jax.ShapeDtypeStruct((rows, d), BF16),
        grid_spec=pltpu.PrefetchScalarGridSpec(
            num_scalar_prefetch=2,
            grid=(rows // tm,),
            in_specs=[pl.BlockSpec((tm, d), lambda i, te, nv: (i, 0)),
                      pl.BlockSpec((tm, 1), lambda i, te, nv: (i, 0)),
                      pl.BlockSpec((None, d, ff), lambda i, te, nv: (te[i], 0, 0)),
                      pl.BlockSpec((None, d, ff), lambda i, te, nv: (te[i], 0, 0)),
                      pl.BlockSpec((None, ff, d), lambda i, te, nv: (te[i], 0, 0))],
            out_specs=pl.BlockSpec((tm, d), lambda i, te, nv: (i, 0)),
        ),
        compiler_params=_params("arbitrary"),
        name="moe_ffn",
    )(tile_expert, n_valid, xs, gw, w_gate, w_up, w_down)


def _dispatch(idx, wts, *, tm):
    t = idx.shape[0]
    n_pairs = 2 * t
    n_tiles = n_pairs // tm + N_EXPERTS
    e_flat = idx.reshape(n_pairs)
    w_flat = wts.reshape(n_pairs)
    order = jnp.argsort(e_flat, stable=True).astype(jnp.int32)
    rank = jnp.argsort(order).astype(jnp.int32)
    counts = jnp.sum(e_flat[:, None] == jnp.arange(N_EXPERTS)[None, :], axis=0).astype(jnp.int32)
    start = jnp.cumsum(counts) - counts
    padded = ((counts + tm - 1) // tm) * tm
    pend = jnp.cumsum(padded)
    pstart = pend - padded
    dest = pstart[e_flat] + rank - start[e_flat]
    pos = jnp.arange(n_tiles * tm, dtype=jnp.int32)
    e_pos = jnp.minimum(jnp.searchsorted(pend, pos, side="right"), N_EXPERTS - 1).astype(jnp.int32)
    local = pos - pstart[e_pos]
    valid = local < counts[e_pos]
    src_pair = order[jnp.clip(start[e_pos] + local, 0, n_pairs - 1)]
    src_tok = jnp.where(valid, src_pair // 2, 0)
    gw = jnp.where(valid, w_flat[src_pair], 0.0).astype(F32)
    tile_expert = e_pos[::tm]
    n_valid = (pend[-1] // tm).astype(jnp.int32).reshape(1)
    return src_tok, gw.reshape(-1, 1), dest.reshape(t, 2), tile_expert, n_valid


def _resid_ln_kernel(*refs, alpha, emit_h):
    it = iter(refs)
    x_ref, f_ref, g_ref, lng_ref, lnb_ref = (next(it) for _ in range(5))
    sc_ref, sh_ref = (next(it), next(it)) if emit_h else (None, None)
    o_ref = next(it)
    x2 = _layer_norm(alpha * x_ref[...] + g_ref[0] * f_ref[...].astype(F32), lng_ref[...], lnb_ref[...])
    o_ref[...] = x2
    if emit_h:
        h_ref = next(it)
        h_ref[...] = (x2 * (1.0 + sc_ref[0]) + sh_ref[0]).astype(h_ref.dtype)


def _resid_ln(x, f, gate, ln_g, ln_b, next_sc, next_sh, *, n_rows, seq, n_groups, alpha):
    d = x.shape[1]
    tm = 512
    emit_h = next_sc is not None
    row = lambda i: (i, 0)
    const = lambda i: (0, 0)
    grp = lambda i: (jnp.minimum(i // (seq // tm), n_groups - 1), 0, 0)
    in_specs = [pl.BlockSpec((tm, d), row), pl.BlockSpec((tm, d), row), pl.BlockSpec((1, 1, d), grp),
                pl.BlockSpec((1, d), const), pl.BlockSpec((1, d), const)]
    args = [x, f, gate, ln_g, ln_b]
    out_shape = [jax.ShapeDtypeStruct((n_rows, d), F32)]
    out_specs = [pl.BlockSpec((tm, d), row)]
    if emit_h:
        in_specs += [pl.BlockSpec((1, 1, d), grp)] * 2
        args += [next_sc, next_sh]
        out_shape.append(jax.ShapeDtypeStruct((n_rows, d), BF16))
        out_specs.append(pl.BlockSpec((tm, d), row))
    res = pl.pallas_call(
        functools.partial(_resid_ln_kernel, alpha=alpha, emit_h=emit_h),
        out_shape=out_shape,
        grid=(n_rows // tm,),
        in_specs=in_specs,
        out_specs=out_specs,
        compiler_params=_params("parallel"),
        name="resid_ln",
    )(*args)
    return res if emit_h else (res[0], None)


def _rope_tables(seq, tm):
    t = jnp.arange(seq, dtype=jnp.int32)
    n_freq = HEAD_DIM // 4
    inv = ROPE_THETA ** (-jnp.arange(n_freq, dtype=F32) / n_freq)
    row = (t // GRID_W).astype(F32)
    col = (t % GRID_W).astype(F32)
    ang = jnp.concatenate([row[:, None] * inv[None], col[:, None] * inv[None]], -1)
    cos, sin = jnp.cos(ang), jnp.sin(ang)
    cos2 = jnp.concatenate([cos, cos], -1)
    sin2 = jnp.concatenate([-sin, sin], -1)
    cos2 = jnp.concatenate([cos2, jnp.ones((tm, HEAD_DIM), F32)], 0)
    sin2 = jnp.concatenate([sin2, jnp.zeros((tm, HEAD_DIM), F32)], 0)
    return cos2, sin2


def _window_bias(seq, g):
    bq = A_BLOCK
    nblk = seq // bq
    rel = np.arange(3 * bq)[None, :] - np.arange(bq)[:, None]
    band = (rel >= bq - A_WINDOW) & (rel <= bq + A_WINDOW)
    tabs = []
    for blk in (0, min(1, nblk - 1), nblk - 1):
        kpos = blk * bq - bq + np.arange(3 * bq)
        ok = band & ((kpos >= 0) & (kpos < seq))[None, :]
        tabs.append(np.tile(np.where(ok, 0.0, NEG_INF).astype(np.float32), (g, 1)))
    return jnp.asarray(np.stack(tabs)[None])


def _neighbourhood_bias(rpb, seq):
    rows = seq // GRID_W
    kh, kw = min(NA_ROWS, rows), NA_COLS
    tr = B_TILE_ROWS
    nblk = rows // tr
    col = np.arange(GRID_W)
    col_start = np.clip(col - kw // 2, 0, GRID_W - kw)
    col_ok = (col[None, :] >= col_start[:, None]) & (col[None, :] < col_start[:, None] + kw)
    dcol = np.clip(col[None, :] - col[:, None] + NA_COLS - 1, 0, 2 * NA_COLS - 2)
    drow_all, ok_all = [], []
    for blk in (0, min(1, nblk - 1), nblk - 1):
        r = blk * tr + np.arange(tr)
        r0 = np.clip(r - kh // 2, 0, rows - kh)
        krow = (blk - 1) * tr + np.arange(3 * tr)
        row_ok = (krow[None, :] >= r0[:, None]) & (krow[None, :] < r0[:, None] + kh)
        drow = np.clip(krow[None, :] - r[:, None] + NA_ROWS - 1, 0, 2 * NA_ROWS - 2)
        ok = row_ok[:, None, :, None] & col_ok[None, :, None, :]
        drow_all.append(np.broadcast_to(drow[:, None, :, None], ok.shape).reshape(tr * GRID_W, -1))
        ok_all.append(ok.reshape(tr * GRID_W, -1))
    dcol_full = np.broadcast_to(dcol[None, :, None, :], (tr, GRID_W, 3 * tr, GRID_W)).reshape(tr * GRID_W, -1)
    drow_idx = jnp.asarray(np.stack(drow_all).astype(np.int32))
    ok = jnp.asarray(np.stack(ok_all))
    vals = rpb.astype(F32)[:, drow_idx, jnp.asarray(dcol_full.astype(np.int32))[None]]
    return jnp.where(ok[None], vals, NEG_INF)


def _deinterleave_cols(n_heads):
    one = np.concatenate([np.arange(0, HEAD_DIM, 2), np.arange(1, HEAD_DIM, 2)])
    return np.concatenate([h * HEAD_DIM + one for h in range(n_heads)])


def kernel(x, c, ctx, c_ctx, w_ada, b_ada, ln1_g, ln1_b, ln2_g, ln2_b, w_in_even, w_out_even, sink_logits, na_rpb, w_in_odd, w_out_odd, q_norm_g, k_norm_g, lambda_q1, lambda_k1, lambda_q2, lambda_k2, subln_g, w_router, router_bias, w_exp_gate, w_exp_up, w_exp_down):
    n_batch, seq, d = x.shape
    ctx_len = ctx.shape[1]
    depth = w_ada.shape[0]
    assert depth == 2, "one even (A||B) layer followed by one odd (C||D) layer"
    n_lat = n_batch * seq
    n_ctx = n_batch * ctx_len
    n_all = n_lat + n_ctx
    n_groups = n_batch + 1
    alpha = float((2 * depth) ** 0.25)
    heads = d // HEAD_DIM
    qh = heads // 2
    kvh = qh // 4
    gq = qh // kvh
    dh = (heads - qh) // 2
    moe_tm = 512
    proj_tm = min(1024, seq)
    assert seq % GRID_W == 0 and (seq // GRID_W) % B_TILE_ROWS == 0 and seq // GRID_W >= NA_ROWS
    assert n_ctx % proj_tm == 0 and n_lat % ctx_len == 0 and (2 * n_all) % moe_tm == 0

    pad_rows = -(-n_groups // 16) * 16
    c_rows = jnp.concatenate([c, c_ctx[None], jnp.zeros((pad_rows - n_groups, d), F32)], 0)
    mod = _adaln(c_rows, w_ada, b_ada)[:, :n_groups]
    mod = mod.reshape(depth, n_groups, 6, 1, d)
    mods = [[mod[l, :, j] for j in range(6)] for l in range(depth)]

    cos2, sin2 = _rope_tables(seq, proj_tm)
    tabs = (cos2, sin2)
    xt = jnp.concatenate([x.reshape(n_lat, d), ctx.reshape(n_ctx, d)], 0)
    row2 = lambda v: v.reshape(1, -1)
    ones = lambda n: jnp.ones((1, n), F32)
    qscale = lambda n: jnp.full((1, n), SCALE, F32)

    sh1, sc1, g1, sh2, sc2, g2 = mods[0]
    h = _modulate(xt, sc1, sh1, seq=seq, n_groups=n_groups)
    w_in = w_in_even[0]
    na, nk = qh * HEAD_DIM, kvh * HEAD_DIM
    perm_q, perm_k = _deinterleave_cols(qh), _deinterleave_cols(kvh)
    w_aq = w_in[:, :na][:, perm_q].astype(BF16)
    w_ak = w_in[:, na:na + nk][:, perm_k].astype(BF16)
    w_rest = w_in[:, na + nk:].astype(BF16)
    nb = (heads - qh) * HEAD_DIM
    rest_scale = jnp.concatenate([ones(nk), qscale(nb), ones(2 * nb)], 1)
    kw = dict(seq=seq, n_latent=n_lat)
    aq_rot, aq_nopos = _proj(h, w_aq, qscale(na), n_rows=n_all, rope_tabs=tabs, emit_nopos=True,
                             name="proj_aq", **kw)
    ak = _proj(h, w_ak, ones(nk), n_rows=n_all, rope_tabs=tabs, name="proj_ak", **kw)
    rest = _proj(h, w_rest, rest_scale, n_rows=n_all, tn=w_rest.shape[1] // 2, name="proj_even_rest", **kw)
    bh = heads - qh
    c_av, c_bq, c_bk, c_bv = 0, kvh, kvh + bh, kvh + 2 * bh
    sink = sink_logits[0].astype(F32)
    akw = dict(n_batch=n_batch, seq=seq, ctx_len=ctx_len, n_latent=n_lat)
    o_a = _band_attn(aq_rot, aq_nopos, ak, rest, _window_bias(seq, gq), sink, n_kv=kvh, g=gq,
                     bq=A_BLOCK, q_col=0, k_col=0, v_col=c_av, name="attn_a", **akw)
    o_b = _band_attn(rest, rest, rest, rest, _neighbourhood_bias(na_rpb[0], seq), None, n_kv=bh, g=1,
                     bq=B_TILE_ROWS * GRID_W, q_col=c_bq, k_col=c_bk, v_col=c_bv, name="attn_b", **akw)
    ckw = dict(n_batch=n_batch, n_q=ctx_len, q_row0=n_lat, n_k=ctx_len, k_row0=n_lat, ctx_len=ctx_len,
               ctx_row0=n_lat, out_rows=n_all)
    o_a = _dense_attn(aq_rot, None, ak, rest, sink, n_kv=kvh, g=gq, q_col=0, k_col=0, v_col=c_av,
                      out_into=o_a, name="ctx_attn_a", **ckw)
    o_b = _dense_attn(rest, None, rest, rest, None, n_kv=bh, g=1, q_col=c_bq, k_col=c_bk, v_col=c_bv,
                      out_into=o_b, name="ctx_attn_b", **ckw)
    w_out = w_out_even[0].astype(BF16)
    x1, h2, idx, wts = _outproj_ln(o_a, o_b, w_out[:na], w_out[na:], xt, g1, sc2, sh2, row2(ln1_g[0]),
                                   row2(ln1_b[0]), w_router, row2(router_bias), n_rows=n_all, seq=seq,
                                   n_groups=n_groups, alpha=alpha)
    src_tok, gw, dest, tile_expert, n_valid = _dispatch(idx, wts, tm=moe_tm)
    ys = _moe_ffn(jnp.take(h2, src_tok, axis=0), gw, tile_expert, n_valid, w_exp_gate[0].astype(BF16),
                  w_exp_up[0].astype(BF16), w_exp_down[0].astype(BF16), tm=moe_tm)
    f = jnp.take(ys, dest[:, 0], axis=0).astype(F32) + jnp.take(ys, dest[:, 1], axis=0).astype(F32)
    nsh1, nsc1 = mods[1][0], mods[1][1]
    xt, h = _resid_ln(x1, f, g2, row2(ln2_g[0]), row2(ln2_b[0]), nsc1, nsh1, n_rows=n_all, seq=seq,
                      n_groups=n_groups, alpha=alpha)

    sh1, sc1, g1, sh2, sc2, g2 = mods[1]
    lambda_init = 0.8 - 0.6 * math.exp(-0.3 * 1)
    w_in = w_in_odd[0]
    nd = dh * 2 * HEAD_DIM
    o0 = na + 2 * nk
    perm_d = _deinterleave_cols(2 * dh)
    perm_one = _deinterleave_cols(1)
    w_cq = w_in[:, :na][:, perm_q].astype(BF16)
    w_ck = w_in[:, na:na + nk][:, perm_k].astype(BF16)
    w_dq = w_in[:, o0:o0 + nd][:, perm_d].astype(BF16)
    w_dk = w_in[:, o0 + nd:o0 + 2 * nd][:, perm_d].astype(BF16)
    w_v = jnp.concatenate([w_in[:, na + nk:o0], w_in[:, o0 + 2 * nd:]], 1).astype(BF16)
    qg = row2(q_norm_g[0][perm_one]).astype(F32)
    kg = row2(k_norm_g[0][perm_one]).astype(F32)
    cq_rot, cq_nopos = _proj(h, w_cq, qscale(na), n_rows=n_lat, gain=qg, rope_tabs=tabs, emit_nopos=True,
                             name="proj_cq", **kw)
    ck = _proj(h, w_ck, ones(nk), n_rows=n_all, gain=kg, rope_tabs=tabs, name="proj_ck", **kw)
    dq_rot, dq_nopos = _proj(h, w_dq, qscale(nd), n_rows=n_lat, rope_tabs=tabs, emit_nopos=True,
                             name="proj_dq", **kw)
    dk = _proj(h, w_dk, ones(nd), n_rows=n_all, rope_tabs=tabs, name="proj_dk", **kw)
    vv = _proj(h, w_v, ones(nk + nd), n_rows=n_all, name="proj_odd_v", **kw)
    o_c = _dense_attn(cq_rot, cq_nopos, ck, vv, None, n_batch=n_batch, n_q=seq, q_row0=0, n_k=seq, k_row0=0,
                      ctx_len=ctx_len, ctx_row0=n_lat, n_kv=kvh, g=gq, q_col=0, k_col=0, v_col=0,
                      out_rows=n_lat, name="attn_c")
    lam_vecs = jnp.stack([lambda_q1[0], lambda_k1[0], lambda_q2[0], lambda_k2[0]]).astype(F32)
    o_d = _diff_attn(dq_rot, dq_nopos, dk, vv, lam_vecs, row2(subln_g[0]).astype(F32), n_batch=n_batch,
                     seq=seq, ctx_len=ctx_len, n_latent=n_lat, n_heads=dh, v_col=nk // (2 * HEAD_DIM),
                     lambda_init=lambda_init)
    w_out = w_out_odd[0].astype(BF16)
    x1, h2, idx, wts = _outproj_ln(o_c, o_d, w_out[:na], w_out[na:], xt, g1, sc2, sh2, row2(ln1_g[1]),
                                   row2(ln1_b[1]), w_router, row2(router_bias), n_rows=n_lat, seq=seq,
                                   n_groups=n_groups, alpha=alpha)
    src_tok, gw, dest, tile_expert, n_valid = _dispatch(idx, wts, tm=moe_tm)
    ys = _moe_ffn(jnp.take(h2, src_tok, axis=0), gw, tile_expert, n_valid, w_exp_gate[1].astype(BF16),
                  w_exp_up[1].astype(BF16), w_exp_down[1].astype(BF16), tm=moe_tm)
    f = jnp.take(ys, dest[:, 0], axis=0).astype(F32) + jnp.take(ys, dest[:, 1], axis=0).astype(F32)
    x2, _ = _resid_ln(x1, f, g2, row2(ln2_g[1]), row2(ln2_b[1]), None, None, n_rows=n_lat, seq=seq,
                      n_groups=n_groups, alpha=alpha)
    return x2.reshape(n_batch, seq, d)
```

```python
import functools
import math

import numpy as np
import jax
import jax.numpy as jnp
from jax import lax
from jax.experimental import pallas as pl
from jax.experimental.pallas import tpu as pltpu

F32 = jnp.float32
BF16 = jnp.bfloat16

HEAD_DIM = 128
GRID_W = 64
ROPE_THETA = 10000.0
SCALE = HEAD_DIM ** -0.5
NEG_INF = -1e30
A_WINDOW = 128
A_BLOCK = 128
NA_ROWS = 8
NA_COLS = 16
B_TILE_ROWS = 4
B_HEADS_PER_STEP = 4
N_EXPERTS = 16
N_GROUPS = 4
EXPERTS_PER_GROUP = N_EXPERTS // N_GROUPS
LN_EPS = 1e-5
RMS_EPS = 1e-6
LOG2E = math.log2(math.e)
Q_SCALE = SCALE * LOG2E

V7X_VMEM_BYTES = 64 * 1024 * 1024
VMEM_LIMIT = V7X_VMEM_BYTES - 8 * 1024 * 1024
LANES = 128

NT_DIMS = (((1,), (1,)), ((), ()))


def _params(*sem):
    return pltpu.CompilerParams(dimension_semantics=sem, vmem_limit_bytes=VMEM_LIMIT)


def _adaln_kernel(c_ref, w_ref, b_ref, o_ref):
    c = c_ref[...]
    a = (c * (1.0 / (1.0 + jnp.exp(-c)))).astype(BF16)
    o_ref[...] = jnp.dot(a, w_ref[...].astype(BF16), preferred_element_type=F32) + b_ref[...]


def _adaln(c_rows, w_ada, b_ada):
    depth, d, n6 = w_ada.shape
    rows = c_rows.shape[0]
    tn = 1024
    return pl.pallas_call(
        _adaln_kernel,
        out_shape=jax.ShapeDtypeStruct((depth, rows, n6), F32),
        grid=(depth, n6 // tn),
        in_specs=[
            pl.BlockSpec((rows, d), lambda l, j: (0, 0)),
            pl.BlockSpec((None, d, tn), lambda l, j: (l, 0, j)),
            pl.BlockSpec((None, 1, tn), lambda l, j: (l, 0, j)),
        ],
        out_specs=pl.BlockSpec((None, rows, tn), lambda l, j: (l, 0, j)),
        compiler_params=_params("parallel", "parallel"),
        name="adaln",
    )(c_rows, w_ada, b_ada.reshape(depth, 1, n6))


def _modulate_kernel(x_ref, sc_ref, sh_ref, o_ref):
    o_ref[...] = (x_ref[...] * (1.0 + sc_ref[0]) + sh_ref[0]).astype(o_ref.dtype)


def _modulate(x, sc, sh, *, seq, n_groups):
    rows, d = x.shape
    tm = 512
    grp = lambda i: (jnp.minimum(i // (seq // tm), n_groups - 1), 0, 0)
    return pl.pallas_call(
        _modulate_kernel,
        out_shape=jax.ShapeDtypeStruct((rows, d), BF16),
        grid=(rows // tm,),
        in_specs=[pl.BlockSpec((tm, d), lambda i: (i, 0)),
                  pl.BlockSpec((1, 1, d), grp), pl.BlockSpec((1, 1, d), grp)],
        out_specs=pl.BlockSpec((tm, d), lambda i: (i, 0)),
        compiler_params=_params("parallel"),
        name="modulate",
    )(x, sc, sh)


def _proj_kernel(*refs, tn, norm, rope, emit_nopos):
    it = iter(refs)
    x_ref, w_ref, cs_ref = next(it), next(it), next(it)
    g_ref = next(it) if norm else None
    cos_ref, sin_ref = (next(it), next(it)) if rope else (None, None)
    o_ref = next(it)
    n_ref = next(it) if emit_nopos else None
    acc = jnp.dot(x_ref[...], w_ref[...], preferred_element_type=F32)
    for hd in range(tn // HEAD_DIM):
        sl = slice(hd * HEAD_DIM, (hd + 1) * HEAD_DIM)
        xh = acc[:, sl]
        if norm:
            xh = xh * lax.rsqrt(jnp.mean(xh * xh, axis=-1, keepdims=True) + RMS_EPS) * g_ref[...]
        cs = cs_ref[:, sl]
        if rope:
            rot = xh * cos_ref[...] + pltpu.roll(xh, HEAD_DIM // 2, 1) * sin_ref[...]
            o_ref[:, sl] = (rot * cs).astype(o_ref.dtype)
            if emit_nopos:
                n_ref[:, sl] = (xh * cs).astype(n_ref.dtype)
        else:
            o_ref[:, sl] = (xh * cs).astype(o_ref.dtype)


def _proj(x, w, col_scale, *, n_rows, seq, n_latent, tn=None, gain=None, rope_tabs=None,
          emit_nopos=False, name="proj"):
    d = x.shape[1]
    nc = w.shape[1]
    tm = min(1024, seq)
    tn = nc if tn is None else tn
    norm, rope = gain is not None, rope_tabs is not None
    in_specs = [pl.BlockSpec((tm, d), lambda i, j: (i, 0)),
                pl.BlockSpec((d, tn), lambda i, j: (0, j)),
                pl.BlockSpec((1, tn), lambda i, j: (0, j))]
    args = [x, w, col_scale]
    if norm:
        in_specs.append(pl.BlockSpec((1, HEAD_DIM), lambda i, j: (0, 0)))
        args.append(gain)
    if rope:
        per_seq = seq // tm
        tab = lambda i, j: (jnp.where(i < n_latent // tm, i % per_seq, per_seq), 0)
        in_specs += [pl.BlockSpec((tm, HEAD_DIM), tab)] * 2
        args += list(rope_tabs)
    n_out = 2 if emit_nopos else 1
    out_shape = [jax.ShapeDtypeStruct((n_rows, nc), BF16)] * n_out
    out_specs = [pl.BlockSpec((tm, tn), lambda i, j: (i, j))] * n_out
    res = pl.pallas_call(
        functools.partial(_proj_kernel, tn=tn, norm=norm, rope=rope, emit_nopos=emit_nopos),
        out_shape=out_shape,
        grid=(n_rows // tm, nc // tn),
        in_specs=in_specs,
        out_specs=out_specs,
        compiler_params=_params("parallel", "parallel"),
        name=name,
    )(*args)
    return res if emit_nopos else res[0]


def _stack_heads(x, g):
    if g == 1:
        return x
    return jnp.concatenate([x[:, i * HEAD_DIM:(i + 1) * HEAD_DIM] for i in range(g)], axis=0)


def _unstack_store(o_ref, o, g, t, col0=0):
    for i in range(g):
        o_ref[:, col0 + i * HEAD_DIM:col0 + (i + 1) * HEAD_DIM] = o[i * t:(i + 1) * t].astype(o_ref.dtype)


def _sink_column(sink_ref, first_head, g, t):
    cols = [jnp.full((t, 1), sink_ref[first_head + i] * LOG2E, F32) for i in range(g)]
    return cols[0] if g == 1 else jnp.concatenate(cols, axis=0)


def _with_ones(v):
    return jnp.concatenate([v, jnp.ones_like(v)], axis=1)


def _row_max(pieces):
    cols = [p[:, c:c + LANES] for p in pieces for c in range(0, p.shape[1], LANES)]
    return jnp.max(functools.reduce(jnp.maximum, cols), axis=-1, keepdims=True)


def _online_update(s, v_aug, m, acc):
    m_new = jnp.maximum(m, jnp.max(s, axis=-1, keepdims=True))
    p = jnp.exp2(s - m_new)
    acc = jnp.exp2(m - m_new) * acc + jnp.dot(p.astype(BF16), v_aug, preferred_element_type=F32)
    return m_new, acc


def _band_kernel(*refs, nh, g, bq, has_sink, same_qc, shared_bias):
    it = iter(refs)
    sink_ref = next(it) if has_sink else None
    q_ref = next(it)
    qc_ref = q_ref if same_qc else next(it)
    k_refs = [next(it) for _ in range(3)]
    v_refs = [next(it) for _ in range(3)]
    kx_ref, vx_ref, bias_ref, o_ref = next(it), next(it), next(it), next(it)
    hw = g * HEAD_DIM
    for h in range(nh):
        qs = slice(h * hw, (h + 1) * hw)
        ks = slice(h * HEAD_DIM, (h + 1) * HEAD_DIM)
        q = _stack_heads(q_ref[:, qs], g)
        qc = q if same_qc else _stack_heads(qc_ref[:, qs], g)
        bias = bias_ref.at[0 if shared_bias else h]
        s = [lax.dot_general(q, k_refs[j][:, ks], NT_DIMS, preferred_element_type=F32)
             + bias[:, j * bq:(j + 1) * bq] for j in range(3)]
        s.append(lax.dot_general(qc, kx_ref[:, ks], NT_DIMS, preferred_element_type=F32))
        vs = [r[:, ks] for r in v_refs] + [vx_ref[:, ks]]
        m = _row_max(s)
        if has_sink:
            sink = _sink_column(sink_ref, (pl.program_id(0) * nh + h) * g, g, bq)
            m = jnp.maximum(m, sink)
        o = functools.reduce(jnp.add, [
            jnp.dot(jnp.exp2(sj - m).astype(BF16), _with_ones(vj), preferred_element_type=F32)
            for sj, vj in zip(s, vs)])
        den = o[:, HEAD_DIM:]
        if has_sink:
            den = den + jnp.exp2(sink - m)
        _unstack_store(o_ref, o[:, :HEAD_DIM] / den, g, bq, col0=h * hw)


def _band_attn(q, qc, k, v, bias, sink, *, n_batch, seq, ctx_len, n_latent, n_kv, nh, g, bq,
               q_col, k_col, v_col, name):
    nblk = seq // bq
    ctx_blk = n_latent // ctx_len
    shared_bias = bias.shape[0] == 1
    has_sink, same_qc = sink is not None, qc is None
    qw, kw = nh * g * HEAD_DIM, nh * HEAD_DIM

    def q_map(hg, i, b):
        return (b * nblk + i, q_col + hg)

    def kv_map(col, off):
        return lambda hg, i, b: (b * nblk + jnp.clip(i + off, 0, nblk - 1), col + hg)

    def ctx_map(col):
        return lambda hg, i, b: (ctx_blk + b, col + hg)

    def bias_map(hg, i, b):
        case = jnp.where(i == 0, 0, jnp.where(i == nblk - 1, 2, 1))
        return (0 if shared_bias else hg, case, 0, 0)

    in_specs, args = [], []
    if has_sink:
        in_specs.append(pl.BlockSpec(memory_space=pltpu.SMEM))
        args.append(sink)
    in_specs.append(pl.BlockSpec((bq, qw), q_map))
    args.append(q)
    if not same_qc:
        in_specs.append(pl.BlockSpec((bq, qw), q_map))
        args.append(qc)
    in_specs += [pl.BlockSpec((bq, kw), kv_map(k_col, off)) for off in (-1, 0, 1)]
    args += [k] * 3
    in_specs += [pl.BlockSpec((bq, kw), kv_map(v_col, off)) for off in (-1, 0, 1)]
    args += [v] * 3
    in_specs += [pl.BlockSpec((ctx_len, kw), ctx_map(k_col)),
                 pl.BlockSpec((ctx_len, kw), ctx_map(v_col)),
                 pl.BlockSpec((1 if shared_bias else nh, None, g * bq, 3 * bq), bias_map)]
    args += [k, v, bias]
    return pl.pallas_call(
        functools.partial(_band_kernel, nh=nh, g=g, bq=bq, has_sink=has_sink, same_qc=same_qc,
                          shared_bias=shared_bias),
        out_shape=jax.ShapeDtypeStruct((n_latent + n_batch * ctx_len, n_kv * g * HEAD_DIM), BF16),
        grid=(n_kv // nh, nblk, n_batch),
        in_specs=in_specs,
        out_specs=pl.BlockSpec((bq, qw), lambda hg, i, b: (b * nblk + i, hg)),
        compiler_params=_params("parallel", "parallel", "parallel"),
        name=name,
    )(*args)


def _dense_kernel(*refs, g, tq, tk, n_k, has_ctx, has_sink, aliased):
    it = iter(refs)
    sink_ref = next(it) if has_sink else None
    q_ref = next(it)
    qc_ref = next(it) if has_ctx else None
    k_ref, v_ref = next(it), next(it)
    kx_ref, vx_ref = (next(it), next(it)) if has_ctx else (None, None)
    if aliased:
        next(it)
    o_ref = next(it)
    q = _stack_heads(q_ref[...], g)
    rows = g * tq
    m = jnp.full((rows, 1), NEG_INF, F32)
    acc = jnp.zeros((rows, 2 * HEAD_DIM), F32)
    for c in range(n_k // tk):
        s = lax.dot_general(q, k_ref[c * tk:(c + 1) * tk, :], NT_DIMS, preferred_element_type=F32)
        m, acc = _online_update(s, _with_ones(v_ref[c * tk:(c + 1) * tk, :]), m, acc)
    if has_ctx:
        qc = _stack_heads(qc_ref[...], g)
        s = lax.dot_general(qc, kx_ref[...], NT_DIMS, preferred_element_type=F32)
        m, acc = _online_update(s, _with_ones(vx_ref[...]), m, acc)
    num, den = acc[:, :HEAD_DIM], acc[:, HEAD_DIM:]
    if has_sink:
        sink = _sink_column(sink_ref, pl.program_id(1) * g, g, tq)
        m_new = jnp.maximum(m, sink)
        alpha = jnp.exp2(m - m_new)
        num = alpha * num
        den = alpha * den + jnp.exp2(sink - m_new)
    _unstack_store(o_ref, num / den, g, tq)


def _dense_attn(q, qc, k, v, sink, *, n_batch, n_q, q_row0, n_k, k_row0, ctx_len, ctx_row0,
                n_kv, g, q_col, k_col, v_col, out_rows, out_into=None, name):
    tq = min(256, n_q)
    tk = min(512, n_k)
    nq_blk = n_q // tq
    has_ctx, has_sink, aliased = qc is not None, sink is not None, out_into is not None

    def q_map(b, h, i):
        return (q_row0 // tq + b * nq_blk + i, q_col + h)

    in_specs, args = [], []
    if has_sink:
        in_specs.append(pl.BlockSpec(memory_space=pltpu.SMEM))
        args.append(sink)
    in_specs.append(pl.BlockSpec((tq, g * HEAD_DIM), q_map))
    args.append(q)
    if has_ctx:
        in_specs.append(pl.BlockSpec((tq, g * HEAD_DIM), q_map))
        args.append(qc)
    in_specs += [pl.BlockSpec((n_k, HEAD_DIM), lambda b, h, i: (k_row0 // n_k + b, k_col + h)),
                 pl.BlockSpec((n_k, HEAD_DIM), lambda b, h, i: (k_row0 // n_k + b, v_col + h))]
    args += [k, v]
    if has_ctx:
        in_specs += [pl.BlockSpec((ctx_len, HEAD_DIM), lambda b, h, i: (ctx_row0 // ctx_len + b, k_col + h)),
                     pl.BlockSpec((ctx_len, HEAD_DIM), lambda b, h, i: (ctx_row0 // ctx_len + b, v_col + h))]
        args += [k, v]
    aliases = {}
    if aliased:
        in_specs.append(pl.BlockSpec(memory_space=pl.ANY))
        aliases = {len(args): 0}
        args.append(out_into)
    return pl.pallas_call(
        functools.partial(_dense_kernel, g=g, tq=tq, tk=tk, n_k=n_k, has_ctx=has_ctx,
                          has_sink=has_sink, aliased=aliased),
        out_shape=jax.ShapeDtypeStruct((out_rows, n_kv * g * HEAD_DIM), BF16),
        grid=(n_batch, n_kv, nq_blk),
        in_specs=in_specs,
        out_specs=pl.BlockSpec((tq, g * HEAD_DIM), lambda b, h, i: (q_row0 // tq + b * nq_blk + i, h)),
        input_output_aliases=aliases,
        compiler_params=_params("parallel", "parallel", "parallel"),
        name=name,
    )(*args)


def _diff_update(s, v, m, l, acc):
    m_new = jnp.maximum(m, jnp.max(s, axis=-1, keepdims=True))
    alpha = jnp.exp2(m - m_new)
    p = jnp.exp2(s - m_new)
    l = alpha * l + jnp.sum(p, axis=-1, keepdims=True)
    acc = alpha * acc + jnp.dot(p.astype(BF16), v, preferred_element_type=F32)
    return m_new, l, acc


def _diff_kernel(lam_ref, g_ref, q_ref, qc_ref, k_ref, v_ref, kx_ref, vx_ref, o_ref, *,
                 tq, tk, n_k, lambda_init):
    dv = 2 * HEAD_DIM
    lam = (jnp.exp(jnp.sum(lam_ref[0:1, :] * lam_ref[1:2, :], axis=1, keepdims=True))
           - jnp.exp(jnp.sum(lam_ref[2:3, :] * lam_ref[3:4, :], axis=1, keepdims=True)) + lambda_init)
    outs = []
    for t in range(2):
        sl = slice(t * HEAD_DIM, (t + 1) * HEAD_DIM)
        q = q_ref[:, sl]
        m = jnp.full((tq, 1), NEG_INF, F32)
        l = jnp.zeros((tq, 1), F32)
        acc = jnp.zeros((tq, dv), F32)
        for c in range(n_k // tk):
            rows = slice(c * tk, (c + 1) * tk)
            s = lax.dot_general(q, k_ref[rows, sl], NT_DIMS, preferred_element_type=F32)
            m, l, acc = _diff_update(s, v_ref[rows, :], m, l, acc)
        s = lax.dot_general(qc_ref[:, sl], kx_ref[:, sl], NT_DIMS, preferred_element_type=F32)
        m, l, acc = _diff_update(s, vx_ref[...], m, l, acc)
        outs.append(acc / l)
    o = outs[0] - lam * outs[1]
    o = o * lax.rsqrt(jnp.mean(o * o, axis=-1, keepdims=True) + RMS_EPS) * g_ref[...]
    o_ref[...] = (o * (1.0 - lambda_init)).astype(o_ref.dtype)


def _diff_attn(q, qc, k, v, lam_vecs, subln_g, *, n_batch, seq, ctx_len, n_latent, n_heads,
               v_col, lambda_init):
    tq = min(512, seq)
    tk = min(512, seq)
    dv = 2 * HEAD_DIM
    nq_blk = seq // tq
    ctx_blk = n_latent // ctx_len
    q_map = lambda b, h, i: (b * nq_blk + i, h)
    return pl.pallas_call(
        functools.partial(_diff_kernel, tq=tq, tk=tk, n_k=seq, lambda_init=lambda_init),
        out_shape=jax.ShapeDtypeStruct((n_latent, n_heads * dv), BF16),
        grid=(n_batch, n_heads, nq_blk),
        in_specs=[
            pl.BlockSpec((4, HEAD_DIM), lambda b, h, i: (0, 0)),
            pl.BlockSpec((1, dv), lambda b, h, i: (0, 0)),
            pl.BlockSpec((tq, dv), q_map),
            pl.BlockSpec((tq, dv), q_map),
            pl.BlockSpec((seq, dv), lambda b, h, i: (b, h)),
            pl.BlockSpec((seq, dv), lambda b, h, i: (b, v_col + h)),
            pl.BlockSpec((ctx_len, dv), lambda b, h, i: (ctx_blk + b, h)),
            pl.BlockSpec((ctx_len, dv), lambda b, h, i: (ctx_blk + b, v_col + h)),
        ],
        out_specs=pl.BlockSpec((tq, dv), q_map),
        compiler_params=_params("parallel", "parallel", "parallel"),
        name="diff_attn",
    )(lam_vecs, subln_g, q, qc, k, v, k, v)


def _layer_norm(z, g, b):
    mu = jnp.mean(z, axis=-1, keepdims=True)
    zc = z - mu
    var = jnp.mean(zc * zc, axis=-1, keepdims=True)
    return zc * lax.rsqrt(var + LN_EPS) * g + b


def _split_bf16(x):
    hi = x.astype(BF16)
    return hi, (x - hi.astype(F32)).astype(BF16)


def _first_argmax(vals, idx, width):
    m = jnp.max(vals, axis=0, keepdims=True)
    first = jnp.min(jnp.where(vals == m, idx, float(width)), axis=0, keepdims=True)
    return m, first


def _route(logits, bias):
    e, t = logits.shape
    scores = 1.0 / (1.0 + jnp.exp(-logits))
    biased = scores + bias
    row_i = lax.broadcasted_iota(jnp.int32, (e, t), 0)
    grp = lax.shift_right_logical(row_i, int(math.log2(EXPERTS_PER_GROUP)))
    row = row_i.astype(F32)
    neg = -jnp.inf
    best_score, best = None, None
    for gi in range(N_GROUPS):
        vg = jnp.where(grp == gi, biased, neg)
        m1, i1 = _first_argmax(vg, row, e)
        m2 = jnp.max(jnp.where(row == i1, neg, vg), axis=0, keepdims=True)
        gs = m1 + m2
        if gi == 0:
            best_score, best = gs, jnp.zeros((1, t), jnp.int32)
        else:
            upd = gs > best_score
            best = jnp.where(upd, gi, best)
            best_score = jnp.where(upd, gs, best_score)
    masked = jnp.where(grp == best, biased, neg)
    _, e1 = _first_argmax(masked, row, e)
    _, e2 = _first_argmax(jnp.where(row == e1, neg, masked), row, e)
    w1 = jnp.sum(jnp.where(row == e1, scores, 0.0), axis=0, keepdims=True)
    w2 = jnp.sum(jnp.where(row == e2, scores, 0.0), axis=0, keepdims=True)
    den = w1 + w2
    two = lax.broadcasted_iota(jnp.int32, (2, t), 0)
    return jnp.where(two == 0, e1, e2).astype(jnp.int32), jnp.where(two == 0, w1 / den, w2 / den)


def _outproj_kernel(xa_ref, xb_ref, wa_ref, wb_ref, xres_ref, g1_ref, sc2_ref, sh2_ref,
                    lng_ref, lnb_ref, wrt_ref, rb_ref, x1_ref, h2_ref, idx_ref, wts_ref, *, alpha):
    y = (jnp.dot(xa_ref[...], wa_ref[...], preferred_element_type=F32)
         + jnp.dot(xb_ref[...], wb_ref[...], preferred_element_type=F32))
    x1 = _layer_norm(alpha * xres_ref[...] + g1_ref[0] * y, lng_ref[...], lnb_ref[...])
    x1_ref[...] = x1
    h2 = x1 * (1.0 + sc2_ref[0]) + sh2_ref[0]
    h2_ref[...] = h2.astype(h2_ref.dtype)
    h_hi, h_lo = _split_bf16(h2)
    w_hi, w_lo = _split_bf16(wrt_ref[...])
    part = lax.dot_general(jnp.concatenate([w_hi, w_lo], axis=0), h_hi, NT_DIMS, preferred_element_type=F32)
    logits = (part[:N_EXPERTS] + part[N_EXPERTS:]
              + lax.dot_general(w_hi, h_lo, NT_DIMS, preferred_element_type=F32))
    idx, wts = _route(logits, rb_ref[...])
    idx_ref[...] = idx
    wts_ref[...] = wts


def _outproj_ln(xa, xb, wa, wb, xres, g1, sc2, sh2, ln_g, ln_b, w_router_t, router_bias, *,
                n_rows, seq, n_groups, alpha):
    d = xres.shape[1]
    ka, kb = xa.shape[1], xb.shape[1]
    tm = 256
    row = lambda i: (i, 0)
    col = lambda i: (0, i)
    const = lambda i: (0, 0)
    grp = lambda i: (jnp.minimum(i // (seq // tm), n_groups - 1), 0, 0)
    return pl.pallas_call(
        functools.partial(_outproj_kernel, alpha=alpha),
        out_shape=[jax.ShapeDtypeStruct((n_rows, d), F32), jax.ShapeDtypeStruct((n_rows, d), BF16),
                   jax.ShapeDtypeStruct((2, n_rows), jnp.int32), jax.ShapeDtypeStruct((2, n_rows), F32)],
        grid=(n_rows // tm,),
        in_specs=[pl.BlockSpec((tm, ka), row), pl.BlockSpec((tm, kb), row),
                  pl.BlockSpec((ka, d), const), pl.BlockSpec((kb, d), const),
                  pl.BlockSpec((tm, d), row),
                  pl.BlockSpec((1, 1, d), grp), pl.BlockSpec((1, 1, d), grp), pl.BlockSpec((1, 1, d), grp),
                  pl.BlockSpec((1, d), const), pl.BlockSpec((1, d), const),
                  pl.BlockSpec((N_EXPERTS, d), const), pl.BlockSpec((N_EXPERTS, 1), const)],
        out_specs=[pl.BlockSpec((tm, d), row), pl.BlockSpec((tm, d), row),
                   pl.BlockSpec((2, tm), col), pl.BlockSpec((2, tm), col)],
        compiler_params=_params("parallel"),
        name="outproj_ln",
    )(xa, xb, wa, wb, xres, g1, sc2, sh2, ln_g, ln_b, w_router_t, router_bias)


def _moe_kernel(te_ref, nv_ref, x_ref, gw_ref, wg_ref, wu_ref, wd_ref, o_ref):
    i = pl.program_id(0)

    @pl.when(i < nv_ref[0])
    def _():
        x = x_ref[...]
        gate = jnp.dot(x, wg_ref[...], preferred_element_type=F32)
        up = jnp.dot(x, wu_ref[...], preferred_element_type=F32)
        act = (gate * (1.0 / (1.0 + jnp.exp(-gate))) * up).astype(BF16)
        y = jnp.dot(act, wd_ref[...], preferred_element_type=F32)
        o_ref[...] = (y * gw_ref[...]).astype(o_ref.dtype)

    @pl.when(i >= nv_ref[0])
    def _():
        o_ref[...] = jnp.zeros_like(o_ref)


def _moe_ffn(xs, gw, tile_expert, n_valid, w_gate, w_up, w_down, *, layer, tm):
    rows, d = xs.shape
    ff = w_gate.shape[3]
    return pl.pallas_call(
        _moe_kernel,
        out_shape=jax.ShapeDtypeStruct((rows, d), BF16),
        grid_spec=pltpu.PrefetchScalarGridSpec(
            num_scalar_prefetch=2,
            grid=(rows // tm,),
            in_specs=[pl.BlockSpec((tm, d), lambda i, te, nv: (i, 0)),
                      pl.BlockSpec((tm, 1), lambda i, te, nv: (i, 0)),
                      pl.BlockSpec((None, None, d, ff), lambda i, te, nv: (layer, te[i], 0, 0)),
                      pl.BlockSpec((None, None, d, ff), lambda i, te, nv: (layer, te[i], 0, 0)),
                      pl.BlockSpec((None, None, ff, d), lambda i, te, nv: (layer, te[i], 0, 0))],
            out_specs=pl.BlockSpec((tm, d), lambda i, te, nv: (i, 0)),
        ),
        compiler_params=_params("arbitrary"),
        name="moe_ffn",
    )(tile_expert, n_valid, xs, gw, w_gate, w_up, w_down)


def _take(x, idx):
    return x.at[idx].get(mode="promise_in_bounds")


def _dispatch(idx, wts, *, tm):
    t = idx.shape[1]
    n_pairs = 2 * t
    n_tiles = n_pairs // tm + N_EXPERTS
    e_flat = idx.reshape(n_pairs)
    w_flat = wts.reshape(n_pairs)
    order = jnp.argsort(e_flat, stable=True).astype(jnp.int32)
    rank = jnp.argsort(order).astype(jnp.int32)
    counts = jnp.sum(e_flat[:, None] == jnp.arange(N_EXPERTS)[None, :], axis=0).astype(jnp.int32)
    start = jnp.cumsum(counts) - counts
    padded = ((counts + tm - 1) // tm) * tm
    pend = jnp.cumsum(padded)
    pstart = pend - padded
    dest = pstart[e_flat] + rank - start[e_flat]
    pos = jnp.arange(n_tiles * tm, dtype=jnp.int32)
    e_pos = jnp.minimum(jnp.sum(pos[:, None] >= pend[None, :], axis=1), N_EXPERTS - 1).astype(jnp.int32)
    local = pos - pstart[e_pos]
    valid = local < counts[e_pos]
    src_pair = _take(order, jnp.clip(start[e_pos] + local, 0, n_pairs - 1))
    src_tok = jnp.where(valid, src_pair % t, 0)
    gw = jnp.where(valid, _take(w_flat, src_pair), 0.0).astype(F32)
    tile_expert = e_pos[::tm]
    n_valid = (pend[-1] // tm).astype(jnp.int32).reshape(1)
    return src_tok, gw.reshape(-1, 1), dest.reshape(2, t), tile_expert, n_valid


def _resid_ln_kernel(*refs, alpha, emit_h):
    it = iter(refs)
    x_ref, f_ref, g_ref, lng_ref, lnb_ref = (next(it) for _ in range(5))
    sc_ref, sh_ref = (next(it), next(it)) if emit_h else (None, None)
    o_ref = next(it)
    x2 = _layer_norm(alpha * x_ref[...] + g_ref[0] * f_ref[...].astype(F32), lng_ref[...], lnb_ref[...])
    o_ref[...] = x2
    if emit_h:
        h_ref = next(it)
        h_ref[...] = (x2 * (1.0 + sc_ref[0]) + sh_ref[0]).astype(h_ref.dtype)


def _resid_ln(x, f, gate, ln_g, ln_b, next_sc, next_sh, *, n_rows, seq, n_groups, alpha):
    d = x.shape[1]
    tm = 512
    emit_h = next_sc is not None
    row = lambda i: (i, 0)
    const = lambda i: (0, 0)
    grp = lambda i: (jnp.minimum(i // (seq // tm), n_groups - 1), 0, 0)
    in_specs = [pl.BlockSpec((tm, d), row), pl.BlockSpec((tm, d), row), pl.BlockSpec((1, 1, d), grp),
                pl.BlockSpec((1, d), const), pl.BlockSpec((1, d), const)]
    args = [x, f, gate, ln_g, ln_b]
    out_shape = [jax.ShapeDtypeStruct((n_rows, d), F32)]
    out_specs = [pl.BlockSpec((tm, d), row)]
    if emit_h:
        in_specs += [pl.BlockSpec((1, 1, d), grp)] * 2
        args += [next_sc, next_sh]
        out_shape.append(jax.ShapeDtypeStruct((n_rows, d), BF16))
        out_specs.append(pl.BlockSpec((tm, d), row))
    res = pl.pallas_call(
        functools.partial(_resid_ln_kernel, alpha=alpha, emit_h=emit_h),
        out_shape=out_shape,
        grid=(n_rows // tm,),
        in_specs=in_specs,
        out_specs=out_specs,
        compiler_params=_params("parallel"),
        name="resid_ln",
    )(*args)
    return res if emit_h else (res[0], None)


def _rope_tables(seq, tm):
    t = jnp.arange(seq, dtype=jnp.int32)
    n_freq = HEAD_DIM // 4
    inv = ROPE_THETA ** (-jnp.arange(n_freq, dtype=F32) / n_freq)
    row = (t // GRID_W).astype(F32)
    col = (t % GRID_W).astype(F32)
    ang = jnp.concatenate([row[:, None] * inv[None], col[:, None] * inv[None]], -1)
    cos, sin = jnp.cos(ang), jnp.sin(ang)
    cos2 = jnp.concatenate([cos, cos], -1)
    sin2 = jnp.concatenate([-sin, sin], -1)
    cos2 = jnp.concatenate([cos2, jnp.ones((tm, HEAD_DIM), F32)], 0)
    sin2 = jnp.concatenate([sin2, jnp.zeros((tm, HEAD_DIM), F32)], 0)
    return cos2, sin2


def _window_bias(seq, g):
    bq = A_BLOCK
    nblk = seq // bq
    rel = np.arange(3 * bq)[None, :] - np.arange(bq)[:, None]
    band = (rel >= bq - A_WINDOW) & (rel <= bq + A_WINDOW)
    tabs = []
    for blk in (0, min(1, nblk - 1), nblk - 1):
        kpos = blk * bq - bq + np.arange(3 * bq)
        ok = band & ((kpos >= 0) & (kpos < seq))[None, :]
        tabs.append(np.tile(np.where(ok, 0.0, NEG_INF).astype(np.float32), (g, 1)))
    return jnp.asarray(np.stack(tabs)[None])


def _neighbourhood_bias(rpb, seq):
    n_heads = rpb.shape[0]
    rows = seq // GRID_W
    kh, kw = min(NA_ROWS, rows), NA_COLS
    tr = B_TILE_ROWS
    nblk = rows // tr
    col = np.arange(GRID_W)
    col_start = np.clip(col - kw // 2, 0, GRID_W - kw)
    col_ok = (col[None, :] >= col_start[:, None]) & (col[None, :] < col_start[:, None] + kw)
    assert np.all(np.abs(col[None, :] - col[:, None])[col_ok] <= NA_COLS - 1)
    masks = []
    for blk in (0, min(1, nblk - 1), nblk - 1):
        r = blk * tr + np.arange(tr)
        r0 = np.clip(r - kh // 2, 0, rows - kh)
        krow = (blk - 1) * tr + np.arange(3 * tr)
        row_ok = (krow[None, :] >= r0[:, None]) & (krow[None, :] < r0[:, None] + kh)
        ok = row_ok[:, None, :, None] & col_ok[None, :, None, :]
        masks.append(ok.reshape(tr * GRID_W, 3 * tr * GRID_W))
    ok = jnp.asarray(np.stack(masks))
    pad = GRID_W - NA_COLS
    rp = jnp.pad(rpb.astype(F32) * LOG2E, ((0, 0), (0, 0), (pad, pad)))
    cexp = jnp.stack([rp[:, :, GRID_W - 1 - cq:2 * GRID_W - 1 - cq] for cq in range(GRID_W)], axis=2)
    off = NA_ROWS - 1 - tr
    assert off - (tr - 1) >= 0 and off + 3 * tr - 1 <= 2 * NA_ROWS - 2
    t5 = jnp.stack([cexp[:, off - rq:off - rq + 3 * tr] for rq in range(tr)], axis=1)
    tab = t5.transpose(0, 1, 3, 2, 4).reshape(n_heads, tr * GRID_W, 3 * tr * GRID_W)
    return jnp.where(ok[None], tab[:, None], NEG_INF)


def _deinterleave_cols(n_heads):
    one = np.concatenate([np.arange(0, HEAD_DIM, 2), np.arange(1, HEAD_DIM, 2)])
    return np.concatenate([h * HEAD_DIM + one for h in range(n_heads)])


def kernel(x, c, ctx, c_ctx, w_ada, b_ada, ln1_g, ln1_b, ln2_g, ln2_b, w_in_even, w_out_even, sink_logits, na_rpb, w_in_odd, w_out_odd, q_norm_g, k_norm_g, lambda_q1, lambda_k1, lambda_q2, lambda_k2, subln_g, w_router, router_bias, w_exp_gate, w_exp_up, w_exp_down):
    n_batch, seq, d = x.shape
    ctx_len = ctx.shape[1]
    depth = w_ada.shape[0]
    assert depth == 2, "one even (A||B) layer followed by one odd (C||D) layer"
    n_lat = n_batch * seq
    n_ctx = n_batch * ctx_len
    n_all = n_lat + n_ctx
    n_groups = n_batch + 1
    alpha = float((2 * depth) ** 0.25)
    heads = d // HEAD_DIM
    qh = heads // 2
    kvh = qh // 4
    gq = qh // kvh
    bh = heads - qh
    dh = (heads - qh) // 2
    moe_tm = 512
    proj_tm = min(1024, seq)
    assert seq % GRID_W == 0 and (seq // GRID_W) % B_TILE_ROWS == 0 and seq // GRID_W >= NA_ROWS
    assert n_ctx % proj_tm == 0 and n_lat % ctx_len == 0 and (2 * n_all) % moe_tm == 0
    assert ctx_len % LANES == 0 and bh % B_HEADS_PER_STEP == 0

    pad_rows = -(-n_groups // 16) * 16
    c_rows = jnp.concatenate([c, c_ctx[None], jnp.zeros((pad_rows - n_groups, d), F32)], 0)
    mod = _adaln(c_rows, w_ada, b_ada)[:, :n_groups]
    mod = mod.reshape(depth, n_groups, 6, 1, d)
    mods = [[mod[l, :, j] for j in range(6)] for l in range(depth)]

    cos2, sin2 = _rope_tables(seq, proj_tm)
    tabs = (cos2, sin2)
    xt = jnp.concatenate([x.reshape(n_lat, d), ctx.reshape(n_ctx, d)], 0)
    row2 = lambda v: v.reshape(1, -1)
    ones = lambda n: jnp.ones((1, n), F32)
    qscale = lambda n: jnp.full((1, n), Q_SCALE, F32)
    w_router_t = w_router.T.astype(F32)
    router_bias_col = router_bias.reshape(-1, 1).astype(F32)
    wg_all, wu_all, wd_all = (w.astype(BF16) for w in (w_exp_gate, w_exp_up, w_exp_down))

    def moe(h2, idx, wts, layer):
        src_tok, gw, dest, tile_expert, n_valid = _dispatch(idx, wts, tm=moe_tm)
        ys = _moe_ffn(_take(h2, src_tok), gw, tile_expert, n_valid, wg_all, wu_all, wd_all,
                      layer=layer, tm=moe_tm)
        return _take(ys, dest[0]).astype(F32) + _take(ys, dest[1]).astype(F32)

    sh1, sc1, g1, sh2, sc2, g2 = mods[0]
    h = _modulate(xt, sc1, sh1, seq=seq, n_groups=n_groups)
    w_in = w_in_even[0]
    na, nk, nb = qh * HEAD_DIM, kvh * HEAD_DIM, bh * HEAD_DIM
    perm_q, perm_k = _deinterleave_cols(qh), _deinterleave_cols(kvh)
    w_aq = w_in[:, :na][:, perm_q].astype(BF16)
    w_ak = w_in[:, na:na + nk][:, perm_k].astype(BF16)
    w_rest = jnp.concatenate([w_in[:, na + 2 * nk:], w_in[:, na + nk:na + 2 * nk]], 1).astype(BF16)
    rest_scale = jnp.concatenate([qscale(nb), ones(2 * nb + nk)], 1)
    kw = dict(seq=seq, n_latent=n_lat)
    aq_rot, aq_nopos = _proj(h, w_aq, qscale(na), n_rows=n_all, rope_tabs=tabs, emit_nopos=True,
                             name="proj_aq", **kw)
    ak = _proj(h, w_ak, ones(nk), n_rows=n_all, rope_tabs=tabs, name="proj_ak", **kw)
    rest = _proj(h, w_rest, rest_scale, n_rows=n_all, tn=w_rest.shape[1] // 2, name="proj_even_rest", **kw)
    sink = sink_logits[0].astype(F32)
    akw = dict(n_batch=n_batch, seq=seq, ctx_len=ctx_len, n_latent=n_lat)
    o_a = _band_attn(aq_rot, aq_nopos, ak, rest, _window_bias(seq, gq), sink, n_kv=kvh, nh=kvh, g=gq,
                     bq=A_BLOCK, q_col=0, k_col=0, v_col=3 * nb // nk, name="attn_a", **akw)
    bw = B_HEADS_PER_STEP * HEAD_DIM
    o_b = _band_attn(rest, None, rest, rest, _neighbourhood_bias(na_rpb[0], seq), None, n_kv=bh,
                     nh=B_HEADS_PER_STEP, g=1, bq=B_TILE_ROWS * GRID_W, q_col=0, k_col=nb // bw,
                     v_col=2 * nb // bw, name="attn_b", **akw)
    ckw = dict(n_batch=n_batch, n_q=ctx_len, q_row0=n_lat, n_k=ctx_len, k_row0=n_lat, ctx_len=ctx_len,
               ctx_row0=n_lat, out_rows=n_all)
    o_a = _dense_attn(aq_rot, None, ak, rest, sink, n_kv=kvh, g=gq, q_col=0, k_col=0,
                      v_col=3 * nb // HEAD_DIM, out_into=o_a, name="ctx_attn_a", **ckw)
    o_b = _dense_attn(rest, None, rest, rest, None, n_kv=bh, g=1, q_col=0, k_col=bh, v_col=2 * bh,
                      out_into=o_b, name="ctx_attn_b", **ckw)
    w_out = w_out_even[0].astype(BF16)
    x1, h2, idx, wts = _outproj_ln(o_a, o_b, w_out[:na], w_out[na:], xt, g1, sc2, sh2, row2(ln1_g[0]),
                                   row2(ln1_b[0]), w_router_t, router_bias_col, n_rows=n_all, seq=seq,
                                   n_groups=n_groups, alpha=alpha)
    f = moe(h2, idx, wts, 0)
    nsh1, nsc1 = mods[1][0], mods[1][1]
    xt, h = _resid_ln(x1, f, g2, row2(ln2_g[0]), row2(ln2_b[0]), nsc1, nsh1, n_rows=n_all, seq=seq,
                      n_groups=n_groups, alpha=alpha)

    sh1, sc1, g1, sh2, sc2, g2 = mods[1]
    lambda_init = 0.8 - 0.6 * math.exp(-0.3 * 1)
    w_in = w_in_odd[0]
    nd = dh * 2 * HEAD_DIM
    o0 = na + 2 * nk
    perm_d = _deinterleave_cols(2 * dh)
    perm_one = _deinterleave_cols(1)
    w_cq = w_in[:, :na][:, perm_q].astype(BF16)
    w_ck = w_in[:, na:na + nk][:, perm_k].astype(BF16)
    w_dq = w_in[:, o0:o0 + nd][:, perm_d].astype(BF16)
    w_dk = w_in[:, o0 + nd:o0 + 2 * nd][:, perm_d].astype(BF16)
    w_v = jnp.concatenate([w_in[:, na + nk:o0], w_in[:, o0 + 2 * nd:]], 1).astype(BF16)
    qg = row2(q_norm_g[0][perm_one]).astype(F32)
    kg = row2(k_norm_g[0][perm_one]).astype(F32)
    cq_rot, cq_nopos = _proj(h, w_cq, qscale(na), n_rows=n_lat, gain=qg, rope_tabs=tabs, emit_nopos=True,
                             name="proj_cq", **kw)
    ck = _proj(h, w_ck, ones(nk), n_rows=n_all, gain=kg, rope_tabs=tabs, name="proj_ck", **kw)
    dq_rot, dq_nopos = _proj(h, w_dq, qscale(nd), n_rows=n_lat, rope_tabs=tabs, emit_nopos=True,
                             name="proj_dq", **kw)
    dk = _proj(h, w_dk, ones(nd), n_rows=n_all, rope_tabs=tabs, name="proj_dk", **kw)
    vv = _proj(h, w_v, ones(nk + nd), n_rows=n_all, name="proj_odd_v", **kw)
    o_c = _dense_attn(cq_rot, cq_nopos, ck, vv, None, n_batch=n_batch, n_q=seq, q_row0=0, n_k=seq, k_row0=0,
                      ctx_len=ctx_len, ctx_row0=n_lat, n_kv=kvh, g=gq, q_col=0, k_col=0, v_col=0,
                      out_rows=n_lat, name="attn_c")
    lam_vecs = jnp.stack([lambda_q1[0], lambda_k1[0], lambda_q2[0], lambda_k2[0]]).astype(F32)
    o_d = _diff_attn(dq_rot, dq_nopos, dk, vv, lam_vecs, row2(subln_g[0]).astype(F32), n_batch=n_batch,
                     seq=seq, ctx_len=ctx_len, n_latent=n_lat, n_heads=dh, v_col=nk // (2 * HEAD_DIM),
                     lambda_init=lambda_init)
    w_out = w_out_odd[0].astype(BF16)
    x1, h2, idx, wts = _outproj_ln(o_c, o_d, w_out[:na], w_out[na:], xt, g1, sc2, sh2, row2(ln1_g[1]),
                                   row2(ln1_b[1]), w_router_t, router_bias_col, n_rows=n_lat, seq=seq,
                                   n_groups=n_groups, alpha=alpha)
    f = moe(h2, idx, wts, 1)
    x2, _ = _resid_ln(x1, f, g2, row2(ln2_g[1]), row2(ln2_b[1]), None, None, n_rows=n_lat, seq=seq,
                      n_groups=n_groups, alpha=alpha)
    return x2.reshape(n_batch, seq, d)
```

```python
import functools
import math

import numpy as np
import jax
import jax.numpy as jnp
from jax import lax
from jax.experimental import pallas as pl
from jax.experimental.pallas import tpu as pltpu

F32 = jnp.float32
BF16 = jnp.bfloat16

HEAD_DIM = 128
GRID_W = 64
ROPE_THETA = 10000.0
SCALE = HEAD_DIM ** -0.5
NEG_INF = -1e30
A_WINDOW = 128
A_BLOCK = 128
NA_ROWS = 8
NA_COLS = 16
B_TILE_ROWS = 4
B_HEADS_PER_STEP = 4
N_EXPERTS = 16
N_GROUPS = 4
EXPERTS_PER_GROUP = N_EXPERTS // N_GROUPS
LN_EPS = 1e-5
RMS_EPS = 1e-6
LOG2E = math.log2(math.e)
Q_SCALE = SCALE * LOG2E

V7X_VMEM_BYTES = 64 * 1024 * 1024
VMEM_LIMIT = V7X_VMEM_BYTES - 8 * 1024 * 1024
LANES = 128

NT_DIMS = (((1,), (1,)), ((), ()))


def _params(*sem):
    return pltpu.CompilerParams(dimension_semantics=sem, vmem_limit_bytes=VMEM_LIMIT)


def _adaln_kernel(c_ref, w_ref, b_ref, o_ref):
    c = c_ref[...]
    a = (c * (1.0 / (1.0 + jnp.exp(-c)))).astype(BF16)
    o_ref[...] = jnp.dot(a, w_ref[...].astype(BF16), preferred_element_type=F32) + b_ref[...]


def _adaln(c_rows, w_ada, b_ada):
    depth, d, n6 = w_ada.shape
    rows = c_rows.shape[0]
    tn = 1024
    return pl.pallas_call(
        _adaln_kernel,
        out_shape=jax.ShapeDtypeStruct((depth, rows, n6), F32),
        grid=(depth, n6 // tn),
        in_specs=[
            pl.BlockSpec((rows, d), lambda l, j: (0, 0)),
            pl.BlockSpec((None, d, tn), lambda l, j: (l, 0, j)),
            pl.BlockSpec((None, 1, tn), lambda l, j: (l, 0, j)),
        ],
        out_specs=pl.BlockSpec((None, rows, tn), lambda l, j: (l, 0, j)),
        compiler_params=_params("parallel", "parallel"),
        name="adaln",
    )(c_rows, w_ada, b_ada.reshape(depth, 1, n6))


def _two_stream_specs(tm, width, n_lat_tiles):
    return [pl.BlockSpec((tm, width), lambda i: (jnp.minimum(i, n_lat_tiles - 1), 0)),
            pl.BlockSpec((tm, width), lambda i: (jnp.maximum(i - n_lat_tiles, 0), 0))]


def _pick_stream(lat_ref, ctx_ref, n_lat_tiles):
    if ctx_ref is None:
        return lat_ref[...]
    return jnp.where(pl.program_id(0) >= n_lat_tiles, ctx_ref[...], lat_ref[...])


def _modulate_kernel(x_ref, c_ref, sc_ref, sh_ref, o_ref, *, n_lat_tiles):
    x = _pick_stream(x_ref, c_ref, n_lat_tiles)
    o_ref[...] = (x * (1.0 + sc_ref[0]) + sh_ref[0]).astype(o_ref.dtype)


def _modulate(x_lat, x_ctx, sc, sh, *, seq, n_groups):
    d = x_lat.shape[1]
    rows = x_lat.shape[0] + x_ctx.shape[0]
    tm = 512
    n_lat_tiles = x_lat.shape[0] // tm
    grp = lambda i: (jnp.minimum(i // (seq // tm), n_groups - 1), 0, 0)
    return pl.pallas_call(
        functools.partial(_modulate_kernel, n_lat_tiles=n_lat_tiles),
        out_shape=jax.ShapeDtypeStruct((rows, d), BF16),
        grid=(rows // tm,),
        in_specs=_two_stream_specs(tm, d, n_lat_tiles)
        + [pl.BlockSpec((1, 1, d), grp), pl.BlockSpec((1, 1, d), grp)],
        out_specs=pl.BlockSpec((tm, d), lambda i: (i, 0)),
        compiler_params=_params("parallel"),
        name="modulate",
    )(x_lat, x_ctx, sc, sh)


def _cast_kernel(x_ref, o_ref):
    o_ref[...] = x_ref[...].astype(o_ref.dtype)


def _cast_bf16(w):
    shape = w.shape
    w2 = w.reshape(-1, shape[-1])
    tm = min(w2.shape[0], (2 * 1024 * 1024) // shape[-1])
    out = pl.pallas_call(
        _cast_kernel,
        out_shape=jax.ShapeDtypeStruct(w2.shape, BF16),
        grid=(w2.shape[0] // tm,),
        in_specs=[pl.BlockSpec((tm, shape[-1]), lambda i: (i, 0))],
        out_specs=pl.BlockSpec((tm, shape[-1]), lambda i: (i, 0)),
        compiler_params=_params("parallel"),
        name="cast_bf16",
    )(w2)
    return out.reshape(shape)


def _proj_kernel(*refs, tn, norm, rope, emit_nopos):
    it = iter(refs)
    x_ref, w_ref, cs_ref = next(it), next(it), next(it)
    g_ref = next(it) if norm else None
    cos_ref, sin_ref = (next(it), next(it)) if rope else (None, None)
    o_ref = next(it)
    n_ref = next(it) if emit_nopos else None
    acc = jnp.dot(x_ref[...], w_ref[...], preferred_element_type=F32)
    for hd in range(tn // HEAD_DIM):
        sl = slice(hd * HEAD_DIM, (hd + 1) * HEAD_DIM)
        xh = acc[:, sl]
        if norm:
            xh = xh * lax.rsqrt(jnp.mean(xh * xh, axis=-1, keepdims=True) + RMS_EPS) * g_ref[...]
        cs = cs_ref[:, sl]
        if rope:
            rot = xh * cos_ref[...] + pltpu.roll(xh, HEAD_DIM // 2, 1) * sin_ref[...]
            o_ref[:, sl] = (rot * cs).astype(o_ref.dtype)
            if emit_nopos:
                n_ref[:, sl] = (xh * cs).astype(n_ref.dtype)
        else:
            o_ref[:, sl] = (xh * cs).astype(o_ref.dtype)


def _proj(x, w, col_scale, *, n_rows, seq, n_latent, tn=None, gain=None, rope_tabs=None,
          emit_nopos=False, name="proj"):
    d = x.shape[1]
    nc = w.shape[1]
    tm = min(1024, seq)
    tn = nc if tn is None else tn
    norm, rope = gain is not None, rope_tabs is not None
    in_specs = [pl.BlockSpec((tm, d), lambda i, j: (i, 0)),
                pl.BlockSpec((d, tn), lambda i, j: (0, j)),
                pl.BlockSpec((1, tn), lambda i, j: (0, j))]
    args = [x, w, col_scale]
    if norm:
        in_specs.append(pl.BlockSpec((1, HEAD_DIM), lambda i, j: (0, 0)))
        args.append(gain)
    if rope:
        per_seq = seq // tm
        tab = lambda i, j: (jnp.where(i < n_latent // tm, i % per_seq, per_seq), 0)
        in_specs += [pl.BlockSpec((tm, HEAD_DIM), tab)] * 2
        args += list(rope_tabs)
    n_out = 2 if emit_nopos else 1
    out_shape = [jax.ShapeDtypeStruct((n_rows, nc), BF16)] * n_out
    out_specs = [pl.BlockSpec((tm, tn), lambda i, j: (i, j))] * n_out
    res = pl.pallas_call(
        functools.partial(_proj_kernel, tn=tn, norm=norm, rope=rope, emit_nopos=emit_nopos),
        out_shape=out_shape,
        grid=(n_rows // tm, nc // tn),
        in_specs=in_specs,
        out_specs=out_specs,
        compiler_params=_params("parallel", "parallel"),
        name=name,
    )(*args)
    return res if emit_nopos else res[0]


def _stack_heads(x, g):
    if g == 1:
        return x
    return jnp.concatenate([x[:, i * HEAD_DIM:(i + 1) * HEAD_DIM] for i in range(g)], axis=0)


def _unstack_store(o_ref, o, g, t, col0=0):
    for i in range(g):
        o_ref[:, col0 + i * HEAD_DIM:col0 + (i + 1) * HEAD_DIM] = o[i * t:(i + 1) * t].astype(o_ref.dtype)


def _sink_column(sink_ref, first_head, g, t):
    cols = [jnp.full((t, 1), sink_ref[first_head + i] * LOG2E, F32) for i in range(g)]
    return cols[0] if g == 1 else jnp.concatenate(cols, axis=0)


def _with_ones(v):
    return jnp.concatenate([v, jnp.ones_like(v)], axis=1)


def _row_max(pieces):
    cols = [p[:, c:c + LANES] for p in pieces for c in range(0, p.shape[1], LANES)]
    return jnp.max(functools.reduce(jnp.maximum, cols), axis=-1, keepdims=True)


def _online_update(s, v_aug, m, acc):
    m_new = jnp.maximum(m, jnp.max(s, axis=-1, keepdims=True))
    p = jnp.exp2(s - m_new)
    acc = jnp.exp2(m - m_new) * acc + jnp.dot(p.astype(BF16), v_aug, preferred_element_type=F32)
    return m_new, acc


def _band_kernel(*refs, nh, g, bq, has_sink, same_qc, shared_bias):
    it = iter(refs)
    sink_ref = next(it) if has_sink else None
    q_ref = next(it)
    qc_ref = q_ref if same_qc else next(it)
    k_refs = [next(it) for _ in range(3)]
    v_refs = [next(it) for _ in range(3)]
    kx_ref, vx_ref, bias_ref, o_ref = next(it), next(it), next(it), next(it)
    hw = g * HEAD_DIM
    for h in range(nh):
        qs = slice(h * hw, (h + 1) * hw)
        ks = slice(h * HEAD_DIM, (h + 1) * HEAD_DIM)
        q = _stack_heads(q_ref[:, qs], g)
        qc = q if same_qc else _stack_heads(qc_ref[:, qs], g)
        bias = bias_ref.at[0 if shared_bias else h]
        s = [lax.dot_general(q, k_refs[j][:, ks], NT_DIMS, preferred_element_type=F32)
             + bias[:, j * bq:(j + 1) * bq] for j in range(3)]
        s.append(lax.dot_general(qc, kx_ref[:, ks], NT_DIMS, preferred_element_type=F32))
        vs = [r[:, ks] for r in v_refs] + [vx_ref[:, ks]]
        m = _row_max(s)
        if has_sink:
            sink = _sink_column(sink_ref, (pl.program_id(0) * nh + h) * g, g, bq)
            m = jnp.maximum(m, sink)
        o = functools.reduce(jnp.add, [
            jnp.dot(jnp.exp2(sj - m).astype(BF16), _with_ones(vj), preferred_element_type=F32)
            for sj, vj in zip(s, vs)])
        den = o[:, HEAD_DIM:]
        if has_sink:
            den = den + jnp.exp2(sink - m)
        _unstack_store(o_ref, o[:, :HEAD_DIM] / den, g, bq, col0=h * hw)


def _band_attn(q, qc, k, v, bias, sink, *, n_batch, seq, ctx_len, n_latent, n_kv, nh, g, bq,
               q_col, k_col, v_col, name):
    nblk = seq // bq
    ctx_blk = n_latent // ctx_len
    shared_bias = bias.shape[0] == 1
    has_sink, same_qc = sink is not None, qc is None
    qw, kw = nh * g * HEAD_DIM, nh * HEAD_DIM

    def q_map(hg, i, b):
        return (b * nblk + i, q_col + hg)

    def kv_map(col, off):
        return lambda hg, i, b: (b * nblk + jnp.clip(i + off, 0, nblk - 1), col + hg)

    def ctx_map(col):
        return lambda hg, i, b: (ctx_blk + b, col + hg)

    def bias_map(hg, i, b):
        case = jnp.where(i == 0, 0, jnp.where(i == nblk - 1, 2, 1))
        return (0 if shared_bias else hg, case, 0, 0)

    in_specs, args = [], []
    if has_sink:
        in_specs.append(pl.BlockSpec(memory_space=pltpu.SMEM))
        args.append(sink)
    in_specs.append(pl.BlockSpec((bq, qw), q_map))
    args.append(q)
    if not same_qc:
        in_specs.append(pl.BlockSpec((bq, qw), q_map))
        args.append(qc)
    in_specs += [pl.BlockSpec((bq, kw), kv_map(k_col, off)) for off in (-1, 0, 1)]
    args += [k] * 3
    in_specs += [pl.BlockSpec((bq, kw), kv_map(v_col, off)) for off in (-1, 0, 1)]
    args += [v] * 3
    in_specs += [pl.BlockSpec((ctx_len, kw), ctx_map(k_col)),
                 pl.BlockSpec((ctx_len, kw), ctx_map(v_col)),
                 pl.BlockSpec((1 if shared_bias else nh, None, g * bq, 3 * bq), bias_map)]
    args += [k, v, bias]
    return pl.pallas_call(
        functools.partial(_band_kernel, nh=nh, g=g, bq=bq, has_sink=has_sink, same_qc=same_qc,
                          shared_bias=shared_bias),
        out_shape=jax.ShapeDtypeStruct((n_latent, n_kv * g * HEAD_DIM), BF16),
        grid=(n_kv // nh, nblk, n_batch),
        in_specs=in_specs,
        out_specs=pl.BlockSpec((bq, qw), lambda hg, i, b: (b * nblk + i, hg)),
        compiler_params=_params("parallel", "parallel", "parallel"),
        name=name,
    )(*args)


def _dense_kernel(*refs, g, tq, tk, n_k, has_ctx, has_sink):
    it = iter(refs)
    sink_ref = next(it) if has_sink else None
    q_ref = next(it)
    qc_ref = next(it) if has_ctx else None
    k_ref, v_ref = next(it), next(it)
    kx_ref, vx_ref = (next(it), next(it)) if has_ctx else (None, None)
    o_ref = next(it)
    q = _stack_heads(q_ref[...], g)
    rows = g * tq
    m = jnp.full((rows, 1), NEG_INF, F32)
    acc = jnp.zeros((rows, 2 * HEAD_DIM), F32)
    for c in range(n_k // tk):
        s = lax.dot_general(q, k_ref[c * tk:(c + 1) * tk, :], NT_DIMS, preferred_element_type=F32)
        m, acc = _online_update(s, _with_ones(v_ref[c * tk:(c + 1) * tk, :]), m, acc)
    if has_ctx:
        qc = _stack_heads(qc_ref[...], g)
        s = lax.dot_general(qc, kx_ref[...], NT_DIMS, preferred_element_type=F32)
        m, acc = _online_update(s, _with_ones(vx_ref[...]), m, acc)
    num, den = acc[:, :HEAD_DIM], acc[:, HEAD_DIM:]
    if has_sink:
        sink = _sink_column(sink_ref, pl.program_id(1) * g, g, tq)
        m_new = jnp.maximum(m, sink)
        alpha = jnp.exp2(m - m_new)
        num = alpha * num
        den = alpha * den + jnp.exp2(sink - m_new)
    _unstack_store(o_ref, num / den, g, tq)


def _dense_attn(q, qc, k, v, sink, *, n_batch, n_q, q_row0, n_k, k_row0, ctx_len, ctx_row0,
                n_kv, g, q_col, k_col, v_col, name):
    tq = min(256, n_q)
    tk = min(512, n_k)
    nq_blk = n_q // tq
    has_ctx, has_sink = qc is not None, sink is not None

    def q_map(b, h, i):
        return (q_row0 // tq + b * nq_blk + i, q_col + h)

    in_specs, args = [], []
    if has_sink:
        in_specs.append(pl.BlockSpec(memory_space=pltpu.SMEM))
        args.append(sink)
    in_specs.append(pl.BlockSpec((tq, g * HEAD_DIM), q_map))
    args.append(q)
    if has_ctx:
        in_specs.append(pl.BlockSpec((tq, g * HEAD_DIM), q_map))
        args.append(qc)
    in_specs += [pl.BlockSpec((n_k, HEAD_DIM), lambda b, h, i: (k_row0 // n_k + b, k_col + h)),
                 pl.BlockSpec((n_k, HEAD_DIM), lambda b, h, i: (k_row0 // n_k + b, v_col + h))]
    args += [k, v]
    if has_ctx:
        in_specs += [pl.BlockSpec((ctx_len, HEAD_DIM), lambda b, h, i: (ctx_row0 // ctx_len + b, k_col + h)),
                     pl.BlockSpec((ctx_len, HEAD_DIM), lambda b, h, i: (ctx_row0 // ctx_len + b, v_col + h))]
        args += [k, v]
    return pl.pallas_call(
        functools.partial(_dense_kernel, g=g, tq=tq, tk=tk, n_k=n_k, has_ctx=has_ctx,
                          has_sink=has_sink),
        out_shape=jax.ShapeDtypeStruct((n_batch * n_q, n_kv * g * HEAD_DIM), BF16),
        grid=(n_batch, n_kv, nq_blk),
        in_specs=in_specs,
        out_specs=pl.BlockSpec((tq, g * HEAD_DIM), lambda b, h, i: (b * nq_blk + i, h)),
        compiler_params=_params("parallel", "parallel", "parallel"),
        name=name,
    )(*args)


def _diff_update(s, v, m, l, acc):
    m_new = jnp.maximum(m, jnp.max(s, axis=-1, keepdims=True))
    alpha = jnp.exp2(m - m_new)
    p = jnp.exp2(s - m_new)
    l = alpha * l + jnp.sum(p, axis=-1, keepdims=True)
    acc = alpha * acc + jnp.dot(p.astype(BF16), v, preferred_element_type=F32)
    return m_new, l, acc


def _diff_kernel(lam_ref, g_ref, q_ref, qc_ref, k_ref, v_ref, kx_ref, vx_ref, o_ref, *,
                 tq, tk, n_k, lambda_init):
    dv = 2 * HEAD_DIM
    lam = (jnp.exp(jnp.sum(lam_ref[0:1, :] * lam_ref[1:2, :], axis=1, keepdims=True))
           - jnp.exp(jnp.sum(lam_ref[2:3, :] * lam_ref[3:4, :], axis=1, keepdims=True)) + lambda_init)
    outs = []
    for t in range(2):
        sl = slice(t * HEAD_DIM, (t + 1) * HEAD_DIM)
        q = q_ref[:, sl]
        m = jnp.full((tq, 1), NEG_INF, F32)
        l = jnp.zeros((tq, 1), F32)
        acc = jnp.zeros((tq, dv), F32)
        for c in range(n_k // tk):
            rows = slice(c * tk, (c + 1) * tk)
            s = lax.dot_general(q, k_ref[rows, sl], NT_DIMS, preferred_element_type=F32)
            m, l, acc = _diff_update(s, v_ref[rows, :], m, l, acc)
        s = lax.dot_general(qc_ref[:, sl], kx_ref[:, sl], NT_DIMS, preferred_element_type=F32)
        m, l, acc = _diff_update(s, vx_ref[...], m, l, acc)
        outs.append(acc / l)
    o = outs[0] - lam * outs[1]
    o = o * lax.rsqrt(jnp.mean(o * o, axis=-1, keepdims=True) + RMS_EPS) * g_ref[...]
    o_ref[...] = (o * (1.0 - lambda_init)).astype(o_ref.dtype)


def _diff_attn(q, qc, k, v, lam_vecs, subln_g, *, n_batch, seq, ctx_len, n_latent, n_heads,
               v_col, lambda_init):
    tq = min(512, seq)
    tk = min(512, seq)
    dv = 2 * HEAD_DIM
    nq_blk = seq // tq
    ctx_blk = n_latent // ctx_len
    q_map = lambda b, h, i: (b * nq_blk + i, h)
    return pl.pallas_call(
        functools.partial(_diff_kernel, tq=tq, tk=tk, n_k=seq, lambda_init=lambda_init),
        out_shape=jax.ShapeDtypeStruct((n_latent, n_heads * dv), BF16),
        grid=(n_batch, n_heads, nq_blk),
        in_specs=[
            pl.BlockSpec((4, HEAD_DIM), lambda b, h, i: (0, 0)),
            pl.BlockSpec((1, dv), lambda b, h, i: (0, 0)),
            pl.BlockSpec((tq, dv), q_map),
            pl.BlockSpec((tq, dv), q_map),
            pl.BlockSpec((seq, dv), lambda b, h, i: (b, h)),
            pl.BlockSpec((seq, dv), lambda b, h, i: (b, v_col + h)),
            pl.BlockSpec((ctx_len, dv), lambda b, h, i: (ctx_blk + b, h)),
            pl.BlockSpec((ctx_len, dv), lambda b, h, i: (ctx_blk + b, v_col + h)),
        ],
        out_specs=pl.BlockSpec((tq, dv), q_map),
        compiler_params=_params("parallel", "parallel", "parallel"),
        name="diff_attn",
    )(lam_vecs, subln_g, q, qc, k, v, k, v)


def _layer_norm(z, g, b):
    mu = jnp.mean(z, axis=-1, keepdims=True)
    zc = z - mu
    var = jnp.mean(zc * zc, axis=-1, keepdims=True)
    return zc * lax.rsqrt(var + LN_EPS) * g + b


def _split_bf16(x):
    hi = x.astype(BF16)
    return hi, (x - hi.astype(F32)).astype(BF16)


def _first_argmax(vals, idx, width):
    m = jnp.max(vals, axis=0, keepdims=True)
    first = jnp.min(jnp.where(vals == m, idx, float(width)), axis=0, keepdims=True)
    return m, first


def _route(logits, bias):
    e, t = logits.shape
    scores = 1.0 / (1.0 + jnp.exp(-logits))
    biased = scores + bias
    row_i = lax.broadcasted_iota(jnp.int32, (e, t), 0)
    grp = lax.shift_right_logical(row_i, int(math.log2(EXPERTS_PER_GROUP)))
    row = row_i.astype(F32)
    neg = -jnp.inf
    best_score, best = None, None
    for gi in range(N_GROUPS):
        vg = jnp.where(grp == gi, biased, neg)
        m1, i1 = _first_argmax(vg, row, e)
        m2 = jnp.max(jnp.where(row == i1, neg, vg), axis=0, keepdims=True)
        gs = m1 + m2
        if gi == 0:
            best_score, best = gs, jnp.zeros((1, t), jnp.int32)
        else:
            upd = gs > best_score
            best = jnp.where(upd, gi, best)
            best_score = jnp.where(upd, gs, best_score)
    masked = jnp.where(grp == best, biased, neg)
    _, e1 = _first_argmax(masked, row, e)
    _, e2 = _first_argmax(jnp.where(row == e1, neg, masked), row, e)
    w1 = jnp.sum(jnp.where(row == e1, scores, 0.0), axis=0, keepdims=True)
    w2 = jnp.sum(jnp.where(row == e2, scores, 0.0), axis=0, keepdims=True)
    den = w1 + w2
    two = lax.broadcasted_iota(jnp.int32, (2, t), 0)
    return jnp.where(two == 0, e1, e2).astype(jnp.int32), jnp.where(two == 0, w1 / den, w2 / den)


def _outproj_kernel(*refs, alpha, two_streams, n_lat_tiles):
    it = iter(refs)
    xa_ref, xa_ctx = next(it), (next(it) if two_streams else None)
    xb_ref, xb_ctx = next(it), (next(it) if two_streams else None)
    xres_ref, xres_ctx = next(it), (next(it) if two_streams else None)
    (wa_ref, wb_ref, g1_ref, sc2_ref, sh2_ref, lng_ref, lnb_ref, wrt_ref, rb_ref,
     x1_ref, h2_ref, idx_ref, wts_ref) = it
    xa = _pick_stream(xa_ref, xa_ctx, n_lat_tiles)
    xb = _pick_stream(xb_ref, xb_ctx, n_lat_tiles)
    xres = _pick_stream(xres_ref, xres_ctx, n_lat_tiles)
    y = (jnp.dot(xa, wa_ref[...], preferred_element_type=F32)
         + jnp.dot(xb, wb_ref[...], preferred_element_type=F32))
    x1 = _layer_norm(alpha * xres + g1_ref[0] * y, lng_ref[...], lnb_ref[...])
    x1_ref[...] = x1
    h2 = x1 * (1.0 + sc2_ref[0]) + sh2_ref[0]
    h2_ref[...] = h2
    h_hi, h_lo = _split_bf16(h2)
    w_hi, w_lo = _split_bf16(wrt_ref[...])
    part = lax.dot_general(jnp.concatenate([w_hi, w_lo], axis=0), h_hi, NT_DIMS, preferred_element_type=F32)
    logits = (part[:N_EXPERTS] + part[N_EXPERTS:]
              + lax.dot_general(w_hi, h_lo, NT_DIMS, preferred_element_type=F32))
    idx, wts = _route(logits, rb_ref[...])
    idx_ref[...] = idx
    wts_ref[...] = wts


def _outproj_ln(xa, xb, xres, wa, wb, g1, sc2, sh2, ln_g, ln_b, w_router_t, router_bias, *,
                n_lat, seq, n_groups, alpha):
    two_streams = xa[1] is not None
    d = xres[0].shape[1]
    ka, kb = xa[0].shape[1], xb[0].shape[1]
    tm = 256
    n_lat_tiles = n_lat // tm
    n_rows = n_lat + (xres[1].shape[0] if two_streams else 0)
    row = lambda i: (i, 0)
    col = lambda i: (0, i)
    const = lambda i: (0, 0)
    grp = lambda i: (jnp.minimum(i // (seq // tm), n_groups - 1), 0, 0)
    in_specs, args = [], []
    for pair, width in ((xa, ka), (xb, kb), (xres, d)):
        if two_streams:
            in_specs += _two_stream_specs(tm, width, n_lat_tiles)
            args += list(pair)
        else:
            in_specs.append(pl.BlockSpec((tm, width), row))
            args.append(pair[0])
    in_specs += [pl.BlockSpec((ka, d), const), pl.BlockSpec((kb, d), const),
                 pl.BlockSpec((1, 1, d), grp), pl.BlockSpec((1, 1, d), grp), pl.BlockSpec((1, 1, d), grp),
                 pl.BlockSpec((1, d), const), pl.BlockSpec((1, d), const),
                 pl.BlockSpec((N_EXPERTS, d), const), pl.BlockSpec((N_EXPERTS, 1), const)]
    args += [wa, wb, g1, sc2, sh2, ln_g, ln_b, w_router_t, router_bias]
    return pl.pallas_call(
        functools.partial(_outproj_kernel, alpha=alpha, two_streams=two_streams, n_lat_tiles=n_lat_tiles),
        out_shape=[jax.ShapeDtypeStruct((n_rows, d), F32), jax.ShapeDtypeStruct((n_rows, d), F32),
                   jax.ShapeDtypeStruct((2, n_rows), jnp.int32), jax.ShapeDtypeStruct((2, n_rows), F32)],
        grid=(n_rows // tm,),
        in_specs=in_specs,
        out_specs=[pl.BlockSpec((tm, d), row), pl.BlockSpec((tm, d), row),
                   pl.BlockSpec((2, tm), col), pl.BlockSpec((2, tm), col)],
        compiler_params=_params("parallel"),
        name="outproj_ln",
    )(*args)


def _moe_kernel(te_ref, nv_ref, src_ref, dst_ref, h_hbm, gw_ref, wg_ref, wu_ref, wd_ref, y_hbm,
                xbuf, obuf, gsem, ssem, *, tm, ff_chunks):
    i = pl.program_id(0)
    n = pl.num_programs(0)
    nv = nv_ref[0]
    slot = jnp.bitwise_and(i, 1)

    def start_gather(tile, s):
        base = tile * (tm // 2)
        for r in range(tm):
            word = src_ref[base + r // 2]
            tok = jnp.bitwise_and(word, 0xFFFF) if r % 2 == 0 else lax.shift_right_logical(word, 16)
            pltpu.make_async_copy(h_hbm.at[pl.ds(tok, 1)], xbuf.at[s, pl.ds(r, 1)], gsem.at[s]).start()

    def wait_gather(s):
        pltpu.make_async_copy(h_hbm.at[pl.ds(0, tm)], xbuf.at[s], gsem.at[s]).wait()

    def start_scatter(tile, s):
        base = tile * tm
        for r in range(tm):
            pltpu.make_async_copy(obuf.at[s, pl.ds(r, 1)], y_hbm.at[pl.ds(dst_ref[base + r], 1)],
                                  ssem.at[s]).start()

    def wait_scatter(s):
        pltpu.make_async_copy(obuf.at[s], y_hbm.at[pl.ds(0, tm)], ssem.at[s]).wait()

    @pl.when(i == 0)
    def _():
        start_gather(0, 0)
        obuf[...] = jnp.zeros_like(obuf)
        pad0 = y_hbm.shape[0] - 2 * tm
        fills = [pltpu.make_async_copy(obuf.at[s], y_hbm.at[pl.ds(pad0 + s * tm, tm)], ssem.at[s])
                 for s in range(2)]
        for f in fills:
            f.start()
        for f in fills:
            f.wait()

    @pl.when(i < nv)
    def _():
        wait_gather(slot)

        @pl.when(i >= 2)
        def _():
            wait_scatter(slot)

        start_gather(jnp.minimum(i + 1, n - 1), 1 - slot)
        x = xbuf[slot].astype(BF16)
        fc = wg_ref.shape[1] // ff_chunks
        y = None
        for c in range(ff_chunks):
            cs = slice(c * fc, (c + 1) * fc)
            gate = jnp.dot(x, wg_ref[:, cs], preferred_element_type=F32)
            up = jnp.dot(x, wu_ref[:, cs], preferred_element_type=F32)
            act = (gate * (1.0 / (1.0 + jnp.exp(-gate))) * up).astype(BF16)
            part = jnp.dot(act, wd_ref[cs, :], preferred_element_type=F32)
            y = part if y is None else y + part
        obuf[slot] = y * gw_ref[...]
        start_scatter(i, slot)

    @pl.when(i == n - 1)
    def _():
        wait_gather(jnp.bitwise_and(nv, 1))

        @pl.when(nv >= 1)
        def _():
            wait_scatter(jnp.bitwise_and(nv - 1, 1))

        @pl.when(nv >= 2)
        def _():
            wait_scatter(jnp.bitwise_and(nv, 1))


def _moe_ffn(h2, src_packed, dst_row, gw, tile_expert, n_valid, w_gate, w_up, w_down, *, layer, tm):
    t, d = h2.shape
    ff = w_gate.shape[3]
    n_tiles = dst_row.shape[0] // tm
    return pl.pallas_call(
        functools.partial(_moe_kernel, tm=tm, ff_chunks=2),
        out_shape=jax.ShapeDtypeStruct((2 * t + 2 * tm, d), F32),
        grid_spec=pltpu.PrefetchScalarGridSpec(
            num_scalar_prefetch=4,
            grid=(n_tiles,),
            in_specs=[pl.BlockSpec(memory_space=pl.ANY),
                      pl.BlockSpec((tm, 1), lambda i, te, nv, src, dst: (i, 0)),
                      pl.BlockSpec((None, None, d, ff), lambda i, te, nv, src, dst: (layer, te[i], 0, 0)),
                      pl.BlockSpec((None, None, d, ff), lambda i, te, nv, src, dst: (layer, te[i], 0, 0)),
                      pl.BlockSpec((None, None, ff, d), lambda i, te, nv, src, dst: (layer, te[i], 0, 0))],
            out_specs=pl.BlockSpec(memory_space=pl.ANY),
            scratch_shapes=[pltpu.VMEM((2, tm, d), F32), pltpu.VMEM((2, tm, d), F32),
                            pltpu.SemaphoreType.DMA((2,)), pltpu.SemaphoreType.DMA((2,))],
        ),
        compiler_params=_params("arbitrary"),
        name="moe_ffn",
    )(tile_expert, n_valid, src_packed, dst_row, h2, gw, w_gate, w_up, w_down)


def _take(x, idx):
    return x.at[idx].get(mode="promise_in_bounds")


def _dispatch(idx, wts, *, tm):
    t = idx.shape[1]
    n_pairs = 2 * t
    n_tiles = n_pairs // tm + N_EXPERTS
    e_flat = idx.reshape(n_pairs)
    w_flat = wts.reshape(n_pairs)
    order = jnp.argsort(e_flat, stable=True).astype(jnp.int32)
    counts = jnp.sum(e_flat[:, None] == jnp.arange(N_EXPERTS)[None, :], axis=0).astype(jnp.int32)
    start = jnp.cumsum(counts) - counts
    padded = ((counts + tm - 1) // tm) * tm
    pend = jnp.cumsum(padded)
    pstart = pend - padded
    pos = jnp.arange(n_tiles * tm, dtype=jnp.int32)
    e_pos = jnp.minimum(jnp.sum(pos[:, None] >= pend[None, :], axis=1), N_EXPERTS - 1).astype(jnp.int32)
    local = pos - pstart[e_pos]
    valid = local < counts[e_pos]
    src_pair = _take(order, jnp.clip(start[e_pos] + local, 0, n_pairs - 1))
    src_tok = jnp.where(valid, jnp.where(src_pair >= t, src_pair - t, src_pair), 0)
    dst_row = jnp.where(valid, src_pair, n_pairs + pos % (2 * tm))
    gw = jnp.where(valid, _take(w_flat, src_pair), 0.0).astype(F32)
    src_packed = jnp.bitwise_or(src_tok[0::2], jnp.left_shift(src_tok[1::2], 16))
    tile_expert = e_pos[::tm]
    n_valid = (pend[-1] // tm).astype(jnp.int32).reshape(1)
    return src_packed, dst_row, gw.reshape(-1, 1), tile_expert, n_valid


def _resid_ln_kernel(*refs, alpha, emit_h):
    it = iter(refs)
    x_ref, f0_ref, f1_ref, g_ref, lng_ref, lnb_ref = (next(it) for _ in range(6))
    sc_ref, sh_ref = (next(it), next(it)) if emit_h else (None, None)
    o_ref = next(it)
    f = f0_ref[...] + f1_ref[...]
    x2 = _layer_norm(alpha * x_ref[...] + g_ref[0] * f, lng_ref[...], lnb_ref[...])
    o_ref[...] = x2
    if emit_h:
        h_ref = next(it)
        h_ref[...] = (x2 * (1.0 + sc_ref[0]) + sh_ref[0]).astype(h_ref.dtype)


def _resid_ln(x, y, gate, ln_g, ln_b, next_sc, next_sh, *, n_rows, seq, n_groups, alpha):
    d = x.shape[1]
    tm = 512
    emit_h = next_sc is not None
    row = lambda i: (i, 0)
    const = lambda i: (0, 0)
    grp = lambda i: (jnp.minimum(i // (seq // tm), n_groups - 1), 0, 0)
    in_specs = [pl.BlockSpec((tm, d), row), pl.BlockSpec((tm, d), row),
                pl.BlockSpec((tm, d), lambda i: (n_rows // tm + i, 0)), pl.BlockSpec((1, 1, d), grp),
                pl.BlockSpec((1, d), const), pl.BlockSpec((1, d), const)]
    args = [x, y, y, gate, ln_g, ln_b]
    out_shape = [jax.ShapeDtypeStruct((n_rows, d), F32)]
    out_specs = [pl.BlockSpec((tm, d), row)]
    if emit_h:
        in_specs += [pl.BlockSpec((1, 1, d), grp)] * 2
        args += [next_sc, next_sh]
        out_shape.append(jax.ShapeDtypeStruct((n_rows, d), BF16))
        out_specs.append(pl.BlockSpec((tm, d), row))
    res = pl.pallas_call(
        functools.partial(_resid_ln_kernel, alpha=alpha, emit_h=emit_h),
        out_shape=out_shape,
        grid=(n_rows // tm,),
        in_specs=in_specs,
        out_specs=out_specs,
        compiler_params=_params("parallel"),
        name="resid_ln",
    )(*args)
    return res if emit_h else (res[0], None)


def _rope_tables(seq, tm):
    t = jnp.arange(seq, dtype=jnp.int32)
    n_freq = HEAD_DIM // 4
    inv = ROPE_THETA ** (-jnp.arange(n_freq, dtype=F32) / n_freq)
    row = (t // GRID_W).astype(F32)
    col = (t % GRID_W).astype(F32)
    ang = jnp.concatenate([row[:, None] * inv[None], col[:, None] * inv[None]], -1)
    cos, sin = jnp.cos(ang), jnp.sin(ang)
    cos2 = jnp.concatenate([cos, cos], -1)
    sin2 = jnp.concatenate([-sin, sin], -1)
    cos2 = jnp.concatenate([cos2, jnp.ones((tm, HEAD_DIM), F32)], 0)
    sin2 = jnp.concatenate([sin2, jnp.zeros((tm, HEAD_DIM), F32)], 0)
    return cos2, sin2


def _window_bias(seq, g):
    bq = A_BLOCK
    nblk = seq // bq
    rel = np.arange(3 * bq)[None, :] - np.arange(bq)[:, None]
    band = (rel >= bq - A_WINDOW) & (rel <= bq + A_WINDOW)
    tabs = []
    for blk in (0, min(1, nblk - 1), nblk - 1):
        kpos = blk * bq - bq + np.arange(3 * bq)
        ok = band & ((kpos >= 0) & (kpos < seq))[None, :]
        tabs.append(np.tile(np.where(ok, 0.0, NEG_INF).astype(np.float32), (g, 1)))
    return jnp.asarray(np.stack(tabs)[None])


def _neighbourhood_bias(rpb, seq):
    n_heads = rpb.shape[0]
    rows = seq // GRID_W
    kh, kw = min(NA_ROWS, rows), NA_COLS
    tr = B_TILE_ROWS
    nblk = rows // tr
    col = np.arange(GRID_W)
    col_start = np.clip(col - kw // 2, 0, GRID_W - kw)
    col_ok = (col[None, :] >= col_start[:, None]) & (col[None, :] < col_start[:, None] + kw)
    assert np.all(np.abs(col[None, :] - col[:, None])[col_ok] <= NA_COLS - 1)
    masks = []
    for blk in (0, min(1, nblk - 1), nblk - 1):
        r = blk * tr + np.arange(tr)
        r0 = np.clip(r - kh // 2, 0, rows - kh)
        krow = (blk - 1) * tr + np.arange(3 * tr)
        row_ok = (krow[None, :] >= r0[:, None]) & (krow[None, :] < r0[:, None] + kh)
        ok = row_ok[:, None, :, None] & col_ok[None, :, None, :]
        masks.append(ok.reshape(tr * GRID_W, 3 * tr * GRID_W))
    ok = jnp.asarray(np.stack(masks))
    pad = GRID_W - NA_COLS
    rp = jnp.pad(rpb.astype(F32) * LOG2E, ((0, 0), (0, 0), (pad, pad)))
    cexp = jnp.stack([rp[:, :, GRID_W - 1 - cq:2 * GRID_W - 1 - cq] for cq in range(GRID_W)], axis=2)
    off = NA_ROWS - 1 - tr
    assert off - (tr - 1) >= 0 and off + 3 * tr - 1 <= 2 * NA_ROWS - 2
    t5 = jnp.stack([cexp[:, off - rq:off - rq + 3 * tr] for rq in range(tr)], axis=1)
    tab = t5.transpose(0, 1, 3, 2, 4).reshape(n_heads, tr * GRID_W, 3 * tr * GRID_W)
    return jnp.where(ok[None], tab[:, None], NEG_INF)


def _deinterleave_cols(n_heads):
    one = np.concatenate([np.arange(0, HEAD_DIM, 2), np.arange(1, HEAD_DIM, 2)])
    return np.concatenate([h * HEAD_DIM + one for h in range(n_heads)])


def kernel(x, c, ctx, c_ctx, w_ada, b_ada, ln1_g, ln1_b, ln2_g, ln2_b, w_in_even, w_out_even, sink_logits, na_rpb, w_in_odd, w_out_odd, q_norm_g, k_norm_g, lambda_q1, lambda_k1, lambda_q2, lambda_k2, subln_g, w_router, router_bias, w_exp_gate, w_exp_up, w_exp_down):
    n_batch, seq, d = x.shape
    ctx_len = ctx.shape[1]
    depth = w_ada.shape[0]
    assert depth == 2, "one even (A||B) layer followed by one odd (C||D) layer"
    n_lat = n_batch * seq
    n_ctx = n_batch * ctx_len
    n_all = n_lat + n_ctx
    n_groups = n_batch + 1
    alpha = float((2 * depth) ** 0.25)
    heads = d // HEAD_DIM
    qh = heads // 2
    kvh = qh // 4
    gq = qh // kvh
    bh = heads - qh
    dh = (heads - qh) // 2
    moe_tm = 512
    proj_tm = min(1024, seq)
    assert seq % GRID_W == 0 and (seq // GRID_W) % B_TILE_ROWS == 0 and seq // GRID_W >= NA_ROWS
    assert n_ctx % proj_tm == 0 and n_lat % ctx_len == 0 and (2 * n_all) % moe_tm == 0
    assert ctx_len % LANES == 0 and bh % B_HEADS_PER_STEP == 0
    assert n_all <= 1 << 16, "token ids are packed as 16-bit halves for the MoE row gather"

    pad_rows = -(-n_groups // 16) * 16
    c_rows = jnp.concatenate([c, c_ctx[None], jnp.zeros((pad_rows - n_groups, d), F32)], 0)
    mod = _adaln(c_rows, w_ada, b_ada)[:, :n_groups]
    mod = mod.reshape(depth, n_groups, 6, 1, d)
    mods = [[mod[l, :, j] for j in range(6)] for l in range(depth)]

    cos2, sin2 = _rope_tables(seq, proj_tm)
    tabs = (cos2, sin2)
    x_lat, x_ctx = x.reshape(n_lat, d), ctx.reshape(n_ctx, d)
    row2 = lambda v: v.reshape(1, -1)
    ones = lambda n: jnp.ones((1, n), F32)
    qscale = lambda n: jnp.full((1, n), Q_SCALE, F32)
    w_router_t = w_router.T.astype(F32)
    router_bias_col = router_bias.reshape(-1, 1).astype(F32)
    wg_all, wu_all, wd_all = (_cast_bf16(w) for w in (w_exp_gate, w_exp_up, w_exp_down))

    def moe(h2, idx, wts, layer):
        src_packed, dst_row, gw, tile_expert, n_valid = _dispatch(idx, wts, tm=moe_tm)
        return _moe_ffn(h2, src_packed, dst_row, gw, tile_expert, n_valid, wg_all, wu_all, wd_all,
                        layer=layer, tm=moe_tm)

    sh1, sc1, g1, sh2, sc2, g2 = mods[0]
    h = _modulate(x_lat, x_ctx, sc1, sh1, seq=seq, n_groups=n_groups)
    w_in = w_in_even[0]
    na, nk, nb = qh * HEAD_DIM, kvh * HEAD_DIM, bh * HEAD_DIM
    perm_q, perm_k = _deinterleave_cols(qh), _deinterleave_cols(kvh)
    w_aq = w_in[:, :na][:, perm_q].astype(BF16)
    w_ak = w_in[:, na:na + nk][:, perm_k].astype(BF16)
    w_rest = jnp.concatenate([w_in[:, na + 2 * nk:], w_in[:, na + nk:na + 2 * nk]], 1).astype(BF16)
    rest_scale = jnp.concatenate([qscale(nb), ones(2 * nb + nk)], 1)
    kw = dict(seq=seq, n_latent=n_lat)
    aq_rot, aq_nopos = _proj(h, w_aq, qscale(na), n_rows=n_all, rope_tabs=tabs, emit_nopos=True,
                             name="proj_aq", **kw)
    ak = _proj(h, w_ak, ones(nk), n_rows=n_all, rope_tabs=tabs, name="proj_ak", **kw)
    rest = _proj(h, w_rest, rest_scale, n_rows=n_all, tn=w_rest.shape[1] // 2, name="proj_even_rest", **kw)
    sink = sink_logits[0].astype(F32)
    akw = dict(n_batch=n_batch, seq=seq, ctx_len=ctx_len, n_latent=n_lat)
    o_a = _band_attn(aq_rot, aq_nopos, ak, rest, _window_bias(seq, gq), sink, n_kv=kvh, nh=kvh, g=gq,
                     bq=A_BLOCK, q_col=0, k_col=0, v_col=3 * nb // nk, name="attn_a", **akw)
    bw = B_HEADS_PER_STEP * HEAD_DIM
    o_b = _band_attn(rest, None, rest, rest, _neighbourhood_bias(na_rpb[0], seq), None, n_kv=bh,
                     nh=B_HEADS_PER_STEP, g=1, bq=B_TILE_ROWS * GRID_W, q_col=0, k_col=nb // bw,
                     v_col=2 * nb // bw, name="attn_b", **akw)
    ckw = dict(n_batch=n_batch, n_q=ctx_len, q_row0=n_lat, n_k=ctx_len, k_row0=n_lat, ctx_len=ctx_len,
               ctx_row0=n_lat)
    oc_a = _dense_attn(aq_rot, None, ak, rest, sink, n_kv=kvh, g=gq, q_col=0, k_col=0,
                       v_col=3 * nb // HEAD_DIM, name="ctx_attn_a", **ckw)
    oc_b = _dense_attn(rest, None, rest, rest, None, n_kv=bh, g=1, q_col=0, k_col=bh, v_col=2 * bh,
                       name="ctx_attn_b", **ckw)
    w_out = w_out_even[0].astype(BF16)
    x1, h2, idx, wts = _outproj_ln((o_a, oc_a), (o_b, oc_b), (x_lat, x_ctx), w_out[:na], w_out[na:], g1, sc2,
                                   sh2, row2(ln1_g[0]), row2(ln1_b[0]), w_router_t, router_bias_col,
                                   n_lat=n_lat, seq=seq, n_groups=n_groups, alpha=alpha)
    y = moe(h2, idx, wts, 0)
    nsh1, nsc1 = mods[1][0], mods[1][1]
    xt, h = _resid_ln(x1, y, g2, row2(ln2_g[0]), row2(ln2_b[0]), nsc1, nsh1, n_rows=n_all, seq=seq,
                      n_groups=n_groups, alpha=alpha)

    sh1, sc1, g1, sh2, sc2, g2 = mods[1]
    lambda_init = 0.8 - 0.6 * math.exp(-0.3 * 1)
    w_in = w_in_odd[0]
    nd = dh * 2 * HEAD_DIM
    o0 = na + 2 * nk
    perm_d = _deinterleave_cols(2 * dh)
    perm_one = _deinterleave_cols(1)
    w_cq = w_in[:, :na][:, perm_q].astype(BF16)
    w_ck = w_in[:, na:na + nk][:, perm_k].astype(BF16)
    w_dq = w_in[:, o0:o0 + nd][:, perm_d].astype(BF16)
    w_dk = w_in[:, o0 + nd:o0 + 2 * nd][:, perm_d].astype(BF16)
    w_v = jnp.concatenate([w_in[:, na + nk:o0], w_in[:, o0 + 2 * nd:]], 1).astype(BF16)
    qg = row2(q_norm_g[0][perm_one]).astype(F32)
    kg = row2(k_norm_g[0][perm_one]).astype(F32)
    cq_rot, cq_nopos = _proj(h, w_cq, qscale(na), n_rows=n_lat, gain=qg, rope_tabs=tabs, emit_nopos=True,
                             name="proj_cq", **kw)
    ck = _proj(h, w_ck, ones(nk), n_rows=n_all, gain=kg, rope_tabs=tabs, name="proj_ck", **kw)
    dq_rot, dq_nopos = _proj(h, w_dq, qscale(nd), n_rows=n_lat, rope_tabs=tabs, emit_nopos=True,
                             name="proj_dq", **kw)
    dk = _proj(h, w_dk, ones(nd), n_rows=n_all, rope_tabs=tabs, name="proj_dk", **kw)
    vv = _proj(h, w_v, ones(nk + nd), n_rows=n_all, name="proj_odd_v", **kw)
    o_c = _dense_attn(cq_rot, cq_nopos, ck, vv, None, n_batch=n_batch, n_q=seq, q_row0=0, n_k=seq, k_row0=0,
                      ctx_len=ctx_len, ctx_row0=n_lat, n_kv=kvh, g=gq, q_col=0, k_col=0, v_col=0,
                      name="attn_c")
    lam_vecs = jnp.stack([lambda_q1[0], lambda_k1[0], lambda_q2[0], lambda_k2[0]]).astype(F32)
    o_d = _diff_attn(dq_rot, dq_nopos, dk, vv, lam_vecs, row2(subln_g[0]).astype(F32), n_batch=n_batch,
                     seq=seq, ctx_len=ctx_len, n_latent=n_lat, n_heads=dh, v_col=nk // (2 * HEAD_DIM),
                     lambda_init=lambda_init)
    w_out = w_out_odd[0].astype(BF16)
    x1, h2, idx, wts = _outproj_ln((o_c, None), (o_d, None), (xt, None), w_out[:na], w_out[na:], g1, sc2,
                                   sh2, row2(ln1_g[1]), row2(ln1_b[1]), w_router_t, router_bias_col,
                                   n_lat=n_lat, seq=seq, n_groups=n_groups, alpha=alpha)
    y = moe(h2, idx, wts, 1)
    x2, _ = _resid_ln(x1, y, g2, row2(ln2_g[1]), row2(ln2_b[1]), None, None, n_rows=n_lat, seq=seq,
                      n_groups=n_groups, alpha=alpha)
    return x2.reshape(n_batch, seq, d)
```

```python
import functools
import math

import numpy as np
import jax
import jax.numpy as jnp
from jax import lax
from jax.experimental import pallas as pl
from jax.experimental.pallas import tpu as pltpu

F32 = jnp.float32
BF16 = jnp.bfloat16

HEAD_DIM = 128
GRID_W = 64
ROPE_THETA = 10000.0
SCALE = HEAD_DIM ** -0.5
NEG_INF = -1e30
A_WINDOW = 128
A_BLOCK = 128
NA_ROWS = 8
NA_COLS = 16
B_TILE_ROWS = 4
B_HEADS_PER_STEP = 4
ROW_PARTS = 2
N_EXPERTS = 16
N_GROUPS = 4
EXPERTS_PER_GROUP = N_EXPERTS // N_GROUPS
LN_EPS = 1e-5
RMS_EPS = 1e-6
LOG2E = math.log2(math.e)
Q_SCALE = SCALE * LOG2E

V7X_VMEM_BYTES = 64 * 1024 * 1024
VMEM_LIMIT = V7X_VMEM_BYTES - 8 * 1024 * 1024
LANES = 128

NT_DIMS = (((1,), (1,)), ((), ()))


def _params(*sem):
    return pltpu.CompilerParams(dimension_semantics=sem, vmem_limit_bytes=VMEM_LIMIT)


def _adaln_kernel(c_ref, w_ref, b_ref, o_ref):
    c = c_ref[...]
    a = (c * (1.0 / (1.0 + jnp.exp(-c)))).astype(BF16)
    o_ref[...] = jnp.dot(a, w_ref[...].astype(BF16), preferred_element_type=F32) + b_ref[...]


def _adaln(c_rows, w_ada, b_ada):
    depth, d, n6 = w_ada.shape
    rows = c_rows.shape[0]
    tn = 1024
    return pl.pallas_call(
        _adaln_kernel,
        out_shape=jax.ShapeDtypeStruct((depth, rows, n6), F32),
        grid=(depth, n6 // tn),
        in_specs=[
            pl.BlockSpec((rows, d), lambda l, j: (0, 0)),
            pl.BlockSpec((None, d, tn), lambda l, j: (l, 0, j)),
            pl.BlockSpec((None, 1, tn), lambda l, j: (l, 0, j)),
        ],
        out_specs=pl.BlockSpec((None, rows, tn), lambda l, j: (l, 0, j)),
        compiler_params=_params("parallel", "parallel"),
        name="adaln",
    )(c_rows, w_ada, b_ada.reshape(depth, 1, n6))


def _two_stream_specs(tm, width, n_lat_tiles):
    return [pl.BlockSpec((tm, width), lambda i: (jnp.minimum(i, n_lat_tiles - 1), 0)),
            pl.BlockSpec((tm, width), lambda i: (jnp.maximum(i - n_lat_tiles, 0), 0))]


def _pick_stream(lat_ref, ctx_ref, n_lat_tiles):
    if ctx_ref is None:
        return lat_ref[...]
    return jnp.where(pl.program_id(0) >= n_lat_tiles, ctx_ref[...], lat_ref[...])


def _modulate_kernel(x_ref, c_ref, sc_ref, sh_ref, o_ref, *, n_lat_tiles):
    x = _pick_stream(x_ref, c_ref, n_lat_tiles)
    o_ref[...] = (x * (1.0 + sc_ref[0]) + sh_ref[0]).astype(o_ref.dtype)


def _modulate(x_lat, x_ctx, sc, sh, *, seq, n_groups):
    d = x_lat.shape[1]
    rows = x_lat.shape[0] + x_ctx.shape[0]
    tm = 512
    n_lat_tiles = x_lat.shape[0] // tm
    grp = lambda i: (jnp.minimum(i // (seq // tm), n_groups - 1), 0, 0)
    return pl.pallas_call(
        functools.partial(_modulate_kernel, n_lat_tiles=n_lat_tiles),
        out_shape=jax.ShapeDtypeStruct((rows, d), BF16),
        grid=(rows // tm,),
        in_specs=_two_stream_specs(tm, d, n_lat_tiles)
        + [pl.BlockSpec((1, 1, d), grp), pl.BlockSpec((1, 1, d), grp)],
        out_specs=pl.BlockSpec((tm, d), lambda i: (i, 0)),
        compiler_params=_params("parallel"),
        name="modulate",
    )(x_lat, x_ctx, sc, sh)


def _cast_kernel(x_ref, o_ref):
    o_ref[...] = x_ref[...].astype(o_ref.dtype)


def _cast_bf16(w):
    shape = w.shape
    w2 = w.reshape(-1, shape[-1])
    tm = min(w2.shape[0], (2 * 1024 * 1024) // shape[-1])
    out = pl.pallas_call(
        _cast_kernel,
        out_shape=jax.ShapeDtypeStruct(w2.shape, BF16),
        grid=(w2.shape[0] // tm,),
        in_specs=[pl.BlockSpec((tm, shape[-1]), lambda i: (i, 0))],
        out_specs=pl.BlockSpec((tm, shape[-1]), lambda i: (i, 0)),
        compiler_params=_params("parallel"),
        name="cast_bf16",
    )(w2)
    return out.reshape(shape)


def _proj_kernel(*refs, tn, norm, rope, emit_nopos):
    it = iter(refs)
    x_ref, w_ref, cs_ref = next(it), next(it), next(it)
    g_ref = next(it) if norm else None
    cos_ref, sin_ref = (next(it), next(it)) if rope else (None, None)
    o_ref = next(it)
    n_ref = next(it) if emit_nopos else None
    tm = x_ref.shape[0]
    rows_per_part = tm // ROW_PARTS
    for part in range(ROW_PARTS):
        rows = slice(part * rows_per_part, (part + 1) * rows_per_part)
        acc = jnp.dot(x_ref[rows, :], w_ref[...], preferred_element_type=F32)
        for hd in range(tn // HEAD_DIM):
            sl = slice(hd * HEAD_DIM, (hd + 1) * HEAD_DIM)
            xh = acc[:, sl]
            if norm:
                xh = xh * lax.rsqrt(jnp.mean(xh * xh, axis=-1, keepdims=True) + RMS_EPS) * g_ref[...]
            cs = cs_ref[:, sl]
            if rope:
                rot = xh * cos_ref[rows, :] + pltpu.roll(xh, HEAD_DIM // 2, 1) * sin_ref[rows, :]
                o_ref[rows, sl] = (rot * cs).astype(o_ref.dtype)
                if emit_nopos:
                    n_ref[rows, sl] = (xh * cs).astype(n_ref.dtype)
            else:
                o_ref[rows, sl] = (xh * cs).astype(o_ref.dtype)


def _proj(x, w, col_scale, *, n_rows, seq, n_latent, tn=None, gain=None, rope_tabs=None,
          emit_nopos=False, name="proj"):
    d = x.shape[1]
    nc = w.shape[1]
    tm = min(1024, seq)
    tn = nc if tn is None else tn
    norm, rope = gain is not None, rope_tabs is not None
    in_specs = [pl.BlockSpec((tm, d), lambda i, j: (i, 0)),
                pl.BlockSpec((d, tn), lambda i, j: (0, j)),
                pl.BlockSpec((1, tn), lambda i, j: (0, j))]
    args = [x, w, col_scale]
    if norm:
        in_specs.append(pl.BlockSpec((1, HEAD_DIM), lambda i, j: (0, 0)))
        args.append(gain)
    if rope:
        per_seq = seq // tm
        tab = lambda i, j: (jnp.where(i < n_latent // tm, i % per_seq, per_seq), 0)
        in_specs += [pl.BlockSpec((tm, HEAD_DIM), tab)] * 2
        args += list(rope_tabs)
    n_out = 2 if emit_nopos else 1
    out_shape = [jax.ShapeDtypeStruct((n_rows, nc), BF16)] * n_out
    out_specs = [pl.BlockSpec((tm, tn), lambda i, j: (i, j))] * n_out
    res = pl.pallas_call(
        functools.partial(_proj_kernel, tn=tn, norm=norm, rope=rope, emit_nopos=emit_nopos),
        out_shape=out_shape,
        grid=(n_rows // tm, nc // tn),
        in_specs=in_specs,
        out_specs=out_specs,
        compiler_params=_params("parallel", "parallel"),
        name=name,
    )(*args)
    return res if emit_nopos else res[0]


def _stack_heads(x, g):
    if g == 1:
        return x
    return jnp.concatenate([x[:, i * HEAD_DIM:(i + 1) * HEAD_DIM] for i in range(g)], axis=0)


def _unstack_store(o_ref, o, g, t, col0=0):
    for i in range(g):
        o_ref[:, col0 + i * HEAD_DIM:col0 + (i + 1) * HEAD_DIM] = o[i * t:(i + 1) * t].astype(o_ref.dtype)


def _sink_column(sink_ref, first_head, g, t):
    cols = [jnp.full((t, 1), sink_ref[first_head + i] * LOG2E, F32) for i in range(g)]
    return cols[0] if g == 1 else jnp.concatenate(cols, axis=0)


def _with_ones(v):
    return jnp.concatenate([v, jnp.ones_like(v)], axis=1)


def _row_max(pieces):
    cols = [p[:, c:c + LANES] for p in pieces for c in range(0, p.shape[1], LANES)]
    return jnp.max(functools.reduce(jnp.maximum, cols), axis=-1, keepdims=True)


def _online_update(s, v_aug, m, acc):
    m_new = jnp.maximum(m, jnp.max(s, axis=-1, keepdims=True))
    p = jnp.exp2(s - m_new)
    acc = jnp.exp2(m - m_new) * acc + jnp.dot(p.astype(BF16), v_aug, preferred_element_type=F32)
    return m_new, acc


def _band_kernel(*refs, nh, g, bq, has_sink, same_qc, shared_bias):
    it = iter(refs)
    sink_ref = next(it) if has_sink else None
    q_ref = next(it)
    qc_ref = q_ref if same_qc else next(it)
    k_refs = [next(it) for _ in range(3)]
    v_refs = [next(it) for _ in range(3)]
    kx_ref, vx_ref, bias_ref, o_ref = next(it), next(it), next(it), next(it)
    hw = g * HEAD_DIM
    for h in range(nh):
        qs = slice(h * hw, (h + 1) * hw)
        ks = slice(h * HEAD_DIM, (h + 1) * HEAD_DIM)
        q = _stack_heads(q_ref[:, qs], g)
        qc = q if same_qc else _stack_heads(qc_ref[:, qs], g)
        bias = bias_ref.at[0 if shared_bias else h]
        s = [lax.dot_general(q, k_refs[j][:, ks], NT_DIMS, preferred_element_type=F32)
             + bias[:, j * bq:(j + 1) * bq] for j in range(3)]
        s.append(lax.dot_general(qc, kx_ref[:, ks], NT_DIMS, preferred_element_type=F32))
        vs = [r[:, ks] for r in v_refs] + [vx_ref[:, ks]]
        m = _row_max(s)
        if has_sink:
            sink = _sink_column(sink_ref, (pl.program_id(0) * nh + h) * g, g, bq)
            m = jnp.maximum(m, sink)
        o = functools.reduce(jnp.add, [
            jnp.dot(jnp.exp2(sj - m).astype(BF16), _with_ones(vj), preferred_element_type=F32)
            for sj, vj in zip(s, vs)])
        den = o[:, HEAD_DIM:]
        if has_sink:
            den = den + jnp.exp2(sink - m)
        _unstack_store(o_ref, o[:, :HEAD_DIM] / den, g, bq, col0=h * hw)


def _band_attn(q, qc, k, v, bias, sink, *, n_batch, seq, ctx_len, n_latent, n_kv, nh, g, bq,
               q_col, k_col, v_col, name):
    nblk = seq // bq
    ctx_blk = n_latent // ctx_len
    shared_bias = bias.shape[0] == 1
    has_sink, same_qc = sink is not None, qc is None
    qw, kw = nh * g * HEAD_DIM, nh * HEAD_DIM

    def q_map(hg, i, b):
        return (b * nblk + i, q_col + hg)

    def kv_map(col, off):
        return lambda hg, i, b: (b * nblk + jnp.clip(i + off, 0, nblk - 1), col + hg)

    def ctx_map(col):
        return lambda hg, i, b: (ctx_blk + b, col + hg)

    def bias_map(hg, i, b):
        case = jnp.where(i == 0, 0, jnp.where(i == nblk - 1, 2, 1))
        return (0 if shared_bias else hg, case, 0, 0)

    in_specs, args = [], []
    if has_sink:
        in_specs.append(pl.BlockSpec(memory_space=pltpu.SMEM))
        args.append(sink)
    in_specs.append(pl.BlockSpec((bq, qw), q_map))
    args.append(q)
    if not same_qc:
        in_specs.append(pl.BlockSpec((bq, qw), q_map))
        args.append(qc)
    in_specs += [pl.BlockSpec((bq, kw), kv_map(k_col, off)) for off in (-1, 0, 1)]
    args += [k] * 3
    in_specs += [pl.BlockSpec((bq, kw), kv_map(v_col, off)) for off in (-1, 0, 1)]
    args += [v] * 3
    in_specs += [pl.BlockSpec((ctx_len, kw), ctx_map(k_col)),
                 pl.BlockSpec((ctx_len, kw), ctx_map(v_col)),
                 pl.BlockSpec((1 if shared_bias else nh, None, g * bq, 3 * bq), bias_map)]
    args += [k, v, bias]
    return pl.pallas_call(
        functools.partial(_band_kernel, nh=nh, g=g, bq=bq, has_sink=has_sink, same_qc=same_qc,
                          shared_bias=shared_bias),
        out_shape=jax.ShapeDtypeStruct((n_latent, n_kv * g * HEAD_DIM), BF16),
        grid=(n_kv // nh, nblk, n_batch),
        in_specs=in_specs,
        out_specs=pl.BlockSpec((bq, qw), lambda hg, i, b: (b * nblk + i, hg)),
        compiler_params=_params("parallel", "parallel", "parallel"),
        name=name,
    )(*args)


def _dense_kernel(*refs, g, tq, tk, n_k, has_ctx, has_sink):
    it = iter(refs)
    sink_ref = next(it) if has_sink else None
    q_ref = next(it)
    qc_ref = next(it) if has_ctx else None
    k_ref, v_ref = next(it), next(it)
    kx_ref, vx_ref = (next(it), next(it)) if has_ctx else (None, None)
    o_ref = next(it)
    q = _stack_heads(q_ref[...], g)
    rows = g * tq
    m = jnp.full((rows, 1), NEG_INF, F32)
    acc = jnp.zeros((rows, 2 * HEAD_DIM), F32)
    for c in range(n_k // tk):
        s = lax.dot_general(q, k_ref[c * tk:(c + 1) * tk, :], NT_DIMS, preferred_element_type=F32)
        m, acc = _online_update(s, _with_ones(v_ref[c * tk:(c + 1) * tk, :]), m, acc)
    if has_ctx:
        qc = _stack_heads(qc_ref[...], g)
        s = lax.dot_general(qc, kx_ref[...], NT_DIMS, preferred_element_type=F32)
        m, acc = _online_update(s, _with_ones(vx_ref[...]), m, acc)
    num, den = acc[:, :HEAD_DIM], acc[:, HEAD_DIM:]
    if has_sink:
        sink = _sink_column(sink_ref, pl.program_id(1) * g, g, tq)
        m_new = jnp.maximum(m, sink)
        alpha = jnp.exp2(m - m_new)
        num = alpha * num
        den = alpha * den + jnp.exp2(sink - m_new)
    _unstack_store(o_ref, num / den, g, tq)


def _dense_attn(q, qc, k, v, sink, *, n_batch, n_q, q_row0, n_k, k_row0, ctx_len, ctx_row0,
                n_kv, g, q_col, k_col, v_col, name):
    tq = min(256, n_q)
    tk = min(512, n_k)
    nq_blk = n_q // tq
    has_ctx, has_sink = qc is not None, sink is not None

    def q_map(b, h, i):
        return (q_row0 // tq + b * nq_blk + i, q_col + h)

    in_specs, args = [], []
    if has_sink:
        in_specs.append(pl.BlockSpec(memory_space=pltpu.SMEM))
        args.append(sink)
    in_specs.append(pl.BlockSpec((tq, g * HEAD_DIM), q_map))
    args.append(q)
    if has_ctx:
        in_specs.append(pl.BlockSpec((tq, g * HEAD_DIM), q_map))
        args.append(qc)
    in_specs += [pl.BlockSpec((n_k, HEAD_DIM), lambda b, h, i: (k_row0 // n_k + b, k_col + h)),
                 pl.BlockSpec((n_k, HEAD_DIM), lambda b, h, i: (k_row0 // n_k + b, v_col + h))]
    args += [k, v]
    if has_ctx:
        in_specs += [pl.BlockSpec((ctx_len, HEAD_DIM), lambda b, h, i: (ctx_row0 // ctx_len + b, k_col + h)),
                     pl.BlockSpec((ctx_len, HEAD_DIM), lambda b, h, i: (ctx_row0 // ctx_len + b, v_col + h))]
        args += [k, v]
    return pl.pallas_call(
        functools.partial(_dense_kernel, g=g, tq=tq, tk=tk, n_k=n_k, has_ctx=has_ctx,
                          has_sink=has_sink),
        out_shape=jax.ShapeDtypeStruct((n_batch * n_q, n_kv * g * HEAD_DIM), BF16),
        grid=(n_batch, n_kv, nq_blk),
        in_specs=in_specs,
        out_specs=pl.BlockSpec((tq, g * HEAD_DIM), lambda b, h, i: (b * nq_blk + i, h)),
        compiler_params=_params("parallel", "parallel", "parallel"),
        name=name,
    )(*args)


def _diff_update(s, v, m, l, acc):
    m_new = jnp.maximum(m, jnp.max(s, axis=-1, keepdims=True))
    alpha = jnp.exp2(m - m_new)
    p = jnp.exp2(s - m_new)
    l = alpha * l + jnp.sum(p, axis=-1, keepdims=True)
    acc = alpha * acc + jnp.dot(p.astype(BF16), v, preferred_element_type=F32)
    return m_new, l, acc


def _diff_kernel(lam_ref, g_ref, q_ref, qc_ref, k_ref, v_ref, kx_ref, vx_ref, o_ref, *,
                 tq, tk, n_k, lambda_init):
    dv = 2 * HEAD_DIM
    lam = (jnp.exp(jnp.sum(lam_ref[0:1, :] * lam_ref[1:2, :], axis=1, keepdims=True))
           - jnp.exp(jnp.sum(lam_ref[2:3, :] * lam_ref[3:4, :], axis=1, keepdims=True)) + lambda_init)
    outs = []
    for t in range(2):
        sl = slice(t * HEAD_DIM, (t + 1) * HEAD_DIM)
        q = q_ref[:, sl]
        m = jnp.full((tq, 1), NEG_INF, F32)
        l = jnp.zeros((tq, 1), F32)
        acc = jnp.zeros((tq, dv), F32)
        for c in range(n_k // tk):
            rows = slice(c * tk, (c + 1) * tk)
            s = lax.dot_general(q, k_ref[rows, sl], NT_DIMS, preferred_element_type=F32)
            m, l, acc = _diff_update(s, v_ref[rows, :], m, l, acc)
        s = lax.dot_general(qc_ref[:, sl], kx_ref[:, sl], NT_DIMS, preferred_element_type=F32)
        m, l, acc = _diff_update(s, vx_ref[...], m, l, acc)
        outs.append(acc / l)
    o = outs[0] - lam * outs[1]
    o = o * lax.rsqrt(jnp.mean(o * o, axis=-1, keepdims=True) + RMS_EPS) * g_ref[...]
    o_ref[...] = (o * (1.0 - lambda_init)).astype(o_ref.dtype)


def _diff_attn(q, qc, k, v, lam_vecs, subln_g, *, n_batch, seq, ctx_len, n_latent, n_heads,
               v_col, lambda_init):
    tq = min(512, seq)
    tk = min(512, seq)
    dv = 2 * HEAD_DIM
    nq_blk = seq // tq
    ctx_blk = n_latent // ctx_len
    q_map = lambda b, h, i: (b * nq_blk + i, h)
    return pl.pallas_call(
        functools.partial(_diff_kernel, tq=tq, tk=tk, n_k=seq, lambda_init=lambda_init),
        out_shape=jax.ShapeDtypeStruct((n_latent, n_heads * dv), BF16),
        grid=(n_batch, n_heads, nq_blk),
        in_specs=[
            pl.BlockSpec((4, HEAD_DIM), lambda b, h, i: (0, 0)),
            pl.BlockSpec((1, dv), lambda b, h, i: (0, 0)),
            pl.BlockSpec((tq, dv), q_map),
            pl.BlockSpec((tq, dv), q_map),
            pl.BlockSpec((seq, dv), lambda b, h, i: (b, h)),
            pl.BlockSpec((seq, dv), lambda b, h, i: (b, v_col + h)),
            pl.BlockSpec((ctx_len, dv), lambda b, h, i: (ctx_blk + b, h)),
            pl.BlockSpec((ctx_len, dv), lambda b, h, i: (ctx_blk + b, v_col + h)),
        ],
        out_specs=pl.BlockSpec((tq, dv), q_map),
        compiler_params=_params("parallel", "parallel", "parallel"),
        name="diff_attn",
    )(lam_vecs, subln_g, q, qc, k, v, k, v)


def _layer_norm(z, g, b):
    mu = jnp.mean(z, axis=-1, keepdims=True)
    zc = z - mu
    var = jnp.mean(zc * zc, axis=-1, keepdims=True)
    return zc * lax.rsqrt(var + LN_EPS) * g + b


def _split_bf16(x):
    hi = x.astype(BF16)
    return hi, (x - hi.astype(F32)).astype(BF16)


def _first_argmax(vals, idx, width):
    m = jnp.max(vals, axis=0, keepdims=True)
    first = jnp.min(jnp.where(vals == m, idx, float(width)), axis=0, keepdims=True)
    return m, first


def _route(logits, bias):
    e, t = logits.shape
    scores = 1.0 / (1.0 + jnp.exp(-logits))
    biased = scores + bias
    row_i = lax.broadcasted_iota(jnp.int32, (e, t), 0)
    grp = lax.shift_right_logical(row_i, int(math.log2(EXPERTS_PER_GROUP)))
    row = row_i.astype(F32)
    neg = -jnp.inf
    best_score, best = None, None
    for gi in range(N_GROUPS):
        vg = jnp.where(grp == gi, biased, neg)
        m1, i1 = _first_argmax(vg, row, e)
        m2 = jnp.max(jnp.where(row == i1, neg, vg), axis=0, keepdims=True)
        gs = m1 + m2
        if gi == 0:
            best_score, best = gs, jnp.zeros((1, t), jnp.int32)
        else:
            upd = gs > best_score
            best = jnp.where(upd, gi, best)
            best_score = jnp.where(upd, gs, best_score)
    masked = jnp.where(grp == best, biased, neg)
    _, e1 = _first_argmax(masked, row, e)
    _, e2 = _first_argmax(jnp.where(row == e1, neg, masked), row, e)
    w1 = jnp.sum(jnp.where(row == e1, scores, 0.0), axis=0, keepdims=True)
    w2 = jnp.sum(jnp.where(row == e2, scores, 0.0), axis=0, keepdims=True)
    den = w1 + w2
    two = lax.broadcasted_iota(jnp.int32, (2, t), 0)
    return jnp.where(two == 0, e1, e2).astype(jnp.int32), jnp.where(two == 0, w1 / den, w2 / den)


def _outproj_kernel(*refs, alpha, two_streams, n_lat_tiles):
    it = iter(refs)
    xa_ref, xa_ctx = next(it), (next(it) if two_streams else None)
    xb_ref, xb_ctx = next(it), (next(it) if two_streams else None)
    xres_ref, xres_ctx = next(it), (next(it) if two_streams else None)
    (wa_ref, wb_ref, g1_ref, sc2_ref, sh2_ref, lng_ref, lnb_ref, wrt_ref, rb_ref,
     x1_ref, h2_ref, idx_ref, wts_ref) = it
    w_hi, w_lo = _split_bf16(wrt_ref[...])
    w_cat = jnp.concatenate([w_hi, w_lo], axis=0)
    tm = x1_ref.shape[0]
    rows_per_part = tm // ROW_PARTS
    for part in range(ROW_PARTS):
        rows = slice(part * rows_per_part, (part + 1) * rows_per_part)
        pick = lambda lat, ctx: _pick_stream(lat.at[rows, :], None if ctx is None else ctx.at[rows, :], n_lat_tiles)
        y = (jnp.dot(pick(xa_ref, xa_ctx), wa_ref[...], preferred_element_type=F32)
             + jnp.dot(pick(xb_ref, xb_ctx), wb_ref[...], preferred_element_type=F32))
        x1 = _layer_norm(alpha * pick(xres_ref, xres_ctx) + g1_ref[0] * y, lng_ref[...], lnb_ref[...])
        x1_ref[rows, :] = x1
        h2 = x1 * (1.0 + sc2_ref[0]) + sh2_ref[0]
        h2_ref[rows, :] = h2
        h_hi, h_lo = _split_bf16(h2)
        both = lax.dot_general(w_cat, h_hi, NT_DIMS, preferred_element_type=F32)
        logits = (both[:N_EXPERTS] + both[N_EXPERTS:]
                  + lax.dot_general(w_hi, h_lo, NT_DIMS, preferred_element_type=F32))
        idx, wts = _route(logits, rb_ref[...])
        idx_ref[:, rows] = idx
        wts_ref[:, rows] = wts


def _outproj_ln(xa, xb, xres, wa, wb, g1, sc2, sh2, ln_g, ln_b, w_router_t, router_bias, *,
                n_lat, seq, n_groups, alpha):
    two_streams = xa[1] is not None
    d = xres[0].shape[1]
    ka, kb = xa[0].shape[1], xb[0].shape[1]
    tm = 256 * ROW_PARTS
    n_lat_tiles = n_lat // tm
    n_rows = n_lat + (xres[1].shape[0] if two_streams else 0)
    row = lambda i: (i, 0)
    col = lambda i: (0, i)
    const = lambda i: (0, 0)
    grp = lambda i: (jnp.minimum(i // (seq // tm), n_groups - 1), 0, 0)
    in_specs, args = [], []
    for pair, width in ((xa, ka), (xb, kb), (xres, d)):
        if two_streams:
            in_specs += _two_stream_specs(tm, width, n_lat_tiles)
            args += list(pair)
        else:
            in_specs.append(pl.BlockSpec((tm, width), row))
            args.append(pair[0])
    in_specs += [pl.BlockSpec((ka, d), const), pl.BlockSpec((kb, d), const),
                 pl.BlockSpec((1, 1, d), grp), pl.BlockSpec((1, 1, d), grp), pl.BlockSpec((1, 1, d), grp),
                 pl.BlockSpec((1, d), const), pl.BlockSpec((1, d), const),
                 pl.BlockSpec((N_EXPERTS, d), const), pl.BlockSpec((N_EXPERTS, 1), const)]
    args += [wa, wb, g1, sc2, sh2, ln_g, ln_b, w_router_t, router_bias]
    return pl.pallas_call(
        functools.partial(_outproj_kernel, alpha=alpha, two_streams=two_streams, n_lat_tiles=n_lat_tiles),
        out_shape=[jax.ShapeDtypeStruct((n_rows, d), F32), jax.ShapeDtypeStruct((n_rows, d), F32),
                   jax.ShapeDtypeStruct((2, n_rows), jnp.int32), jax.ShapeDtypeStruct((2, n_rows), F32)],
        grid=(n_rows // tm,),
        in_specs=in_specs,
        out_specs=[pl.BlockSpec((tm, d), row), pl.BlockSpec((tm, d), row),
                   pl.BlockSpec((2, tm), col), pl.BlockSpec((2, tm), col)],
        compiler_params=_params("parallel"),
        name="outproj_ln",
    )(*args)


def _moe_kernel(te_ref, nv_ref, src_ref, dst_ref, h_hbm, gw_ref, wg_ref, wu_ref, wd_ref, y_hbm,
                xbuf, obuf, gsem, ssem, *, tm, ff_chunks):
    i = pl.program_id(0)
    n = pl.num_programs(0)
    nv = nv_ref[0]
    slot = jnp.bitwise_and(i, 1)

    def start_gather(tile, s):
        base = tile * (tm // 2)
        for r in range(tm):
            word = src_ref[base + r // 2]
            tok = jnp.bitwise_and(word, 0xFFFF) if r % 2 == 0 else lax.shift_right_logical(word, 16)
            pltpu.make_async_copy(h_hbm.at[pl.ds(tok, 1)], xbuf.at[s, pl.ds(r, 1)], gsem.at[s]).start()

    def wait_gather(s):
        pltpu.make_async_copy(h_hbm.at[pl.ds(0, tm)], xbuf.at[s], gsem.at[s]).wait()

    def start_scatter(tile, s):
        base = tile * tm
        for r in range(tm):
            pltpu.make_async_copy(obuf.at[s, pl.ds(r, 1)], y_hbm.at[pl.ds(dst_ref[base + r], 1)],
                                  ssem.at[s]).start()

    def wait_scatter(s):
        pltpu.make_async_copy(obuf.at[s], y_hbm.at[pl.ds(0, tm)], ssem.at[s]).wait()

    @pl.when(i == 0)
    def _():
        start_gather(0, 0)
        obuf[...] = jnp.zeros_like(obuf)
        pad0 = y_hbm.shape[0] - 2 * tm
        fills = [pltpu.make_async_copy(obuf.at[s], y_hbm.at[pl.ds(pad0 + s * tm, tm)], ssem.at[s])
                 for s in range(2)]
        for f in fills:
            f.start()
        for f in fills:
            f.wait()

    @pl.when(i < nv)
    def _():
        wait_gather(slot)

        @pl.when(i >= 2)
        def _():
            wait_scatter(slot)

        start_gather(jnp.minimum(i + 1, n - 1), 1 - slot)
        x = xbuf[slot].astype(BF16)
        fc = wg_ref.shape[1] // ff_chunks
        y = None
        for c in range(ff_chunks):
            cs = slice(c * fc, (c + 1) * fc)
            gate = jnp.dot(x, wg_ref[:, cs], preferred_element_type=F32)
            up = jnp.dot(x, wu_ref[:, cs], preferred_element_type=F32)
            act = (gate * (1.0 / (1.0 + jnp.exp(-gate))) * up).astype(BF16)
            part = jnp.dot(act, wd_ref[cs, :], preferred_element_type=F32)
            y = part if y is None else y + part
        obuf[slot] = y * gw_ref[...]
        start_scatter(i, slot)

    @pl.when(i == n - 1)
    def _():
        wait_gather(jnp.bitwise_and(nv, 1))

        @pl.when(nv >= 1)
        def _():
            wait_scatter(jnp.bitwise_and(nv - 1, 1))

        @pl.when(nv >= 2)
        def _():
            wait_scatter(jnp.bitwise_and(nv, 1))


def _moe_ffn(h2, src_packed, dst_row, gw, tile_expert, n_valid, w_gate, w_up, w_down, *, layer, tm):
    t, d = h2.shape
    ff = w_gate.shape[3]
    n_tiles = dst_row.shape[0] // tm
    return pl.pallas_call(
        functools.partial(_moe_kernel, tm=tm, ff_chunks=2),
        out_shape=jax.ShapeDtypeStruct((2 * t + 2 * tm, d), F32),
        grid_spec=pltpu.PrefetchScalarGridSpec(
            num_scalar_prefetch=4,
            grid=(n_tiles,),
            in_specs=[pl.BlockSpec(memory_space=pl.ANY),
                      pl.BlockSpec((tm, 1), lambda i, te, nv, src, dst: (i, 0)),
                      pl.BlockSpec((None, None, d, ff), lambda i, te, nv, src, dst: (layer, te[i], 0, 0)),
                      pl.BlockSpec((None, None, d, ff), lambda i, te, nv, src, dst: (layer, te[i], 0, 0)),
                      pl.BlockSpec((None, None, ff, d), lambda i, te, nv, src, dst: (layer, te[i], 0, 0))],
            out_specs=pl.BlockSpec(memory_space=pl.ANY),
            scratch_shapes=[pltpu.VMEM((2, tm, d), F32), pltpu.VMEM((2, tm, d), F32),
                            pltpu.SemaphoreType.DMA((2,)), pltpu.SemaphoreType.DMA((2,))],
        ),
        compiler_params=_params("arbitrary"),
        name="moe_ffn",
    )(tile_expert, n_valid, src_packed, dst_row, h2, gw, w_gate, w_up, w_down)


def _take(x, idx):
    return x.at[idx].get(mode="promise_in_bounds")


def _dispatch(idx, wts, *, tm):
    t = idx.shape[1]
    n_pairs = 2 * t
    n_tiles = n_pairs // tm + N_EXPERTS
    e_flat = idx.reshape(n_pairs)
    w_flat = wts.reshape(n_pairs)
    _, order, w_sorted = lax.sort((e_flat, jnp.arange(n_pairs, dtype=jnp.int32), w_flat), num_keys=1)
    experts = jnp.arange(N_EXPERTS, dtype=jnp.int32)
    counts = jnp.sum(e_flat[None, :] == experts[:, None], axis=1).astype(jnp.int32)
    start = jnp.cumsum(counts) - counts
    padded = ((counts + tm - 1) // tm) * tm
    pend = jnp.cumsum(padded)
    pstart = pend - padded
    tile_pos = jnp.arange(n_tiles, dtype=jnp.int32) * tm
    tile_expert = jnp.minimum(jnp.sum(tile_pos[:, None] >= pend[None, :], axis=1), N_EXPERTS - 1).astype(jnp.int32)
    local = (tile_pos - pstart[tile_expert])[:, None] + jnp.arange(tm, dtype=jnp.int32)[None, :]
    valid = local < counts[tile_expert][:, None]
    sorted_pos = jnp.clip(start[tile_expert][:, None] + local, 0, n_pairs - 1)
    src_pair = _take(order, sorted_pos)
    src_tok = jnp.where(valid, jnp.where(src_pair >= t, src_pair - t, src_pair), 0)
    pad_row = n_pairs + (tile_pos[:, None] + jnp.arange(tm, dtype=jnp.int32)[None, :]) % (2 * tm)
    dst_row = jnp.where(valid, src_pair, pad_row)
    gw = jnp.where(valid, _take(w_sorted, sorted_pos), 0.0).astype(F32)
    src_packed = jnp.bitwise_or(src_tok[:, 0::2], jnp.left_shift(src_tok[:, 1::2], 16))
    n_valid = (pend[-1] // tm).astype(jnp.int32).reshape(1)
    return src_packed.reshape(-1), dst_row.reshape(-1), gw.reshape(-1, 1), tile_expert, n_valid


def _resid_ln_kernel(*refs, alpha, emit_h):
    it = iter(refs)
    x_ref, f0_ref, f1_ref, g_ref, lng_ref, lnb_ref = (next(it) for _ in range(6))
    sc_ref, sh_ref = (next(it), next(it)) if emit_h else (None, None)
    o_ref = next(it)
    f = f0_ref[...] + f1_ref[...]
    x2 = _layer_norm(alpha * x_ref[...] + g_ref[0] * f, lng_ref[...], lnb_ref[...])
    o_ref[...] = x2
    if emit_h:
        h_ref = next(it)
        h_ref[...] = (x2 * (1.0 + sc_ref[0]) + sh_ref[0]).astype(h_ref.dtype)


def _resid_ln(x, y, gate, ln_g, ln_b, next_sc, next_sh, *, n_rows, seq, n_groups, alpha):
    d = x.shape[1]
    tm = 512
    emit_h = next_sc is not None
    row = lambda i: (i, 0)
    const = lambda i: (0, 0)
    grp = lambda i: (jnp.minimum(i // (seq // tm), n_groups - 1), 0, 0)
    in_specs = [pl.BlockSpec((tm, d), row), pl.BlockSpec((tm, d), row),
                pl.BlockSpec((tm, d), lambda i: (n_rows // tm + i, 0)), pl.BlockSpec((1, 1, d), grp),
                pl.BlockSpec((1, d), const), pl.BlockSpec((1, d), const)]
    args = [x, y, y, gate, ln_g, ln_b]
    out_shape = [jax.ShapeDtypeStruct((n_rows, d), F32)]
    out_specs = [pl.BlockSpec((tm, d), row)]
    if emit_h:
        in_specs += [pl.BlockSpec((1, 1, d), grp)] * 2
        args += [next_sc, next_sh]
        out_shape.append(jax.ShapeDtypeStruct((n_rows, d), BF16))
        out_specs.append(pl.BlockSpec((tm, d), row))
    res = pl.pallas_call(
        functools.partial(_resid_ln_kernel, alpha=alpha, emit_h=emit_h),
        out_shape=out_shape,
        grid=(n_rows // tm,),
        in_specs=in_specs,
        out_specs=out_specs,
        compiler_params=_params("parallel"),
        name="resid_ln",
    )(*args)
    return res if emit_h else (res[0], None)


def _rope_tables(seq, tm):
    t = jnp.arange(seq, dtype=jnp.int32)
    n_freq = HEAD_DIM // 4
    inv = ROPE_THETA ** (-jnp.arange(n_freq, dtype=F32) / n_freq)
    row = (t // GRID_W).astype(F32)
    col = (t % GRID_W).astype(F32)
    ang = jnp.concatenate([row[:, None] * inv[None], col[:, None] * inv[None]], -1)
    cos, sin = jnp.cos(ang), jnp.sin(ang)
    cos2 = jnp.concatenate([cos, cos], -1)
    sin2 = jnp.concatenate([-sin, sin], -1)
    cos2 = jnp.concatenate([cos2, jnp.ones((tm, HEAD_DIM), F32)], 0)
    sin2 = jnp.concatenate([sin2, jnp.zeros((tm, HEAD_DIM), F32)], 0)
    return cos2, sin2


def _window_bias(seq, g):
    bq = A_BLOCK
    nblk = seq // bq
    rel = np.arange(3 * bq)[None, :] - np.arange(bq)[:, None]
    band = (rel >= bq - A_WINDOW) & (rel <= bq + A_WINDOW)
    tabs = []
    for blk in (0, min(1, nblk - 1), nblk - 1):
        kpos = blk * bq - bq + np.arange(3 * bq)
        ok = band & ((kpos >= 0) & (kpos < seq))[None, :]
        tabs.append(np.tile(np.where(ok, 0.0, NEG_INF).astype(np.float32), (g, 1)))
    return jnp.asarray(np.stack(tabs)[None])


def _neighbourhood_bias(rpb, seq):
    n_heads = rpb.shape[0]
    rows = seq // GRID_W
    kh, kw = min(NA_ROWS, rows), NA_COLS
    tr = B_TILE_ROWS
    nblk = rows // tr
    col = np.arange(GRID_W)
    col_start = np.clip(col - kw // 2, 0, GRID_W - kw)
    col_ok = (col[None, :] >= col_start[:, None]) & (col[None, :] < col_start[:, None] + kw)
    assert np.all(np.abs(col[None, :] - col[:, None])[col_ok] <= NA_COLS - 1)
    masks = []
    for blk in (0, min(1, nblk - 1), nblk - 1):
        r = blk * tr + np.arange(tr)
        r0 = np.clip(r - kh // 2, 0, rows - kh)
        krow = (blk - 1) * tr + np.arange(3 * tr)
        row_ok = (krow[None, :] >= r0[:, None]) & (krow[None, :] < r0[:, None] + kh)
        ok = row_ok[:, None, :, None] & col_ok[None, :, None, :]
        masks.append(ok.reshape(tr * GRID_W, 3 * tr * GRID_W))
    ok = jnp.asarray(np.stack(masks))
    pad = GRID_W - NA_COLS
    rp = jnp.pad(rpb.astype(F32) * LOG2E, ((0, 0), (0, 0), (pad, pad)))
    cexp = jnp.stack([rp[:, :, GRID_W - 1 - cq:2 * GRID_W - 1 - cq] for cq in range(GRID_W)], axis=2)
    off = NA_ROWS - 1 - tr
    assert off - (tr - 1) >= 0 and off + 3 * tr - 1 <= 2 * NA_ROWS - 2
    t5 = jnp.stack([cexp[:, off - rq:off - rq + 3 * tr] for rq in range(tr)], axis=1)
    tab = t5.transpose(0, 1, 3, 2, 4).reshape(n_heads, tr * GRID_W, 3 * tr * GRID_W)
    return jnp.where(ok[None], tab[:, None], NEG_INF)


def _deinterleave_cols(n_heads):
    one = np.concatenate([np.arange(0, HEAD_DIM, 2), np.arange(1, HEAD_DIM, 2)])
    return np.concatenate([h * HEAD_DIM + one for h in range(n_heads)])


def kernel(x, c, ctx, c_ctx, w_ada, b_ada, ln1_g, ln1_b, ln2_g, ln2_b, w_in_even, w_out_even, sink_logits, na_rpb, w_in_odd, w_out_odd, q_norm_g, k_norm_g, lambda_q1, lambda_k1, lambda_q2, lambda_k2, subln_g, w_router, router_bias, w_exp_gate, w_exp_up, w_exp_down):
    n_batch, seq, d = x.shape
    ctx_len = ctx.shape[1]
    depth = w_ada.shape[0]
    assert depth == 2, "one even (A||B) layer followed by one odd (C||D) layer"
    n_lat = n_batch * seq
    n_ctx = n_batch * ctx_len
    n_all = n_lat + n_ctx
    n_groups = n_batch + 1
    alpha = float((2 * depth) ** 0.25)
    heads = d // HEAD_DIM
    qh = heads // 2
    kvh = qh // 4
    gq = qh // kvh
    bh = heads - qh
    dh = (heads - qh) // 2
    moe_tm = 512
    proj_tm = min(1024, seq)
    assert seq % GRID_W == 0 and (seq // GRID_W) % B_TILE_ROWS == 0 and seq // GRID_W >= NA_ROWS
    assert n_ctx % proj_tm == 0 and n_lat % ctx_len == 0 and (2 * n_all) % moe_tm == 0
    assert ctx_len % LANES == 0 and bh % B_HEADS_PER_STEP == 0
    assert n_all <= 1 << 16, "token ids are packed as 16-bit halves for the MoE row gather"

    pad_rows = -(-n_groups // 16) * 16
    c_rows = jnp.concatenate([c, c_ctx[None], jnp.zeros((pad_rows - n_groups, d), F32)], 0)
    mod = _adaln(c_rows, w_ada, b_ada)[:, :n_groups]
    mod = mod.reshape(depth, n_groups, 6, 1, d)
    mods = [[mod[l, :, j] for j in range(6)] for l in range(depth)]

    cos2, sin2 = _rope_tables(seq, proj_tm)
    tabs = (cos2, sin2)
    x_lat, x_ctx = x.reshape(n_lat, d), ctx.reshape(n_ctx, d)
    row2 = lambda v: v.reshape(1, -1)
    ones = lambda n: jnp.ones((1, n), F32)
    qscale = lambda n: jnp.full((1, n), Q_SCALE, F32)
    w_router_t = w_router.T.astype(F32)
    router_bias_col = router_bias.reshape(-1, 1).astype(F32)
    wg_all, wu_all, wd_all = (_cast_bf16(w) for w in (w_exp_gate, w_exp_up, w_exp_down))

    def moe(h2, idx, wts, layer):
        src_packed, dst_row, gw, tile_expert, n_valid = _dispatch(idx, wts, tm=moe_tm)
        return _moe_ffn(h2, src_packed, dst_row, gw, tile_expert, n_valid, wg_all, wu_all, wd_all,
                        layer=layer, tm=moe_tm)

    sh1, sc1, g1, sh2, sc2, g2 = mods[0]
    h = _modulate(x_lat, x_ctx, sc1, sh1, seq=seq, n_groups=n_groups)
    w_in = w_in_even[0]
    na, nk, nb = qh * HEAD_DIM, kvh * HEAD_DIM, bh * HEAD_DIM
    perm_q, perm_k = _deinterleave_cols(qh), _deinterleave_cols(kvh)
    w_aq = w_in[:, :na][:, perm_q].astype(BF16)
    w_ak = w_in[:, na:na + nk][:, perm_k].astype(BF16)
    w_rest = jnp.concatenate([w_in[:, na + 2 * nk:], w_in[:, na + nk:na + 2 * nk]], 1).astype(BF16)
    rest_scale = jnp.concatenate([qscale(nb), ones(2 * nb + nk)], 1)
    kw = dict(seq=seq, n_latent=n_lat)
    aq_rot, aq_nopos = _proj(h, w_aq, qscale(na), n_rows=n_all, rope_tabs=tabs, emit_nopos=True,
                             name="proj_aq", **kw)
    ak = _proj(h, w_ak, ones(nk), n_rows=n_all, rope_tabs=tabs, name="proj_ak", **kw)
    rest = _proj(h, w_rest, rest_scale, n_rows=n_all, tn=w_rest.shape[1] // 2, name="proj_even_rest", **kw)
    sink = sink_logits[0].astype(F32)
    akw = dict(n_batch=n_batch, seq=seq, ctx_len=ctx_len, n_latent=n_lat)
    o_a = _band_attn(aq_rot, aq_nopos, ak, rest, _window_bias(seq, gq), sink, n_kv=kvh, nh=kvh, g=gq,
                     bq=A_BLOCK, q_col=0, k_col=0, v_col=3 * nb // nk, name="attn_a", **akw)
    bw = B_HEADS_PER_STEP * HEAD_DIM
    o_b = _band_attn(rest, None, rest, rest, _neighbourhood_bias(na_rpb[0], seq), None, n_kv=bh,
                     nh=B_HEADS_PER_STEP, g=1, bq=B_TILE_ROWS * GRID_W, q_col=0, k_col=nb // bw,
                     v_col=2 * nb // bw, name="attn_b", **akw)
    ckw = dict(n_batch=n_batch, n_q=ctx_len, q_row0=n_lat, n_k=ctx_len, k_row0=n_lat, ctx_len=ctx_len,
               ctx_row0=n_lat)
    oc_a = _dense_attn(aq_rot, None, ak, rest, sink, n_kv=kvh, g=gq, q_col=0, k_col=0,
                       v_col=3 * nb // HEAD_DIM, name="ctx_attn_a", **ckw)
    oc_b = _dense_attn(rest, None, rest, rest, None, n_kv=bh, g=1, q_col=0, k_col=bh, v_col=2 * bh,
                       name="ctx_attn_b", **ckw)
    w_out = w_out_even[0].astype(BF16)
    x1, h2, idx, wts = _outproj_ln((o_a, oc_a), (o_b, oc_b), (x_lat, x_ctx), w_out[:na], w_out[na:], g1, sc2,
                                   sh2, row2(ln1_g[0]), row2(ln1_b[0]), w_router_t, router_bias_col,
                                   n_lat=n_lat, seq=seq, n_groups=n_groups, alpha=alpha)
    y = moe(h2, idx, wts, 0)
    nsh1, nsc1 = mods[1][0], mods[1][1]
    xt, h = _resid_ln(x1, y, g2, row2(ln2_g[0]), row2(ln2_b[0]), nsc1, nsh1, n_rows=n_all, seq=seq,
                      n_groups=n_groups, alpha=alpha)

    sh1, sc1, g1, sh2, sc2, g2 = mods[1]
    lambda_init = 0.8 - 0.6 * math.exp(-0.3 * 1)
    w_in = w_in_odd[0]
    nd = dh * 2 * HEAD_DIM
    o0 = na + 2 * nk
    perm_d = _deinterleave_cols(2 * dh)
    perm_one = _deinterleave_cols(1)
    w_cq = w_in[:, :na][:, perm_q].astype(BF16)
    w_ck = w_in[:, na:na + nk][:, perm_k].astype(BF16)
    w_dq = w_in[:, o0:o0 + nd][:, perm_d].astype(BF16)
    w_dk = w_in[:, o0 + nd:o0 + 2 * nd][:, perm_d].astype(BF16)
    w_v = jnp.concatenate([w_in[:, na + nk:o0], w_in[:, o0 + 2 * nd:]], 1).astype(BF16)
    qg = row2(q_norm_g[0][perm_one]).astype(F32)
    kg = row2(k_norm_g[0][perm_one]).astype(F32)
    cq_rot, cq_nopos = _proj(h, w_cq, qscale(na), n_rows=n_lat, gain=qg, rope_tabs=tabs, emit_nopos=True,
                             name="proj_cq", **kw)
    ck = _proj(h, w_ck, ones(nk), n_rows=n_all, gain=kg, rope_tabs=tabs, name="proj_ck", **kw)
    dq_rot, dq_nopos = _proj(h, w_dq, qscale(nd), n_rows=n_lat, rope_tabs=tabs, emit_nopos=True,
                             name="proj_dq", **kw)
    dk = _proj(h, w_dk, ones(nd), n_rows=n_all, rope_tabs=tabs, name="proj_dk", **kw)
    vv = _proj(h, w_v, ones(nk + nd), n_rows=n_all, name="proj_odd_v", **kw)
    o_c = _dense_attn(cq_rot, cq_nopos, ck, vv, None, n_batch=n_batch, n_q=seq, q_row0=0, n_k=seq, k_row0=0,
                      ctx_len=ctx_len, ctx_row0=n_lat, n_kv=kvh, g=gq, q_col=0, k_col=0, v_col=0,
                      name="attn_c")
    lam_vecs = jnp.stack([lambda_q1[0], lambda_k1[0], lambda_q2[0], lambda_k2[0]]).astype(F32)
    o_d = _diff_attn(dq_rot, dq_nopos, dk, vv, lam_vecs, row2(subln_g[0]).astype(F32), n_batch=n_batch,
                     seq=seq, ctx_len=ctx_len, n_latent=n_lat, n_heads=dh, v_col=nk // (2 * HEAD_DIM),
                     lambda_init=lambda_init)
    w_out = w_out_odd[0].astype(BF16)
    x1, h2, idx, wts = _outproj_ln((o_c, None), (o_d, None), (xt, None), w_out[:na], w_out[na:], g1, sc2,
                                   sh2, row2(ln1_g[1]), row2(ln1_b[1]), w_router_t, router_bias_col,
                                   n_lat=n_lat, seq=seq, n_groups=n_groups, alpha=alpha)
    y = moe(h2, idx, wts, 1)
    x2, _ = _resid_ln(x1, y, g2, row2(ln2_g[1]), row2(ln2_b[1]), None, None, n_rows=n_lat, seq=seq,
                      n_groups=n_groups, alpha=alpha)
    return x2.reshape(n_batch, seq, d)
```

```python
import functools
import math

import numpy as np
import jax
import jax.numpy as jnp
from jax import lax
from jax.experimental import pallas as pl
from jax.experimental.pallas import tpu as pltpu

F32 = jnp.float32
BF16 = jnp.bfloat16

HEAD_DIM = 128
GRID_W = 64
ROPE_THETA = 10000.0
SCALE = HEAD_DIM ** -0.5
NEG_INF = -1e30
A_WINDOW = 128
A_BLOCK = 128
NA_ROWS = 8
NA_COLS = 16
B_TILE_ROWS = 4
B_HEADS_PER_STEP = 8
ROW_PARTS = 2
N_EXPERTS = 16
N_GROUPS = 4
EXPERTS_PER_GROUP = N_EXPERTS // N_GROUPS
LN_EPS = 1e-5
RMS_EPS = 1e-6
LOG2E = math.log2(math.e)
Q_SCALE = SCALE * LOG2E

V7X_VMEM_BYTES = 64 * 1024 * 1024
VMEM_LIMIT = V7X_VMEM_BYTES - 8 * 1024 * 1024
LANES = 128

NT_DIMS = (((1,), (1,)), ((), ()))


def _params(*sem):
    return pltpu.CompilerParams(dimension_semantics=sem, vmem_limit_bytes=VMEM_LIMIT)


def _adaln_kernel(c_ref, w_ref, b_ref, o_ref):
    c = c_ref[...]
    a = (c * (1.0 / (1.0 + jnp.exp(-c)))).astype(BF16)
    o_ref[...] = jnp.dot(a, w_ref[...].astype(BF16), preferred_element_type=F32) + b_ref[...]


def _adaln(c_rows, w_ada, b_ada):
    depth, d, n6 = w_ada.shape
    rows = c_rows.shape[0]
    tn = 1024
    return pl.pallas_call(
        _adaln_kernel,
        out_shape=jax.ShapeDtypeStruct((depth, rows, n6), F32),
        grid=(depth, n6 // tn),
        in_specs=[
            pl.BlockSpec((rows, d), lambda l, j: (0, 0)),
            pl.BlockSpec((None, d, tn), lambda l, j: (l, 0, j)),
            pl.BlockSpec((None, 1, tn), lambda l, j: (l, 0, j)),
        ],
        out_specs=pl.BlockSpec((None, rows, tn), lambda l, j: (l, 0, j)),
        compiler_params=_params("parallel", "parallel"),
        name="adaln",
    )(c_rows, w_ada, b_ada.reshape(depth, 1, n6))


def _two_stream_specs(tm, width, n_lat_tiles):
    return [pl.BlockSpec((tm, width), lambda i: (jnp.minimum(i, n_lat_tiles - 1), 0)),
            pl.BlockSpec((tm, width), lambda i: (jnp.maximum(i - n_lat_tiles, 0), 0))]


def _pick_stream(lat_ref, ctx_ref, n_lat_tiles):
    if ctx_ref is None:
        return lat_ref[...]
    return jnp.where(pl.program_id(0) >= n_lat_tiles, ctx_ref[...], lat_ref[...])


def _modulate_kernel(x_ref, c_ref, sc_ref, sh_ref, o_ref, *, n_lat_tiles):
    x = _pick_stream(x_ref, c_ref, n_lat_tiles)
    o_ref[...] = (x * (1.0 + sc_ref[0]) + sh_ref[0]).astype(o_ref.dtype)


def _modulate(x_lat, x_ctx, sc, sh, *, seq, n_groups):
    d = x_lat.shape[1]
    rows = x_lat.shape[0] + x_ctx.shape[0]
    tm = 512
    n_lat_tiles = x_lat.shape[0] // tm
    grp = lambda i: (jnp.minimum(i // (seq // tm), n_groups - 1), 0, 0)
    return pl.pallas_call(
        functools.partial(_modulate_kernel, n_lat_tiles=n_lat_tiles),
        out_shape=jax.ShapeDtypeStruct((rows, d), BF16),
        grid=(rows // tm,),
        in_specs=_two_stream_specs(tm, d, n_lat_tiles)
        + [pl.BlockSpec((1, 1, d), grp), pl.BlockSpec((1, 1, d), grp)],
        out_specs=pl.BlockSpec((tm, d), lambda i: (i, 0)),
        compiler_params=_params("parallel"),
        name="modulate",
    )(x_lat, x_ctx, sc, sh)


def _cast_kernel(x_ref, o_ref):
    o_ref[...] = x_ref[...].astype(o_ref.dtype)


def _cast_bf16(w):
    shape = w.shape
    w2 = w.reshape(-1, shape[-1])
    tm = min(w2.shape[0], (2 * 1024 * 1024) // shape[-1])
    out = pl.pallas_call(
        _cast_kernel,
        out_shape=jax.ShapeDtypeStruct(w2.shape, BF16),
        grid=(w2.shape[0] // tm,),
        in_specs=[pl.BlockSpec((tm, shape[-1]), lambda i: (i, 0))],
        out_specs=pl.BlockSpec((tm, shape[-1]), lambda i: (i, 0)),
        compiler_params=_params("parallel"),
        name="cast_bf16",
    )(w2)
    return out.reshape(shape)


def _proj_kernel(*refs, tn, norm, rope, emit_nopos):
    it = iter(refs)
    x_ref, w_ref, cs_ref = next(it), next(it), next(it)
    g_ref = next(it) if norm else None
    cos_ref, sin_ref = (next(it), next(it)) if rope else (None, None)
    o_ref = next(it)
    n_ref = next(it) if emit_nopos else None
    tm = x_ref.shape[0]
    rows_per_part = tm // ROW_PARTS
    for part in range(ROW_PARTS):
        rows = slice(part * rows_per_part, (part + 1) * rows_per_part)
        acc = jnp.dot(x_ref[rows, :], w_ref[...], preferred_element_type=F32)
        for hd in range(tn // HEAD_DIM):
            sl = slice(hd * HEAD_DIM, (hd + 1) * HEAD_DIM)
            xh = acc[:, sl]
            if norm:
                xh = xh * lax.rsqrt(jnp.mean(xh * xh, axis=-1, keepdims=True) + RMS_EPS) * g_ref[...]
            cs = cs_ref[:, sl]
            if rope:
                rot = xh * cos_ref[rows, :] + pltpu.roll(xh, HEAD_DIM // 2, 1) * sin_ref[rows, :]
                o_ref[rows, sl] = (rot * cs).astype(o_ref.dtype)
                if emit_nopos:
                    n_ref[rows, sl] = (xh * cs).astype(n_ref.dtype)
            else:
                o_ref[rows, sl] = (xh * cs).astype(o_ref.dtype)


def _proj(x, w, col_scale, *, n_rows, seq, n_latent, tn=None, gain=None, rope_tabs=None,
          emit_nopos=False, name="proj"):
    d = x.shape[1]
    nc = w.shape[1]
    tm = min(1024, seq)
    tn = nc if tn is None else tn
    norm, rope = gain is not None, rope_tabs is not None
    in_specs = [pl.BlockSpec((tm, d), lambda i, j: (i, 0)),
                pl.BlockSpec((d, tn), lambda i, j: (0, j)),
                pl.BlockSpec((1, tn), lambda i, j: (0, j))]
    args = [x, w, col_scale]
    if norm:
        in_specs.append(pl.BlockSpec((1, HEAD_DIM), lambda i, j: (0, 0)))
        args.append(gain)
    if rope:
        per_seq = seq // tm
        tab = lambda i, j: (jnp.where(i < n_latent // tm, i % per_seq, per_seq), 0)
        in_specs += [pl.BlockSpec((tm, HEAD_DIM), tab)] * 2
        args += list(rope_tabs)
    n_out = 2 if emit_nopos else 1
    out_shape = [jax.ShapeDtypeStruct((n_rows, nc), BF16)] * n_out
    out_specs = [pl.BlockSpec((tm, tn), lambda i, j: (i, j))] * n_out
    res = pl.pallas_call(
        functools.partial(_proj_kernel, tn=tn, norm=norm, rope=rope, emit_nopos=emit_nopos),
        out_shape=out_shape,
        grid=(n_rows // tm, nc // tn),
        in_specs=in_specs,
        out_specs=out_specs,
        compiler_params=_params("parallel", "parallel"),
        name=name,
    )(*args)
    return res if emit_nopos else res[0]


def _stack_heads(x, g):
    if g == 1:
        return x
    return jnp.concatenate([x[:, i * HEAD_DIM:(i + 1) * HEAD_DIM] for i in range(g)], axis=0)


def _unstack_store(o_ref, o, g, t, col0=0):
    for i in range(g):
        o_ref[:, col0 + i * HEAD_DIM:col0 + (i + 1) * HEAD_DIM] = o[i * t:(i + 1) * t].astype(o_ref.dtype)


def _sink_column(sink_ref, first_head, g, t):
    cols = [jnp.full((t, 1), sink_ref[first_head + i] * LOG2E, F32) for i in range(g)]
    return cols[0] if g == 1 else jnp.concatenate(cols, axis=0)


def _with_ones(v):
    return jnp.concatenate([v, jnp.ones_like(v)], axis=1)


def _row_max(pieces):
    cols = [p[:, c:c + LANES] for p in pieces for c in range(0, p.shape[1], LANES)]
    return jnp.max(functools.reduce(jnp.maximum, cols), axis=-1, keepdims=True)


def _online_update(s, v_aug, m, acc):
    m_new = jnp.maximum(m, jnp.max(s, axis=-1, keepdims=True))
    p = jnp.exp2(s - m_new)
    acc = jnp.exp2(m - m_new) * acc + jnp.dot(p.astype(BF16), v_aug, preferred_element_type=F32)
    return m_new, acc


def _band_kernel(*refs, nh, g, bq, has_sink, same_qc, shared_bias):
    it = iter(refs)
    sink_ref = next(it) if has_sink else None
    q_ref = next(it)
    qc_ref = q_ref if same_qc else next(it)
    k_refs = [next(it) for _ in range(3)]
    v_refs = [next(it) for _ in range(3)]
    kx_ref, vx_ref, bias_ref, o_ref = next(it), next(it), next(it), next(it)
    hw = g * HEAD_DIM
    for h in range(nh):
        qs = slice(h * hw, (h + 1) * hw)
        ks = slice(h * HEAD_DIM, (h + 1) * HEAD_DIM)
        q = _stack_heads(q_ref[:, qs], g)
        qc = q if same_qc else _stack_heads(qc_ref[:, qs], g)
        bias = bias_ref.at[0 if shared_bias else h]
        s = [lax.dot_general(q, k_refs[j][:, ks], NT_DIMS, preferred_element_type=F32)
             + bias[:, j * bq:(j + 1) * bq] for j in range(3)]
        s.append(lax.dot_general(qc, kx_ref[:, ks], NT_DIMS, preferred_element_type=F32))
        vs = [r[:, ks] for r in v_refs] + [vx_ref[:, ks]]
        m = _row_max(s)
        if has_sink:
            sink = _sink_column(sink_ref, (pl.program_id(0) * nh + h) * g, g, bq)
            m = jnp.maximum(m, sink)
        o = functools.reduce(jnp.add, [
            jnp.dot(jnp.exp2(sj - m).astype(BF16), _with_ones(vj), preferred_element_type=F32)
            for sj, vj in zip(s, vs)])
        den = o[:, HEAD_DIM:]
        if has_sink:
            den = den + jnp.exp2(sink - m)
        _unstack_store(o_ref, o[:, :HEAD_DIM] / den, g, bq, col0=h * hw)


def _band_attn(q, qc, k, v, bias, sink, *, n_batch, seq, ctx_len, n_latent, n_kv, nh, g, bq,
               q_col, k_col, v_col, name):
    nblk = seq // bq
    ctx_blk = n_latent // ctx_len
    shared_bias = bias.shape[0] == 1
    has_sink, same_qc = sink is not None, qc is None
    qw, kw = nh * g * HEAD_DIM, nh * HEAD_DIM

    def q_map(hg, i, b):
        return (b * nblk + i, q_col + hg)

    def kv_map(col, off):
        return lambda hg, i, b: (b * nblk + jnp.clip(i + off, 0, nblk - 1), col + hg)

    def ctx_map(col):
        return lambda hg, i, b: (ctx_blk + b, col + hg)

    def bias_map(hg, i, b):
        case = jnp.where(i == 0, 0, jnp.where(i == nblk - 1, 2, 1))
        return (0 if shared_bias else hg, case, 0, 0)

    in_specs, args = [], []
    if has_sink:
        in_specs.append(pl.BlockSpec(memory_space=pltpu.SMEM))
        args.append(sink)
    in_specs.append(pl.BlockSpec((bq, qw), q_map))
    args.append(q)
    if not same_qc:
        in_specs.append(pl.BlockSpec((bq, qw), q_map))
        args.append(qc)
    in_specs += [pl.BlockSpec((bq, kw), kv_map(k_col, off)) for off in (-1, 0, 1)]
    args += [k] * 3
    in_specs += [pl.BlockSpec((bq, kw), kv_map(v_col, off)) for off in (-1, 0, 1)]
    args += [v] * 3
    in_specs += [pl.BlockSpec((ctx_len, kw), ctx_map(k_col)),
                 pl.BlockSpec((ctx_len, kw), ctx_map(v_col)),
                 pl.BlockSpec((1 if shared_bias else nh, None, g * bq, 3 * bq), bias_map)]
    args += [k, v, bias]
    return pl.pallas_call(
        functools.partial(_band_kernel, nh=nh, g=g, bq=bq, has_sink=has_sink, same_qc=same_qc,
                          shared_bias=shared_bias),
        out_shape=jax.ShapeDtypeStruct((n_latent, n_kv * g * HEAD_DIM), BF16),
        grid=(n_kv // nh, nblk, n_batch),
        in_specs=in_specs,
        out_specs=pl.BlockSpec((bq, qw), lambda hg, i, b: (b * nblk + i, hg)),
        compiler_params=_params("parallel", "parallel", "parallel"),
        name=name,
    )(*args)


def _dense_kernel(*refs, g, tq, tk, n_k, has_ctx, has_sink):
    it = iter(refs)
    sink_ref = next(it) if has_sink else None
    q_ref = next(it)
    qc_ref = next(it) if has_ctx else None
    k_ref, v_ref = next(it), next(it)
    kx_ref, vx_ref = (next(it), next(it)) if has_ctx else (None, None)
    o_ref = next(it)
    q = _stack_heads(q_ref[...], g)
    rows = g * tq
    m = jnp.full((rows, 1), NEG_INF, F32)
    acc = jnp.zeros((rows, 2 * HEAD_DIM), F32)
    for c in range(n_k // tk):
        s = lax.dot_general(q, k_ref[c * tk:(c + 1) * tk, :], NT_DIMS, preferred_element_type=F32)
        m, acc = _online_update(s, _with_ones(v_ref[c * tk:(c + 1) * tk, :]), m, acc)
    if has_ctx:
        qc = _stack_heads(qc_ref[...], g)
        s = lax.dot_general(qc, kx_ref[...], NT_DIMS, preferred_element_type=F32)
        m, acc = _online_update(s, _with_ones(vx_ref[...]), m, acc)
    num, den = acc[:, :HEAD_DIM], acc[:, HEAD_DIM:]
    if has_sink:
        sink = _sink_column(sink_ref, pl.program_id(1) * g, g, tq)
        m_new = jnp.maximum(m, sink)
        alpha = jnp.exp2(m - m_new)
        num = alpha * num
        den = alpha * den + jnp.exp2(sink - m_new)
    _unstack_store(o_ref, num / den, g, tq)


def _dense_attn(q, qc, k, v, sink, *, n_batch, n_q, q_row0, n_k, k_row0, ctx_len, ctx_row0,
                n_kv, g, q_col, k_col, v_col, name):
    tq = min(256, n_q)
    tk = min(512, n_k)
    nq_blk = n_q // tq
    has_ctx, has_sink = qc is not None, sink is not None

    def q_map(b, h, i):
        return (q_row0 // tq + b * nq_blk + i, q_col + h)

    in_specs, args = [], []
    if has_sink:
        in_specs.append(pl.BlockSpec(memory_space=pltpu.SMEM))
        args.append(sink)
    in_specs.append(pl.BlockSpec((tq, g * HEAD_DIM), q_map))
    args.append(q)
    if has_ctx:
        in_specs.append(pl.BlockSpec((tq, g * HEAD_DIM), q_map))
        args.append(qc)
    in_specs += [pl.BlockSpec((n_k, HEAD_DIM), lambda b, h, i: (k_row0 // n_k + b, k_col + h)),
                 pl.BlockSpec((n_k, HEAD_DIM), lambda b, h, i: (k_row0 // n_k + b, v_col + h))]
    args += [k, v]
    if has_ctx:
        in_specs += [pl.BlockSpec((ctx_len, HEAD_DIM), lambda b, h, i: (ctx_row0 // ctx_len + b, k_col + h)),
                     pl.BlockSpec((ctx_len, HEAD_DIM), lambda b, h, i: (ctx_row0 // ctx_len + b, v_col + h))]
        args += [k, v]
    return pl.pallas_call(
        functools.partial(_dense_kernel, g=g, tq=tq, tk=tk, n_k=n_k, has_ctx=has_ctx,
                          has_sink=has_sink),
        out_shape=jax.ShapeDtypeStruct((n_batch * n_q, n_kv * g * HEAD_DIM), BF16),
        grid=(n_batch, n_kv, nq_blk),
        in_specs=in_specs,
        out_specs=pl.BlockSpec((tq, g * HEAD_DIM), lambda b, h, i: (b * nq_blk + i, h)),
        compiler_params=_params("parallel", "parallel", "parallel"),
        name=name,
    )(*args)


def _diff_update(s, v, m, l, acc):
    m_new = jnp.maximum(m, jnp.max(s, axis=-1, keepdims=True))
    alpha = jnp.exp2(m - m_new)
    p = jnp.exp2(s - m_new)
    l = alpha * l + jnp.sum(p, axis=-1, keepdims=True)
    acc = alpha * acc + jnp.dot(p.astype(BF16), v, preferred_element_type=F32)
    return m_new, l, acc


def _diff_kernel(lam_ref, g_ref, q_ref, qc_ref, k_ref, v_ref, kx_ref, vx_ref, o_ref, *,
                 tq, tk, n_k, lambda_init):
    dv = 2 * HEAD_DIM
    lam = (jnp.exp(jnp.sum(lam_ref[0:1, :] * lam_ref[1:2, :], axis=1, keepdims=True))
           - jnp.exp(jnp.sum(lam_ref[2:3, :] * lam_ref[3:4, :], axis=1, keepdims=True)) + lambda_init)
    outs = []
    for t in range(2):
        sl = slice(t * HEAD_DIM, (t + 1) * HEAD_DIM)
        q = q_ref[:, sl]
        m = jnp.full((tq, 1), NEG_INF, F32)
        l = jnp.zeros((tq, 1), F32)
        acc = jnp.zeros((tq, dv), F32)
        for c in range(n_k // tk):
            rows = slice(c * tk, (c + 1) * tk)
            s = lax.dot_general(q, k_ref[rows, sl], NT_DIMS, preferred_element_type=F32)
            m, l, acc = _diff_update(s, v_ref[rows, :], m, l, acc)
        s = lax.dot_general(qc_ref[:, sl], kx_ref[:, sl], NT_DIMS, preferred_element_type=F32)
        m, l, acc = _diff_update(s, vx_ref[...], m, l, acc)
        outs.append(acc / l)
    o = outs[0] - lam * outs[1]
    o = o * lax.rsqrt(jnp.mean(o * o, axis=-1, keepdims=True) + RMS_EPS) * g_ref[...]
    o_ref[...] = (o * (1.0 - lambda_init)).astype(o_ref.dtype)


def _diff_attn(q, qc, k, v, lam_vecs, subln_g, *, n_batch, seq, ctx_len, n_latent, n_heads,
               v_col, lambda_init):
    tq = min(512, seq)
    tk = min(512, seq)
    dv = 2 * HEAD_DIM
    nq_blk = seq // tq
    ctx_blk = n_latent // ctx_len
    q_map = lambda b, h, i: (b * nq_blk + i, h)
    return pl.pallas_call(
        functools.partial(_diff_kernel, tq=tq, tk=tk, n_k=seq, lambda_init=lambda_init),
        out_shape=jax.ShapeDtypeStruct((n_latent, n_heads * dv), BF16),
        grid=(n_batch, n_heads, nq_blk),
        in_specs=[
            pl.BlockSpec((4, HEAD_DIM), lambda b, h, i: (0, 0)),
            pl.BlockSpec((1, dv), lambda b, h, i: (0, 0)),
            pl.BlockSpec((tq, dv), q_map),
            pl.BlockSpec((tq, dv), q_map),
            pl.BlockSpec((seq, dv), lambda b, h, i: (b, h)),
            pl.BlockSpec((seq, dv), lambda b, h, i: (b, v_col + h)),
            pl.BlockSpec((ctx_len, dv), lambda b, h, i: (ctx_blk + b, h)),
            pl.BlockSpec((ctx_len, dv), lambda b, h, i: (ctx_blk + b, v_col + h)),
        ],
        out_specs=pl.BlockSpec((tq, dv), q_map),
        compiler_params=_params("parallel", "parallel", "parallel"),
        name="diff_attn",
    )(lam_vecs, subln_g, q, qc, k, v, k, v)


def _layer_norm(z, g, b):
    mu = jnp.mean(z, axis=-1, keepdims=True)
    zc = z - mu
    var = jnp.mean(zc * zc, axis=-1, keepdims=True)
    return zc * lax.rsqrt(var + LN_EPS) * g + b


def _split_bf16(x):
    hi = x.astype(BF16)
    return hi, (x - hi.astype(F32)).astype(BF16)


def _first_argmax(vals, idx, width):
    m = jnp.max(vals, axis=0, keepdims=True)
    first = jnp.min(jnp.where(vals == m, idx, float(width)), axis=0, keepdims=True)
    return m, first


def _route(logits, bias):
    e, t = logits.shape
    scores = 1.0 / (1.0 + jnp.exp(-logits))
    biased = scores + bias
    row_i = lax.broadcasted_iota(jnp.int32, (e, t), 0)
    grp = lax.shift_right_logical(row_i, int(math.log2(EXPERTS_PER_GROUP)))
    row = row_i.astype(F32)
    neg = -jnp.inf
    best_score, best = None, None
    for gi in range(N_GROUPS):
        vg = jnp.where(grp == gi, biased, neg)
        m1, i1 = _first_argmax(vg, row, e)
        m2 = jnp.max(jnp.where(row == i1, neg, vg), axis=0, keepdims=True)
        gs = m1 + m2
        if gi == 0:
            best_score, best = gs, jnp.zeros((1, t), jnp.int32)
        else:
            upd = gs > best_score
            best = jnp.where(upd, gi, best)
            best_score = jnp.where(upd, gs, best_score)
    masked = jnp.where(grp == best, biased, neg)
    _, e1 = _first_argmax(masked, row, e)
    _, e2 = _first_argmax(jnp.where(row == e1, neg, masked), row, e)
    w1 = jnp.sum(jnp.where(row == e1, scores, 0.0), axis=0, keepdims=True)
    w2 = jnp.sum(jnp.where(row == e2, scores, 0.0), axis=0, keepdims=True)
    den = w1 + w2
    two = lax.broadcasted_iota(jnp.int32, (2, t), 0)
    return jnp.where(two == 0, e1, e2).astype(jnp.int32), jnp.where(two == 0, w1 / den, w2 / den)


def _outproj_kernel(*refs, alpha, two_streams, n_lat_tiles):
    it = iter(refs)
    xa_ref, xa_ctx = next(it), (next(it) if two_streams else None)
    xb_ref, xb_ctx = next(it), (next(it) if two_streams else None)
    xres_ref, xres_ctx = next(it), (next(it) if two_streams else None)
    (wa_ref, wb_ref, g1_ref, sc2_ref, sh2_ref, lng_ref, lnb_ref, wrt_ref, rb_ref,
     x1_ref, h2_ref, idx_ref, wts_ref) = it
    w_hi, w_lo = _split_bf16(wrt_ref[...])
    w_cat = jnp.concatenate([w_hi, w_lo], axis=0)
    tm = x1_ref.shape[0]
    rows_per_part = tm // ROW_PARTS
    for part in range(ROW_PARTS):
        rows = slice(part * rows_per_part, (part + 1) * rows_per_part)
        pick = lambda lat, ctx: _pick_stream(lat.at[rows, :], None if ctx is None else ctx.at[rows, :], n_lat_tiles)
        y = (jnp.dot(pick(xa_ref, xa_ctx), wa_ref[...], preferred_element_type=F32)
             + jnp.dot(pick(xb_ref, xb_ctx), wb_ref[...], preferred_element_type=F32))
        x1 = _layer_norm(alpha * pick(xres_ref, xres_ctx) + g1_ref[0] * y, lng_ref[...], lnb_ref[...])
        x1_ref[rows, :] = x1
        h2 = x1 * (1.0 + sc2_ref[0]) + sh2_ref[0]
        h2_ref[rows, :] = h2
        h_hi, h_lo = _split_bf16(h2)
        both = lax.dot_general(w_cat, h_hi, NT_DIMS, preferred_element_type=F32)
        logits = (both[:N_EXPERTS] + both[N_EXPERTS:]
                  + lax.dot_general(w_hi, h_lo, NT_DIMS, preferred_element_type=F32))
        idx, wts = _route(logits, rb_ref[...])
        idx_ref[:, rows] = idx
        wts_ref[:, rows] = wts


def _outproj_ln(xa, xb, xres, wa, wb, g1, sc2, sh2, ln_g, ln_b, w_router_t, router_bias, *,
                n_lat, seq, n_groups, alpha):
    two_streams = xa[1] is not None
    d = xres[0].shape[1]
    ka, kb = xa[0].shape[1], xb[0].shape[1]
    tm = 256 * ROW_PARTS
    n_lat_tiles = n_lat // tm
    n_rows = n_lat + (xres[1].shape[0] if two_streams else 0)
    row = lambda i: (i, 0)
    col = lambda i: (0, i)
    const = lambda i: (0, 0)
    grp = lambda i: (jnp.minimum(i // (seq // tm), n_groups - 1), 0, 0)
    in_specs, args = [], []
    for pair, width in ((xa, ka), (xb, kb), (xres, d)):
        if two_streams:
            in_specs += _two_stream_specs(tm, width, n_lat_tiles)
            args += list(pair)
        else:
            in_specs.append(pl.BlockSpec((tm, width), row))
            args.append(pair[0])
    in_specs += [pl.BlockSpec((ka, d), const), pl.BlockSpec((kb, d), const),
                 pl.BlockSpec((1, 1, d), grp), pl.BlockSpec((1, 1, d), grp), pl.BlockSpec((1, 1, d), grp),
                 pl.BlockSpec((1, d), const), pl.BlockSpec((1, d), const),
                 pl.BlockSpec((N_EXPERTS, d), const), pl.BlockSpec((N_EXPERTS, 1), const)]
    args += [wa, wb, g1, sc2, sh2, ln_g, ln_b, w_router_t, router_bias]
    return pl.pallas_call(
        functools.partial(_outproj_kernel, alpha=alpha, two_streams=two_streams, n_lat_tiles=n_lat_tiles),
        out_shape=[jax.ShapeDtypeStruct((n_rows, d), F32), jax.ShapeDtypeStruct((n_rows, d), F32),
                   jax.ShapeDtypeStruct((2, n_rows), jnp.int32), jax.ShapeDtypeStruct((2, n_rows), F32)],
        grid=(n_rows // tm,),
        in_specs=in_specs,
        out_specs=[pl.BlockSpec((tm, d), row), pl.BlockSpec((tm, d), row),
                   pl.BlockSpec((2, tm), col), pl.BlockSpec((2, tm), col)],
        compiler_params=_params("parallel"),
        name="outproj_ln",
    )(*args)


def _moe_kernel(te_ref, nv_ref, src_ref, dst_ref, h_hbm, gw_ref, wg_ref, wu_ref, wd_ref, y_hbm,
                xbuf, obuf, gsem, ssem, *, tm, ff_chunks):
    i = pl.program_id(0)
    n = pl.num_programs(0)
    nv = nv_ref[0]

    def start_gather(tile, s):
        base = tile * (tm // 2)
        for r in range(tm):
            word = src_ref[base + r // 2]
            tok = jnp.bitwise_and(word, 0xFFFF) if r % 2 == 0 else lax.shift_right_logical(word, 16)
            pltpu.make_async_copy(h_hbm.at[pl.ds(tok, 1)], xbuf.at[s, pl.ds(r, 1)],
                                  gsem.at[s]).start(priority=r % 2)

    def wait_gather(s):
        pltpu.make_async_copy(h_hbm.at[pl.ds(0, tm)], xbuf.at[s], gsem.at[s]).wait()

    def start_scatter(tile, s):
        base = tile * tm
        for r in range(tm):
            pltpu.make_async_copy(obuf.at[s, pl.ds(r, 1)], y_hbm.at[pl.ds(dst_ref[base + r], 1)],
                                  ssem.at[s]).start(priority=r % 2)

    def wait_scatter(s):
        pltpu.make_async_copy(obuf.at[s], y_hbm.at[pl.ds(0, tm)], ssem.at[s]).wait()

    @pl.when(i == 0)
    def _():
        start_gather(0, 0)
        obuf[...] = jnp.zeros_like(obuf)
        pad0 = y_hbm.shape[0] - 2 * tm
        fills = [pltpu.make_async_copy(obuf.at[s], y_hbm.at[pl.ds(pad0 + s * tm, tm)], ssem.at[s])
                 for s in range(2)]
        for f in fills:
            f.start()
        for f in fills:
            f.wait()

    def step(slot):
        wait_gather(slot)

        @pl.when(i >= 2)
        def _():
            wait_scatter(slot)

        start_gather(jnp.minimum(i + 1, n - 1), 1 - slot)
        x = xbuf[slot].astype(BF16)
        fc = wg_ref.shape[1] // ff_chunks
        y = None
        for c in range(ff_chunks):
            cs = slice(c * fc, (c + 1) * fc)
            gate = jnp.dot(x, wg_ref[:, cs], preferred_element_type=F32)
            up = jnp.dot(x, wu_ref[:, cs], preferred_element_type=F32)
            act = (gate * (1.0 / (1.0 + jnp.exp(-gate))) * up).astype(BF16)
            part = jnp.dot(act, wd_ref[cs, :], preferred_element_type=F32)
            y = part if y is None else y + part
        obuf[slot] = y * gw_ref[...]
        start_scatter(i, slot)

    for slot in range(2):
        @pl.when(jnp.logical_and(i < nv, jnp.bitwise_and(i, 1) == slot))
        def _():
            step(slot)

    @pl.when(i == n - 1)
    def _():
        wait_gather(jnp.bitwise_and(nv, 1))

        @pl.when(nv >= 1)
        def _():
            wait_scatter(jnp.bitwise_and(nv - 1, 1))

        @pl.when(nv >= 2)
        def _():
            wait_scatter(jnp.bitwise_and(nv, 1))


def _moe_ffn(h2, src_packed, dst_row, gw, tile_expert, n_valid, w_gate, w_up, w_down, *, layer, tm):
    t, d = h2.shape
    ff = w_gate.shape[3]
    n_tiles = dst_row.shape[0] // tm
    return pl.pallas_call(
        functools.partial(_moe_kernel, tm=tm, ff_chunks=2),
        out_shape=jax.ShapeDtypeStruct((2 * t + 2 * tm, d), F32),
        grid_spec=pltpu.PrefetchScalarGridSpec(
            num_scalar_prefetch=4,
            grid=(n_tiles,),
            in_specs=[pl.BlockSpec(memory_space=pl.ANY),
                      pl.BlockSpec((tm, 1), lambda i, te, nv, src, dst: (i, 0)),
                      pl.BlockSpec((None, None, d, ff), lambda i, te, nv, src, dst: (layer, te[i], 0, 0)),
                      pl.BlockSpec((None, None, d, ff), lambda i, te, nv, src, dst: (layer, te[i], 0, 0)),
                      pl.BlockSpec((None, None, ff, d), lambda i, te, nv, src, dst: (layer, te[i], 0, 0))],
            out_specs=pl.BlockSpec(memory_space=pl.ANY),
            scratch_shapes=[pltpu.VMEM((2, tm, d), F32), pltpu.VMEM((2, tm, d), F32),
                            pltpu.SemaphoreType.DMA((2,)), pltpu.SemaphoreType.DMA((2,))],
        ),
        compiler_params=_params("arbitrary"),
        name="moe_ffn",
    )(tile_expert, n_valid, src_packed, dst_row, h2, gw, w_gate, w_up, w_down)


def _take(x, idx):
    return x.at[idx].get(mode="promise_in_bounds")


def _dispatch(idx, wts, *, tm):
    t = idx.shape[1]
    n_pairs = 2 * t
    n_tiles = n_pairs // tm + N_EXPERTS
    e_flat = idx.reshape(n_pairs)
    w_flat = wts.reshape(n_pairs)
    _, order, w_sorted = lax.sort((e_flat, jnp.arange(n_pairs, dtype=jnp.int32), w_flat), num_keys=1)
    experts = jnp.arange(N_EXPERTS, dtype=jnp.int32)
    counts = jnp.sum(e_flat[None, :] == experts[:, None], axis=1).astype(jnp.int32)
    start = jnp.cumsum(counts) - counts
    padded = ((counts + tm - 1) // tm) * tm
    pend = jnp.cumsum(padded)
    pstart = pend - padded
    tile_pos = jnp.arange(n_tiles, dtype=jnp.int32) * tm
    tile_expert = jnp.minimum(jnp.sum(tile_pos[:, None] >= pend[None, :], axis=1), N_EXPERTS - 1).astype(jnp.int32)
    local = (tile_pos - pstart[tile_expert])[:, None] + jnp.arange(tm, dtype=jnp.int32)[None, :]
    valid = local < counts[tile_expert][:, None]
    sorted_pos = jnp.clip(start[tile_expert][:, None] + local, 0, n_pairs - 1)
    src_pair = _take(order, sorted_pos)
    src_tok = jnp.where(valid, jnp.where(src_pair >= t, src_pair - t, src_pair), 0)
    pad_row = n_pairs + (tile_pos[:, None] + jnp.arange(tm, dtype=jnp.int32)[None, :]) % (2 * tm)
    dst_row = jnp.where(valid, src_pair, pad_row)
    gw = jnp.where(valid, _take(w_sorted, sorted_pos), 0.0).astype(F32)
    src_packed = jnp.bitwise_or(src_tok[:, 0::2], jnp.left_shift(src_tok[:, 1::2], 16))
    n_valid = (pend[-1] // tm).astype(jnp.int32).reshape(1)
    return src_packed.reshape(-1), dst_row.reshape(-1), gw.reshape(-1, 1), tile_expert, n_valid


def _resid_ln_kernel(*refs, alpha, emit_h):
    it = iter(refs)
    x_ref, f0_ref, f1_ref, g_ref, lng_ref, lnb_ref = (next(it) for _ in range(6))
    sc_ref, sh_ref = (next(it), next(it)) if emit_h else (None, None)
    o_ref = next(it)
    f = f0_ref[...] + f1_ref[...]
    x2 = _layer_norm(alpha * x_ref[...] + g_ref[0] * f, lng_ref[...], lnb_ref[...])
    o_ref[...] = x2
    if emit_h:
        h_ref = next(it)
        h_ref[...] = (x2 * (1.0 + sc_ref[0]) + sh_ref[0]).astype(h_ref.dtype)


def _resid_ln(x, y, gate, ln_g, ln_b, next_sc, next_sh, *, n_rows, seq, n_groups, alpha):
    d = x.shape[1]
    tm = 512
    emit_h = next_sc is not None
    row = lambda i: (i, 0)
    const = lambda i: (0, 0)
    grp = lambda i: (jnp.minimum(i // (seq // tm), n_groups - 1), 0, 0)
    in_specs = [pl.BlockSpec((tm, d), row), pl.BlockSpec((tm, d), row),
                pl.BlockSpec((tm, d), lambda i: (n_rows // tm + i, 0)), pl.BlockSpec((1, 1, d), grp),
                pl.BlockSpec((1, d), const), pl.BlockSpec((1, d), const)]
    args = [x, y, y, gate, ln_g, ln_b]
    out_shape = [jax.ShapeDtypeStruct((n_rows, d), F32)]
    out_specs = [pl.BlockSpec((tm, d), row)]
    if emit_h:
        in_specs += [pl.BlockSpec((1, 1, d), grp)] * 2
        args += [next_sc, next_sh]
        out_shape.append(jax.ShapeDtypeStruct((n_rows, d), BF16))
        out_specs.append(pl.BlockSpec((tm, d), row))
    res = pl.pallas_call(
        functools.partial(_resid_ln_kernel, alpha=alpha, emit_h=emit_h),
        out_shape=out_shape,
        grid=(n_rows // tm,),
        in_specs=in_specs,
        out_specs=out_specs,
        compiler_params=_params("parallel"),
        name="resid_ln",
    )(*args)
    return res if emit_h else (res[0], None)


def _rope_tables(seq, tm):
    t = jnp.arange(seq, dtype=jnp.int32)
    n_freq = HEAD_DIM // 4
    inv = ROPE_THETA ** (-jnp.arange(n_freq, dtype=F32) / n_freq)
    row = (t // GRID_W).astype(F32)
    col = (t % GRID_W).astype(F32)
    ang = jnp.concatenate([row[:, None] * inv[None], col[:, None] * inv[None]], -1)
    cos, sin = jnp.cos(ang), jnp.sin(ang)
    cos2 = jnp.concatenate([cos, cos], -1)
    sin2 = jnp.concatenate([-sin, sin], -1)
    cos2 = jnp.concatenate([cos2, jnp.ones((tm, HEAD_DIM), F32)], 0)
    sin2 = jnp.concatenate([sin2, jnp.zeros((tm, HEAD_DIM), F32)], 0)
    return cos2, sin2


def _window_bias(seq, g):
    bq = A_BLOCK
    nblk = seq // bq
    rel = np.arange(3 * bq)[None, :] - np.arange(bq)[:, None]
    band = (rel >= bq - A_WINDOW) & (rel <= bq + A_WINDOW)
    tabs = []
    for blk in (0, min(1, nblk - 1), nblk - 1):
        kpos = blk * bq - bq + np.arange(3 * bq)
        ok = band & ((kpos >= 0) & (kpos < seq))[None, :]
        tabs.append(np.tile(np.where(ok, 0.0, NEG_INF).astype(np.float32), (g, 1)))
    return jnp.asarray(np.stack(tabs)[None])


def _neighbourhood_bias(rpb, seq):
    n_heads = rpb.shape[0]
    rows = seq // GRID_W
    kh, kw = min(NA_ROWS, rows), NA_COLS
    tr = B_TILE_ROWS
    nblk = rows // tr
    col = np.arange(GRID_W)
    col_start = np.clip(col - kw // 2, 0, GRID_W - kw)
    col_ok = (col[None, :] >= col_start[:, None]) & (col[None, :] < col_start[:, None] + kw)
    assert np.all(np.abs(col[None, :] - col[:, None])[col_ok] <= NA_COLS - 1)
    masks = []
    for blk in (0, min(1, nblk - 1), nblk - 1):
        r = blk * tr + np.arange(tr)
        r0 = np.clip(r - kh // 2, 0, rows - kh)
        krow = (blk - 1) * tr + np.arange(3 * tr)
        row_ok = (krow[None, :] >= r0[:, None]) & (krow[None, :] < r0[:, None] + kh)
        ok = row_ok[:, None, :, None] & col_ok[None, :, None, :]
        masks.append(ok.reshape(tr * GRID_W, 3 * tr * GRID_W))
    ok = jnp.asarray(np.stack(masks))
    pad = GRID_W - NA_COLS
    rp = jnp.pad(rpb.astype(F32) * LOG2E, ((0, 0), (0, 0), (pad, pad)))
    cexp = jnp.stack([rp[:, :, GRID_W - 1 - cq:2 * GRID_W - 1 - cq] for cq in range(GRID_W)], axis=2)
    off = NA_ROWS - 1 - tr
    assert off - (tr - 1) >= 0 and off + 3 * tr - 1 <= 2 * NA_ROWS - 2
    t5 = jnp.stack([cexp[:, off - rq:off - rq + 3 * tr] for rq in range(tr)], axis=1)
    tab = t5.transpose(0, 1, 3, 2, 4).reshape(n_heads, tr * GRID_W, 3 * tr * GRID_W)
    return jnp.where(ok[None], tab[:, None], NEG_INF)


def _deinterleave_cols(n_heads):
    one = np.concatenate([np.arange(0, HEAD_DIM, 2), np.arange(1, HEAD_DIM, 2)])
    return np.concatenate([h * HEAD_DIM + one for h in range(n_heads)])


def kernel(x, c, ctx, c_ctx, w_ada, b_ada, ln1_g, ln1_b, ln2_g, ln2_b, w_in_even, w_out_even, sink_logits, na_rpb, w_in_odd, w_out_odd, q_norm_g, k_norm_g, lambda_q1, lambda_k1, lambda_q2, lambda_k2, subln_g, w_router, router_bias, w_exp_gate, w_exp_up, w_exp_down):
    n_batch, seq, d = x.shape
    ctx_len = ctx.shape[1]
    depth = w_ada.shape[0]
    assert depth == 2, "one even (A||B) layer followed by one odd (C||D) layer"
    n_lat = n_batch * seq
    n_ctx = n_batch * ctx_len
    n_all = n_lat + n_ctx
    n_groups = n_batch + 1
    alpha = float((2 * depth) ** 0.25)
    heads = d // HEAD_DIM
    qh = heads // 2
    kvh = qh // 4
    gq = qh // kvh
    bh = heads - qh
    dh = (heads - qh) // 2
    moe_tm = 512
    proj_tm = min(1024, seq)
    assert seq % GRID_W == 0 and (seq // GRID_W) % B_TILE_ROWS == 0 and seq // GRID_W >= NA_ROWS
    assert n_ctx % proj_tm == 0 and n_lat % ctx_len == 0 and (2 * n_all) % moe_tm == 0
    assert ctx_len % LANES == 0 and bh % B_HEADS_PER_STEP == 0
    assert n_all <= 1 << 16, "token ids are packed as 16-bit halves for the MoE row gather"

    pad_rows = -(-n_groups // 16) * 16
    c_rows = jnp.concatenate([c, c_ctx[None], jnp.zeros((pad_rows - n_groups, d), F32)], 0)
    mod = _adaln(c_rows, w_ada, b_ada)[:, :n_groups]
    mod = mod.reshape(depth, n_groups, 6, 1, d)
    mods = [[mod[l, :, j] for j in range(6)] for l in range(depth)]

    cos2, sin2 = _rope_tables(seq, proj_tm)
    tabs = (cos2, sin2)
    x_lat, x_ctx = x.reshape(n_lat, d), ctx.reshape(n_ctx, d)
    row2 = lambda v: v.reshape(1, -1)
    ones = lambda n: jnp.ones((1, n), F32)
    qscale = lambda n: jnp.full((1, n), Q_SCALE, F32)
    w_router_t = w_router.T.astype(F32)
    router_bias_col = router_bias.reshape(-1, 1).astype(F32)
    wg_all, wu_all, wd_all = (_cast_bf16(w) for w in (w_exp_gate, w_exp_up, w_exp_down))

    def moe(h2, idx, wts, layer):
        src_packed, dst_row, gw, tile_expert, n_valid = _dispatch(idx, wts, tm=moe_tm)
        return _moe_ffn(h2, src_packed, dst_row, gw, tile_expert, n_valid, wg_all, wu_all, wd_all,
                        layer=layer, tm=moe_tm)

    sh1, sc1, g1, sh2, sc2, g2 = mods[0]
    h = _modulate(x_lat, x_ctx, sc1, sh1, seq=seq, n_groups=n_groups)
    w_in = w_in_even[0]
    na, nk, nb = qh * HEAD_DIM, kvh * HEAD_DIM, bh * HEAD_DIM
    perm_q, perm_k = _deinterleave_cols(qh), _deinterleave_cols(kvh)
    w_aq = w_in[:, :na][:, perm_q].astype(BF16)
    w_ak = w_in[:, na:na + nk][:, perm_k].astype(BF16)
    w_rest = jnp.concatenate([w_in[:, na + 2 * nk:], w_in[:, na + nk:na + 2 * nk]], 1).astype(BF16)
    rest_scale = jnp.concatenate([qscale(nb), ones(2 * nb + nk)], 1)
    kw = dict(seq=seq, n_latent=n_lat)
    aq_rot, aq_nopos = _proj(h, w_aq, qscale(na), n_rows=n_all, rope_tabs=tabs, emit_nopos=True,
                             name="proj_aq", **kw)
    ak = _proj(h, w_ak, ones(nk), n_rows=n_all, rope_tabs=tabs, name="proj_ak", **kw)
    rest = _proj(h, w_rest, rest_scale, n_rows=n_all, tn=w_rest.shape[1] // 2, name="proj_even_rest", **kw)
    sink = sink_logits[0].astype(F32)
    akw = dict(n_batch=n_batch, seq=seq, ctx_len=ctx_len, n_latent=n_lat)
    o_a = _band_attn(aq_rot, aq_nopos, ak, rest, _window_bias(seq, gq), sink, n_kv=kvh, nh=kvh, g=gq,
                     bq=A_BLOCK, q_col=0, k_col=0, v_col=3 * nb // nk, name="attn_a", **akw)
    bw = B_HEADS_PER_STEP * HEAD_DIM
    o_b = _band_attn(rest, None, rest, rest, _neighbourhood_bias(na_rpb[0], seq), None, n_kv=bh,
                     nh=B_HEADS_PER_STEP, g=1, bq=B_TILE_ROWS * GRID_W, q_col=0, k_col=nb // bw,
                     v_col=2 * nb // bw, name="attn_b", **akw)
    ckw = dict(n_batch=n_batch, n_q=ctx_len, q_row0=n_lat, n_k=ctx_len, k_row0=n_lat, ctx_len=ctx_len,
               ctx_row0=n_lat)
    oc_a = _dense_attn(aq_rot, None, ak, rest, sink, n_kv=kvh, g=gq, q_col=0, k_col=0,
                       v_col=3 * nb // HEAD_DIM, name="ctx_attn_a", **ckw)
    oc_b = _dense_attn(rest, None, rest, rest, None, n_kv=bh, g=1, q_col=0, k_col=bh, v_col=2 * bh,
                       name="ctx_attn_b", **ckw)
    w_out = w_out_even[0].astype(BF16)
    x1, h2, idx, wts = _outproj_ln((o_a, oc_a), (o_b, oc_b), (x_lat, x_ctx), w_out[:na], w_out[na:], g1, sc2,
                                   sh2, row2(ln1_g[0]), row2(ln1_b[0]), w_router_t, router_bias_col,
                                   n_lat=n_lat, seq=seq, n_groups=n_groups, alpha=alpha)
    y = moe(h2, idx, wts, 0)
    nsh1, nsc1 = mods[1][0], mods[1][1]
    xt, h = _resid_ln(x1, y, g2, row2(ln2_g[0]), row2(ln2_b[0]), nsc1, nsh1, n_rows=n_all, seq=seq,
                      n_groups=n_groups, alpha=alpha)

    sh1, sc1, g1, sh2, sc2, g2 = mods[1]
    lambda_init = 0.8 - 0.6 * math.exp(-0.3 * 1)
    w_in = w_in_odd[0]
    nd = dh * 2 * HEAD_DIM
    o0 = na + 2 * nk
    perm_d = _deinterleave_cols(2 * dh)
    perm_one = _deinterleave_cols(1)
    w_cq = w_in[:, :na][:, perm_q].astype(BF16)
    w_ck = w_in[:, na:na + nk][:, perm_k].astype(BF16)
    w_dq = w_in[:, o0:o0 + nd][:, perm_d].astype(BF16)
    w_dk = w_in[:, o0 + nd:o0 + 2 * nd][:, perm_d].astype(BF16)
    w_v = jnp.concatenate([w_in[:, na + nk:o0], w_in[:, o0 + 2 * nd:]], 1).astype(BF16)
    qg = row2(q_norm_g[0][perm_one]).astype(F32)
    kg = row2(k_norm_g[0][perm_one]).astype(F32)
    cq_rot, cq_nopos = _proj(h, w_cq, qscale(na), n_rows=n_lat, gain=qg, rope_tabs=tabs, emit_nopos=True,
                             name="proj_cq", **kw)
    ck = _proj(h, w_ck, ones(nk), n_rows=n_all, gain=kg, rope_tabs=tabs, name="proj_ck", **kw)
    dq_rot, dq_nopos = _proj(h, w_dq, qscale(nd), n_rows=n_lat, rope_tabs=tabs, emit_nopos=True,
                             name="proj_dq", **kw)
    dk = _proj(h, w_dk, ones(nd), n_rows=n_all, rope_tabs=tabs, name="proj_dk", **kw)
    vv = _proj(h, w_v, ones(nk + nd), n_rows=n_all, name="proj_odd_v", **kw)
    o_c = _dense_attn(cq_rot, cq_nopos, ck, vv, None, n_batch=n_batch, n_q=seq, q_row0=0, n_k=seq, k_row0=0,
                      ctx_len=ctx_len, ctx_row0=n_lat, n_kv=kvh, g=gq, q_col=0, k_col=0, v_col=0,
                      name="attn_c")
    lam_vecs = jnp.stack([lambda_q1[0], lambda_k1[0], lambda_q2[0], lambda_k2[0]]).astype(F32)
    o_d = _diff_attn(dq_rot, dq_nopos, dk, vv, lam_vecs, row2(subln_g[0]).astype(F32), n_batch=n_batch,
                     seq=seq, ctx_len=ctx_len, n_latent=n_lat, n_heads=dh, v_col=nk // (2 * HEAD_DIM),
                     lambda_init=lambda_init)
    w_out = w_out_odd[0].astype(BF16)
    x1, h2, idx, wts = _outproj_ln((o_c, None), (o_d, None), (xt, None), w_out[:na], w_out[na:], g1, sc2,
                                   sh2, row2(ln1_g[1]), row2(ln1_b[1]), w_router_t, router_bias_col,
                                   n_lat=n_lat, seq=seq, n_groups=n_groups, alpha=alpha)
    y = moe(h2, idx, wts, 1)
    x2, _ = _resid_ln(x1, y, g2, row2(ln2_g[1]), row2(ln2_b[1]), None, None, n_rows=n_lat, seq=seq,
                      n_groups=n_groups, alpha=alpha)
    return x2.reshape(n_batch, seq, d)
```

```python
import functools
import math

import numpy as np
import jax
import jax.numpy as jnp
from jax import lax
from jax.experimental import pallas as pl
from jax.experimental.pallas import tpu as pltpu

F32 = jnp.float32
BF16 = jnp.bfloat16

HEAD_DIM = 128
GRID_W = 64
ROPE_THETA = 10000.0
SCALE = HEAD_DIM ** -0.5
NEG_INF = -1e30
A_WINDOW = 128
A_BLOCK = 128
NA_ROWS = 8
NA_COLS = 16
B_TILE_ROWS = 4
B_HEADS_PER_STEP = 8
ROW_PARTS = 2
N_EXPERTS = 16
N_GROUPS = 4
EXPERTS_PER_GROUP = N_EXPERTS // N_GROUPS
LN_EPS = 1e-5
RMS_EPS = 1e-6
LOG2E = math.log2(math.e)
Q_SCALE = SCALE * LOG2E

V7X_VMEM_BYTES = 64 * 1024 * 1024
VMEM_LIMIT = V7X_VMEM_BYTES - 8 * 1024 * 1024
LANES = 128

NT_DIMS = (((1,), (1,)), ((), ()))


def _params(*sem):
    return pltpu.CompilerParams(dimension_semantics=sem, vmem_limit_bytes=VMEM_LIMIT)


def _adaln_kernel(c_ref, w_ref, b_ref, o_ref):
    c = c_ref[...]
    a = (c * (1.0 / (1.0 + jnp.exp(-c)))).astype(BF16)
    o_ref[...] = jnp.dot(a, w_ref[...].astype(BF16), preferred_element_type=F32) + b_ref[...]


def _adaln(c_rows, w_ada, b_ada):
    depth, d, n6 = w_ada.shape
    rows = c_rows.shape[0]
    tn = 1024
    return pl.pallas_call(
        _adaln_kernel,
        out_shape=jax.ShapeDtypeStruct((depth, rows, n6), F32),
        grid=(depth, n6 // tn),
        in_specs=[
            pl.BlockSpec((rows, d), lambda l, j: (0, 0)),
            pl.BlockSpec((None, d, tn), lambda l, j: (l, 0, j)),
            pl.BlockSpec((None, 1, tn), lambda l, j: (l, 0, j)),
        ],
        out_specs=pl.BlockSpec((None, rows, tn), lambda l, j: (l, 0, j)),
        compiler_params=_params("parallel", "parallel"),
        name="adaln",
    )(c_rows, w_ada, b_ada.reshape(depth, 1, n6))


def _two_stream_specs(tm, width, n_lat_tiles):
    return [pl.BlockSpec((tm, width), lambda i: (jnp.minimum(i, n_lat_tiles - 1), 0)),
            pl.BlockSpec((tm, width), lambda i: (jnp.maximum(i - n_lat_tiles, 0), 0))]


def _pick_stream(lat_ref, ctx_ref, n_lat_tiles):
    if ctx_ref is None:
        return lat_ref[...]
    return jnp.where(pl.program_id(0) >= n_lat_tiles, ctx_ref[...], lat_ref[...])


def _modulate_kernel(x_ref, c_ref, sc_ref, sh_ref, o_ref, *, n_lat_tiles):
    x = _pick_stream(x_ref, c_ref, n_lat_tiles)
    o_ref[...] = (x * (1.0 + sc_ref[0]) + sh_ref[0]).astype(o_ref.dtype)


def _modulate(x_lat, x_ctx, sc, sh, *, seq, n_groups):
    d = x_lat.shape[1]
    rows = x_lat.shape[0] + x_ctx.shape[0]
    tm = 512
    n_lat_tiles = x_lat.shape[0] // tm
    grp = lambda i: (jnp.minimum(i // (seq // tm), n_groups - 1), 0, 0)
    return pl.pallas_call(
        functools.partial(_modulate_kernel, n_lat_tiles=n_lat_tiles),
        out_shape=jax.ShapeDtypeStruct((rows, d), BF16),
        grid=(rows // tm,),
        in_specs=_two_stream_specs(tm, d, n_lat_tiles)
        + [pl.BlockSpec((1, 1, d), grp), pl.BlockSpec((1, 1, d), grp)],
        out_specs=pl.BlockSpec((tm, d), lambda i: (i, 0)),
        compiler_params=_params("parallel"),
        name="modulate",
    )(x_lat, x_ctx, sc, sh)


def _cast_kernel(x_ref, o_ref):
    o_ref[...] = x_ref[...].astype(o_ref.dtype)


def _cast_bf16(w):
    shape = w.shape
    w2 = w.reshape(-1, shape[-1])
    tm = min(w2.shape[0], (2 * 1024 * 1024) // shape[-1])
    out = pl.pallas_call(
        _cast_kernel,
        out_shape=jax.ShapeDtypeStruct(w2.shape, BF16),
        grid=(w2.shape[0] // tm,),
        in_specs=[pl.BlockSpec((tm, shape[-1]), lambda i: (i, 0))],
        out_specs=pl.BlockSpec((tm, shape[-1]), lambda i: (i, 0)),
        compiler_params=_params("parallel"),
        name="cast_bf16",
    )(w2)
    return out.reshape(shape)


def _proj_kernel(*refs, tn, norm, rope, emit_nopos):
    it = iter(refs)
    x_ref, w_ref, cs_ref = next(it), next(it), next(it)
    g_ref = next(it) if norm else None
    cos_ref, sin_ref = (next(it), next(it)) if rope else (None, None)
    o_ref = next(it)
    n_ref = next(it) if emit_nopos else None
    tm = x_ref.shape[0]
    rows_per_part = tm // ROW_PARTS
    for part in range(ROW_PARTS):
        rows = slice(part * rows_per_part, (part + 1) * rows_per_part)
        acc = jnp.dot(x_ref[rows, :], w_ref[...], preferred_element_type=F32)
        for hd in range(tn // HEAD_DIM):
            sl = slice(hd * HEAD_DIM, (hd + 1) * HEAD_DIM)
            xh = acc[:, sl]
            if norm:
                xh = xh * lax.rsqrt(jnp.mean(xh * xh, axis=-1, keepdims=True) + RMS_EPS) * g_ref[...]
            cs = cs_ref[:, sl]
            if rope:
                rot = xh * cos_ref[rows, :] + pltpu.roll(xh, HEAD_DIM // 2, 1) * sin_ref[rows, :]
                o_ref[rows, sl] = (rot * cs).astype(o_ref.dtype)
                if emit_nopos:
                    n_ref[rows, sl] = (xh * cs).astype(n_ref.dtype)
            else:
                o_ref[rows, sl] = (xh * cs).astype(o_ref.dtype)


def _proj(x, w, col_scale, *, n_rows, seq, n_latent, tn=None, gain=None, rope_tabs=None,
          emit_nopos=False, name="proj"):
    d = x.shape[1]
    nc = w.shape[1]
    tm = min(1024, seq)
    tn = nc if tn is None else tn
    norm, rope = gain is not None, rope_tabs is not None
    in_specs = [pl.BlockSpec((tm, d), lambda i, j: (i, 0)),
                pl.BlockSpec((d, tn), lambda i, j: (0, j)),
                pl.BlockSpec((1, tn), lambda i, j: (0, j))]
    args = [x, w, col_scale]
    if norm:
        in_specs.append(pl.BlockSpec((1, HEAD_DIM), lambda i, j: (0, 0)))
        args.append(gain)
    if rope:
        per_seq = seq // tm
        tab = lambda i, j: (jnp.where(i < n_latent // tm, i % per_seq, per_seq), 0)
        in_specs += [pl.BlockSpec((tm, HEAD_DIM), tab)] * 2
        args += list(rope_tabs)
    n_out = 2 if emit_nopos else 1
    out_shape = [jax.ShapeDtypeStruct((n_rows, nc), BF16)] * n_out
    out_specs = [pl.BlockSpec((tm, tn), lambda i, j: (i, j))] * n_out
    res = pl.pallas_call(
        functools.partial(_proj_kernel, tn=tn, norm=norm, rope=rope, emit_nopos=emit_nopos),
        out_shape=out_shape,
        grid=(n_rows // tm, nc // tn),
        in_specs=in_specs,
        out_specs=out_specs,
        compiler_params=_params("parallel", "parallel"),
        name=name,
    )(*args)
    return res if emit_nopos else res[0]


def _stack_heads(x, g):
    if g == 1:
        return x
    return jnp.concatenate([x[:, i * HEAD_DIM:(i + 1) * HEAD_DIM] for i in range(g)], axis=0)


def _unstack_store(o_ref, o, g, t, col0=0):
    for i in range(g):
        o_ref[:, col0 + i * HEAD_DIM:col0 + (i + 1) * HEAD_DIM] = o[i * t:(i + 1) * t].astype(o_ref.dtype)


def _sink_column(sink_ref, first_head, g, t):
    cols = [jnp.full((t, 1), sink_ref[first_head + i] * LOG2E, F32) for i in range(g)]
    return cols[0] if g == 1 else jnp.concatenate(cols, axis=0)


def _with_ones(v):
    return jnp.concatenate([v, jnp.ones_like(v)], axis=1)


def _row_max(pieces):
    cols = [p[:, c:c + LANES] for p in pieces for c in range(0, p.shape[1], LANES)]
    return jnp.max(functools.reduce(jnp.maximum, cols), axis=-1, keepdims=True)


def _online_update(s, v_aug, m, acc):
    m_new = jnp.maximum(m, jnp.max(s, axis=-1, keepdims=True))
    p = jnp.exp2(s - m_new)
    acc = jnp.exp2(m - m_new) * acc + jnp.dot(p.astype(BF16), v_aug, preferred_element_type=F32)
    return m_new, acc


def _band_kernel(*refs, nh, g, bq, has_sink, same_qc, shared_bias):
    it = iter(refs)
    sink_ref = next(it) if has_sink else None
    q_ref = next(it)
    qc_ref = q_ref if same_qc else next(it)
    k_refs = [next(it) for _ in range(3)]
    v_refs = [next(it) for _ in range(3)]
    kx_ref, vx_ref, bias_ref, o_ref = next(it), next(it), next(it), next(it)
    hw = g * HEAD_DIM
    for h in range(nh):
        qs = slice(h * hw, (h + 1) * hw)
        ks = slice(h * HEAD_DIM, (h + 1) * HEAD_DIM)
        q = _stack_heads(q_ref[:, qs], g)
        qc = q if same_qc else _stack_heads(qc_ref[:, qs], g)
        bias = bias_ref.at[0 if shared_bias else h]
        s = [lax.dot_general(q, k_refs[j][:, ks], NT_DIMS, preferred_element_type=F32)
             + bias[:, j * bq:(j + 1) * bq] for j in range(3)]
        s.append(lax.dot_general(qc, kx_ref[:, ks], NT_DIMS, preferred_element_type=F32))
        vs = [r[:, ks] for r in v_refs] + [vx_ref[:, ks]]
        m = _row_max(s)
        if has_sink:
            sink = _sink_column(sink_ref, (pl.program_id(0) * nh + h) * g, g, bq)
            m = jnp.maximum(m, sink)
        o = functools.reduce(jnp.add, [
            jnp.dot(jnp.exp2(sj - m).astype(BF16), _with_ones(vj), preferred_element_type=F32)
            for sj, vj in zip(s, vs)])
        den = o[:, HEAD_DIM:]
        if has_sink:
            den = den + jnp.exp2(sink - m)
        _unstack_store(o_ref, o[:, :HEAD_DIM] / den, g, bq, col0=h * hw)


def _band_attn(q, qc, k, v, bias, sink, *, n_batch, seq, ctx_len, n_latent, n_kv, nh, g, bq,
               q_col, k_col, v_col, name):
    nblk = seq // bq
    ctx_blk = n_latent // ctx_len
    shared_bias = bias.shape[0] == 1
    has_sink, same_qc = sink is not None, qc is None
    qw, kw = nh * g * HEAD_DIM, nh * HEAD_DIM

    def q_map(hg, i, b):
        return (b * nblk + i, q_col + hg)

    def kv_map(col, off):
        return lambda hg, i, b: (b * nblk + jnp.clip(i + off, 0, nblk - 1), col + hg)

    def ctx_map(col):
        return lambda hg, i, b: (ctx_blk + b, col + hg)

    def bias_map(hg, i, b):
        case = jnp.where(i == 0, 0, jnp.where(i == nblk - 1, 2, 1))
        return (0 if shared_bias else hg, case, 0, 0)

    in_specs, args = [], []
    if has_sink:
        in_specs.append(pl.BlockSpec(memory_space=pltpu.SMEM))
        args.append(sink)
    in_specs.append(pl.BlockSpec((bq, qw), q_map))
    args.append(q)
    if not same_qc:
        in_specs.append(pl.BlockSpec((bq, qw), q_map))
        args.append(qc)
    in_specs += [pl.BlockSpec((bq, kw), kv_map(k_col, off)) for off in (-1, 0, 1)]
    args += [k] * 3
    in_specs += [pl.BlockSpec((bq, kw), kv_map(v_col, off)) for off in (-1, 0, 1)]
    args += [v] * 3
    in_specs += [pl.BlockSpec((ctx_len, kw), ctx_map(k_col)),
                 pl.BlockSpec((ctx_len, kw), ctx_map(v_col)),
                 pl.BlockSpec((1 if shared_bias else nh, None, g * bq, 3 * bq), bias_map)]
    args += [k, v, bias]
    return pl.pallas_call(
        functools.partial(_band_kernel, nh=nh, g=g, bq=bq, has_sink=has_sink, same_qc=same_qc,
                          shared_bias=shared_bias),
        out_shape=jax.ShapeDtypeStruct((n_latent, n_kv * g * HEAD_DIM), BF16),
        grid=(n_kv // nh, nblk, n_batch),
        in_specs=in_specs,
        out_specs=pl.BlockSpec((bq, qw), lambda hg, i, b: (b * nblk + i, hg)),
        compiler_params=_params("parallel", "parallel", "parallel"),
        name=name,
    )(*args)


def _dense_kernel(*refs, g, tq, tk, n_k, has_ctx, has_sink):
    it = iter(refs)
    sink_ref = next(it) if has_sink else None
    q_ref = next(it)
    qc_ref = next(it) if has_ctx else None
    k_ref, v_ref = next(it), next(it)
    kx_ref, vx_ref = (next(it), next(it)) if has_ctx else (None, None)
    o_ref = next(it)
    q = _stack_heads(q_ref[...], g)
    rows = g * tq
    m = jnp.full((rows, 1), NEG_INF, F32)
    acc = jnp.zeros((rows, 2 * HEAD_DIM), F32)
    for c in range(n_k // tk):
        s = lax.dot_general(q, k_ref[c * tk:(c + 1) * tk, :], NT_DIMS, preferred_element_type=F32)
        m, acc = _online_update(s, _with_ones(v_ref[c * tk:(c + 1) * tk, :]), m, acc)
    if has_ctx:
        qc = _stack_heads(qc_ref[...], g)
        s = lax.dot_general(qc, kx_ref[...], NT_DIMS, preferred_element_type=F32)
        m, acc = _online_update(s, _with_ones(vx_ref[...]), m, acc)
    num, den = acc[:, :HEAD_DIM], acc[:, HEAD_DIM:]
    if has_sink:
        sink = _sink_column(sink_ref, pl.program_id(1) * g, g, tq)
        m_new = jnp.maximum(m, sink)
        alpha = jnp.exp2(m - m_new)
        num = alpha * num
        den = alpha * den + jnp.exp2(sink - m_new)
    _unstack_store(o_ref, num / den, g, tq)


def _dense_attn(q, qc, k, v, sink, *, n_batch, n_q, q_row0, n_k, k_row0, ctx_len, ctx_row0,
                n_kv, g, q_col, k_col, v_col, name):
    tq = min(256, n_q)
    tk = min(512, n_k)
    nq_blk = n_q // tq
    has_ctx, has_sink = qc is not None, sink is not None

    def q_map(b, h, i):
        return (q_row0 // tq + b * nq_blk + i, q_col + h)

    in_specs, args = [], []
    if has_sink:
        in_specs.append(pl.BlockSpec(memory_space=pltpu.SMEM))
        args.append(sink)
    in_specs.append(pl.BlockSpec((tq, g * HEAD_DIM), q_map))
    args.append(q)
    if has_ctx:
        in_specs.append(pl.BlockSpec((tq, g * HEAD_DIM), q_map))
        args.append(qc)
    in_specs += [pl.BlockSpec((n_k, HEAD_DIM), lambda b, h, i: (k_row0 // n_k + b, k_col + h)),
                 pl.BlockSpec((n_k, HEAD_DIM), lambda b, h, i: (k_row0 // n_k + b, v_col + h))]
    args += [k, v]
    if has_ctx:
        in_specs += [pl.BlockSpec((ctx_len, HEAD_DIM), lambda b, h, i: (ctx_row0 // ctx_len + b, k_col + h)),
                     pl.BlockSpec((ctx_len, HEAD_DIM), lambda b, h, i: (ctx_row0 // ctx_len + b, v_col + h))]
        args += [k, v]
    return pl.pallas_call(
        functools.partial(_dense_kernel, g=g, tq=tq, tk=tk, n_k=n_k, has_ctx=has_ctx,
                          has_sink=has_sink),
        out_shape=jax.ShapeDtypeStruct((n_batch * n_q, n_kv * g * HEAD_DIM), BF16),
        grid=(n_batch, n_kv, nq_blk),
        in_specs=in_specs,
        out_specs=pl.BlockSpec((tq, g * HEAD_DIM), lambda b, h, i: (b * nq_blk + i, h)),
        compiler_params=_params("parallel", "parallel", "parallel"),
        name=name,
    )(*args)


def _diff_update(s, v, m, l, acc):
    m_new = jnp.maximum(m, jnp.max(s, axis=-1, keepdims=True))
    alpha = jnp.exp2(m - m_new)
    p = jnp.exp2(s - m_new)
    l = alpha * l + jnp.sum(p, axis=-1, keepdims=True)
    acc = alpha * acc + jnp.dot(p.astype(BF16), v, preferred_element_type=F32)
    return m_new, l, acc


def _diff_kernel(lam_ref, g_ref, q_ref, qc_ref, k_ref, v_ref, kx_ref, vx_ref, o_ref, *,
                 tq, tk, n_k, lambda_init):
    dv = 2 * HEAD_DIM
    lam = (jnp.exp(jnp.sum(lam_ref[0:1, :] * lam_ref[1:2, :], axis=1, keepdims=True))
           - jnp.exp(jnp.sum(lam_ref[2:3, :] * lam_ref[3:4, :], axis=1, keepdims=True)) + lambda_init)
    outs = []
    for t in range(2):
        sl = slice(t * HEAD_DIM, (t + 1) * HEAD_DIM)
        q = q_ref[:, sl]
        m = jnp.full((tq, 1), NEG_INF, F32)
        l = jnp.zeros((tq, 1), F32)
        acc = jnp.zeros((tq, dv), F32)
        for c in range(n_k // tk):
            rows = slice(c * tk, (c + 1) * tk)
            s = lax.dot_general(q, k_ref[rows, sl], NT_DIMS, preferred_element_type=F32)
            m, l, acc = _diff_update(s, v_ref[rows, :], m, l, acc)
        s = lax.dot_general(qc_ref[:, sl], kx_ref[:, sl], NT_DIMS, preferred_element_type=F32)
        m, l, acc = _diff_update(s, vx_ref[...], m, l, acc)
        outs.append(acc / l)
    o = outs[0] - lam * outs[1]
    o = o * lax.rsqrt(jnp.mean(o * o, axis=-1, keepdims=True) + RMS_EPS) * g_ref[...]
    o_ref[...] = (o * (1.0 - lambda_init)).astype(o_ref.dtype)


def _diff_attn(q, qc, k, v, lam_vecs, subln_g, *, n_batch, seq, ctx_len, n_latent, n_heads,
               v_col, lambda_init):
    tq = min(512, seq)
    tk = min(512, seq)
    dv = 2 * HEAD_DIM
    nq_blk = seq // tq
    ctx_blk = n_latent // ctx_len
    q_map = lambda b, h, i: (b * nq_blk + i, h)
    return pl.pallas_call(
        functools.partial(_diff_kernel, tq=tq, tk=tk, n_k=seq, lambda_init=lambda_init),
        out_shape=jax.ShapeDtypeStruct((n_latent, n_heads * dv), BF16),
        grid=(n_batch, n_heads, nq_blk),
        in_specs=[
            pl.BlockSpec((4, HEAD_DIM), lambda b, h, i: (0, 0)),
            pl.BlockSpec((1, dv), lambda b, h, i: (0, 0)),
            pl.BlockSpec((tq, dv), q_map),
            pl.BlockSpec((tq, dv), q_map),
            pl.BlockSpec((seq, dv), lambda b, h, i: (b, h)),
            pl.BlockSpec((seq, dv), lambda b, h, i: (b, v_col + h)),
            pl.BlockSpec((ctx_len, dv), lambda b, h, i: (ctx_blk + b, h)),
            pl.BlockSpec((ctx_len, dv), lambda b, h, i: (ctx_blk + b, v_col + h)),
        ],
        out_specs=pl.BlockSpec((tq, dv), q_map),
        compiler_params=_params("parallel", "parallel", "parallel"),
        name="diff_attn",
    )(lam_vecs, subln_g, q, qc, k, v, k, v)


def _layer_norm(z, g, b):
    mu = jnp.mean(z, axis=-1, keepdims=True)
    zc = z - mu
    var = jnp.mean(zc * zc, axis=-1, keepdims=True)
    return zc * lax.rsqrt(var + LN_EPS) * g + b


def _split_bf16(x):
    hi = x.astype(BF16)
    return hi, (x - hi.astype(F32)).astype(BF16)


def _first_argmax(vals, idx, width):
    m = jnp.max(vals, axis=0, keepdims=True)
    first = jnp.min(jnp.where(vals == m, idx, float(width)), axis=0, keepdims=True)
    return m, first


def _route(logits, bias):
    e, t = logits.shape
    scores = 1.0 / (1.0 + jnp.exp(-logits))
    biased = scores + bias
    row_i = lax.broadcasted_iota(jnp.int32, (e, t), 0)
    grp = lax.shift_right_logical(row_i, int(math.log2(EXPERTS_PER_GROUP)))
    row = row_i.astype(F32)
    neg = -jnp.inf
    best_score, best = None, None
    for gi in range(N_GROUPS):
        vg = jnp.where(grp == gi, biased, neg)
        m1, i1 = _first_argmax(vg, row, e)
        m2 = jnp.max(jnp.where(row == i1, neg, vg), axis=0, keepdims=True)
        gs = m1 + m2
        if gi == 0:
            best_score, best = gs, jnp.zeros((1, t), jnp.int32)
        else:
            upd = gs > best_score
            best = jnp.where(upd, gi, best)
            best_score = jnp.where(upd, gs, best_score)
    masked = jnp.where(grp == best, biased, neg)
    _, e1 = _first_argmax(masked, row, e)
    _, e2 = _first_argmax(jnp.where(row == e1, neg, masked), row, e)
    w1 = jnp.sum(jnp.where(row == e1, scores, 0.0), axis=0, keepdims=True)
    w2 = jnp.sum(jnp.where(row == e2, scores, 0.0), axis=0, keepdims=True)
    den = w1 + w2
    two = lax.broadcasted_iota(jnp.int32, (2, t), 0)
    return jnp.where(two == 0, e1, e2).astype(jnp.int32), jnp.where(two == 0, w1 / den, w2 / den)


def _outproj_kernel(*refs, alpha, two_streams, n_lat_tiles):
    it = iter(refs)
    xa_ref, xa_ctx = next(it), (next(it) if two_streams else None)
    xb_ref, xb_ctx = next(it), (next(it) if two_streams else None)
    xres_ref, xres_ctx = next(it), (next(it) if two_streams else None)
    (wa_ref, wb_ref, g1_ref, sc2_ref, sh2_ref, lng_ref, lnb_ref, wrt_ref, rb_ref,
     x1_ref, h2_ref, idx_ref, wts_ref) = it
    w_hi, w_lo = _split_bf16(wrt_ref[...])
    w_cat = jnp.concatenate([w_hi, w_lo], axis=0)
    tm = x1_ref.shape[0]
    rows_per_part = tm // ROW_PARTS
    for part in range(ROW_PARTS):
        rows = slice(part * rows_per_part, (part + 1) * rows_per_part)
        pick = lambda lat, ctx: _pick_stream(lat.at[rows, :], None if ctx is None else ctx.at[rows, :], n_lat_tiles)
        y = (jnp.dot(pick(xa_ref, xa_ctx), wa_ref[...], preferred_element_type=F32)
             + jnp.dot(pick(xb_ref, xb_ctx), wb_ref[...], preferred_element_type=F32))
        x1 = _layer_norm(alpha * pick(xres_ref, xres_ctx) + g1_ref[0] * y, lng_ref[...], lnb_ref[...])
        x1_ref[rows, :] = x1
        h2 = x1 * (1.0 + sc2_ref[0]) + sh2_ref[0]
        h2_ref[rows, :] = h2
        h_hi, h_lo = _split_bf16(h2)
        both = lax.dot_general(w_cat, h_hi, NT_DIMS, preferred_element_type=F32)
        logits = (both[:N_EXPERTS] + both[N_EXPERTS:]
                  + lax.dot_general(w_hi, h_lo, NT_DIMS, preferred_element_type=F32))
        idx, wts = _route(logits, rb_ref[...])
        idx_ref[:, rows] = idx
        wts_ref[:, rows] = wts


def _outproj_ln(xa, xb, xres, wa, wb, g1, sc2, sh2, ln_g, ln_b, w_router_t, router_bias, *,
                n_lat, seq, n_groups, alpha):
    two_streams = xa[1] is not None
    d = xres[0].shape[1]
    ka, kb = xa[0].shape[1], xb[0].shape[1]
    tm = 256 * ROW_PARTS
    n_lat_tiles = n_lat // tm
    n_rows = n_lat + (xres[1].shape[0] if two_streams else 0)
    row = lambda i: (i, 0)
    col = lambda i: (0, i)
    const = lambda i: (0, 0)
    grp = lambda i: (jnp.minimum(i // (seq // tm), n_groups - 1), 0, 0)
    in_specs, args = [], []
    for pair, width in ((xa, ka), (xb, kb), (xres, d)):
        if two_streams:
            in_specs += _two_stream_specs(tm, width, n_lat_tiles)
            args += list(pair)
        else:
            in_specs.append(pl.BlockSpec((tm, width), row))
            args.append(pair[0])
    in_specs += [pl.BlockSpec((ka, d), const), pl.BlockSpec((kb, d), const),
                 pl.BlockSpec((1, 1, d), grp), pl.BlockSpec((1, 1, d), grp), pl.BlockSpec((1, 1, d), grp),
                 pl.BlockSpec((1, d), const), pl.BlockSpec((1, d), const),
                 pl.BlockSpec((N_EXPERTS, d), const), pl.BlockSpec((N_EXPERTS, 1), const)]
    args += [wa, wb, g1, sc2, sh2, ln_g, ln_b, w_router_t, router_bias]
    return pl.pallas_call(
        functools.partial(_outproj_kernel, alpha=alpha, two_streams=two_streams, n_lat_tiles=n_lat_tiles),
        out_shape=[jax.ShapeDtypeStruct((n_rows, d), F32), jax.ShapeDtypeStruct((n_rows, d), F32),
                   jax.ShapeDtypeStruct((2, n_rows), jnp.int32), jax.ShapeDtypeStruct((2, n_rows), F32)],
        grid=(n_rows // tm,),
        in_specs=in_specs,
        out_specs=[pl.BlockSpec((tm, d), row), pl.BlockSpec((tm, d), row),
                   pl.BlockSpec((2, tm), col), pl.BlockSpec((2, tm), col)],
        compiler_params=_params("parallel"),
        name="outproj_ln",
    )(*args)


def _pack_bf16_pairs(x):
    n = x.shape[1] // 2
    lo = pltpu.bitcast(x[:, :n].astype(BF16).astype(F32), jnp.uint32)
    hi = pltpu.bitcast(x[:, n:].astype(BF16).astype(F32), jnp.uint32)
    return jnp.bitwise_or(lax.shift_right_logical(lo, jnp.uint32(16)), hi)


def _unpack_bf16_pairs(w):
    lo = pltpu.bitcast(lax.shift_left(w, jnp.uint32(16)), F32)
    hi = pltpu.bitcast(jnp.bitwise_and(w, jnp.uint32(0xFFFF0000)), F32)
    return jnp.concatenate([lo, hi], axis=1)


def _moe_kernel(te_ref, nv_ref, src_ref, dst_ref, h_hbm, gw_ref, wg_ref, wu_ref, wd_ref, y_hbm,
                xbuf, obuf, gsem, ssem, *, tm, ff_chunks):
    i = pl.program_id(0)
    n = pl.num_programs(0)
    nv = nv_ref[0]

    def start_gather(tile, s):
        base = tile * (tm // 2)
        for r in range(tm):
            word = src_ref[base + r // 2]
            tok = jnp.bitwise_and(word, 0xFFFF) if r % 2 == 0 else lax.shift_right_logical(word, 16)
            pltpu.make_async_copy(h_hbm.at[pl.ds(tok, 1)], xbuf.at[s, pl.ds(r, 1)],
                                  gsem.at[s]).start(priority=r % 2)

    def wait_gather(s):
        pltpu.make_async_copy(h_hbm.at[pl.ds(0, tm)], xbuf.at[s], gsem.at[s]).wait()

    def start_scatter(tile, s):
        base = tile * tm
        for r in range(tm):
            pltpu.make_async_copy(obuf.at[s, pl.ds(r, 1)], y_hbm.at[pl.ds(dst_ref[base + r], 1)],
                                  ssem.at[s]).start(priority=r % 2)

    def wait_scatter(s):
        pltpu.make_async_copy(obuf.at[s], y_hbm.at[pl.ds(0, tm)], ssem.at[s]).wait()

    @pl.when(i == 0)
    def _():
        start_gather(0, 0)
        obuf[...] = jnp.zeros_like(obuf)
        pad0 = y_hbm.shape[0] - 2 * tm
        fills = [pltpu.make_async_copy(obuf.at[s], y_hbm.at[pl.ds(pad0 + s * tm, tm)], ssem.at[s])
                 for s in range(2)]
        for f in fills:
            f.start()
        for f in fills:
            f.wait()

    def step(slot):
        wait_gather(slot)

        @pl.when(i >= 2)
        def _():
            wait_scatter(slot)

        start_gather(jnp.minimum(i + 1, n - 1), 1 - slot)
        x = xbuf[slot].astype(BF16)
        fc = wg_ref.shape[1] // ff_chunks
        y = None
        for c in range(ff_chunks):
            cs = slice(c * fc, (c + 1) * fc)
            gate = jnp.dot(x, wg_ref[:, cs], preferred_element_type=F32)
            up = jnp.dot(x, wu_ref[:, cs], preferred_element_type=F32)
            act = (gate * (1.0 / (1.0 + jnp.exp(-gate))) * up).astype(BF16)
            part = jnp.dot(act, wd_ref[cs, :], preferred_element_type=F32)
            y = part if y is None else y + part
        obuf[slot] = _pack_bf16_pairs(y * gw_ref[...])
        start_scatter(i, slot)

    for slot in range(2):
        @pl.when(jnp.logical_and(i < nv, jnp.bitwise_and(i, 1) == slot))
        def _():
            step(slot)

    @pl.when(i == n - 1)
    def _():
        wait_gather(jnp.bitwise_and(nv, 1))

        @pl.when(nv >= 1)
        def _():
            wait_scatter(jnp.bitwise_and(nv - 1, 1))

        @pl.when(nv >= 2)
        def _():
            wait_scatter(jnp.bitwise_and(nv, 1))


def _moe_ffn(h2, src_packed, dst_row, gw, tile_expert, n_valid, w_gate, w_up, w_down, *, layer, tm):
    t, d = h2.shape
    ff = w_gate.shape[3]
    n_tiles = dst_row.shape[0] // tm
    return pl.pallas_call(
        functools.partial(_moe_kernel, tm=tm, ff_chunks=2),
        out_shape=jax.ShapeDtypeStruct((2 * t + 2 * tm, d // 2), jnp.uint32),
        grid_spec=pltpu.PrefetchScalarGridSpec(
            num_scalar_prefetch=4,
            grid=(n_tiles,),
            in_specs=[pl.BlockSpec(memory_space=pl.ANY),
                      pl.BlockSpec((tm, 1), lambda i, te, nv, src, dst: (i, 0)),
                      pl.BlockSpec((None, None, d, ff), lambda i, te, nv, src, dst: (layer, te[i], 0, 0)),
                      pl.BlockSpec((None, None, d, ff), lambda i, te, nv, src, dst: (layer, te[i], 0, 0)),
                      pl.BlockSpec((None, None, ff, d), lambda i, te, nv, src, dst: (layer, te[i], 0, 0))],
            out_specs=pl.BlockSpec(memory_space=pl.ANY),
            scratch_shapes=[pltpu.VMEM((2, tm, d), F32), pltpu.VMEM((2, tm, d // 2), jnp.uint32),
                            pltpu.SemaphoreType.DMA((2,)), pltpu.SemaphoreType.DMA((2,))],
        ),
        compiler_params=_params("arbitrary"),
        name="moe_ffn",
    )(tile_expert, n_valid, src_packed, dst_row, h2, gw, w_gate, w_up, w_down)


def _take(x, idx):
    return x.at[idx].get(mode="promise_in_bounds")


def _dispatch(idx, wts, *, tm):
    t = idx.shape[1]
    n_pairs = 2 * t
    n_tiles = n_pairs // tm + N_EXPERTS
    e_flat = idx.reshape(n_pairs)
    w_flat = wts.reshape(n_pairs)
    _, order, w_sorted = lax.sort((e_flat, jnp.arange(n_pairs, dtype=jnp.int32), w_flat), num_keys=1)
    experts = jnp.arange(N_EXPERTS, dtype=jnp.int32)
    counts = jnp.sum(e_flat[None, :] == experts[:, None], axis=1).astype(jnp.int32)
    start = jnp.cumsum(counts) - counts
    padded = ((counts + tm - 1) // tm) * tm
    pend = jnp.cumsum(padded)
    pstart = pend - padded
    tile_pos = jnp.arange(n_tiles, dtype=jnp.int32) * tm
    tile_expert = jnp.minimum(jnp.sum(tile_pos[:, None] >= pend[None, :], axis=1), N_EXPERTS - 1).astype(jnp.int32)
    local = (tile_pos - pstart[tile_expert])[:, None] + jnp.arange(tm, dtype=jnp.int32)[None, :]
    valid = local < counts[tile_expert][:, None]
    sorted_pos = jnp.clip(start[tile_expert][:, None] + local, 0, n_pairs - 1)
    src_pair = _take(order, sorted_pos)
    src_tok = jnp.where(valid, jnp.where(src_pair >= t, src_pair - t, src_pair), 0)
    pad_row = n_pairs + (tile_pos[:, None] + jnp.arange(tm, dtype=jnp.int32)[None, :]) % (2 * tm)
    dst_row = jnp.where(valid, src_pair, pad_row)
    gw = jnp.where(valid, _take(w_sorted, sorted_pos), 0.0).astype(F32)
    src_packed = jnp.bitwise_or(src_tok[:, 0::2], jnp.left_shift(src_tok[:, 1::2], 16))
    n_valid = (pend[-1] // tm).astype(jnp.int32).reshape(1)
    return src_packed.reshape(-1), dst_row.reshape(-1), gw.reshape(-1, 1), tile_expert, n_valid


def _resid_ln_kernel(*refs, alpha, emit_h):
    it = iter(refs)
    x_ref, f0_ref, f1_ref, g_ref, lng_ref, lnb_ref = (next(it) for _ in range(6))
    sc_ref, sh_ref = (next(it), next(it)) if emit_h else (None, None)
    o_ref = next(it)
    f = _unpack_bf16_pairs(f0_ref[...]) + _unpack_bf16_pairs(f1_ref[...])
    x2 = _layer_norm(alpha * x_ref[...] + g_ref[0] * f, lng_ref[...], lnb_ref[...])
    o_ref[...] = x2
    if emit_h:
        h_ref = next(it)
        h_ref[...] = (x2 * (1.0 + sc_ref[0]) + sh_ref[0]).astype(h_ref.dtype)


def _resid_ln(x, y, gate, ln_g, ln_b, next_sc, next_sh, *, n_rows, seq, n_groups, alpha):
    d = x.shape[1]
    tm = 512
    emit_h = next_sc is not None
    row = lambda i: (i, 0)
    const = lambda i: (0, 0)
    grp = lambda i: (jnp.minimum(i // (seq // tm), n_groups - 1), 0, 0)
    in_specs = [pl.BlockSpec((tm, d), row), pl.BlockSpec((tm, d // 2), row),
                pl.BlockSpec((tm, d // 2), lambda i: (n_rows // tm + i, 0)), pl.BlockSpec((1, 1, d), grp),
                pl.BlockSpec((1, d), const), pl.BlockSpec((1, d), const)]
    args = [x, y, y, gate, ln_g, ln_b]
    out_shape = [jax.ShapeDtypeStruct((n_rows, d), F32)]
    out_specs = [pl.BlockSpec((tm, d), row)]
    if emit_h:
        in_specs += [pl.BlockSpec((1, 1, d), grp)] * 2
        args += [next_sc, next_sh]
        out_shape.append(jax.ShapeDtypeStruct((n_rows, d), BF16))
        out_specs.append(pl.BlockSpec((tm, d), row))
    res = pl.pallas_call(
        functools.partial(_resid_ln_kernel, alpha=alpha, emit_h=emit_h),
        out_shape=out_shape,
        grid=(n_rows // tm,),
        in_specs=in_specs,
        out_specs=out_specs,
        compiler_params=_params("parallel"),
        name="resid_ln",
    )(*args)
    return res if emit_h else (res[0], None)


def _rope_tables(seq, tm):
    t = jnp.arange(seq, dtype=jnp.int32)
    n_freq = HEAD_DIM // 4
    inv = ROPE_THETA ** (-jnp.arange(n_freq, dtype=F32) / n_freq)
    row = (t // GRID_W).astype(F32)
    col = (t % GRID_W).astype(F32)
    ang = jnp.concatenate([row[:, None] * inv[None], col[:, None] * inv[None]], -1)
    cos, sin = jnp.cos(ang), jnp.sin(ang)
    cos2 = jnp.concatenate([cos, cos], -1)
    sin2 = jnp.concatenate([-sin, sin], -1)
    cos2 = jnp.concatenate([cos2, jnp.ones((tm, HEAD_DIM), F32)], 0)
    sin2 = jnp.concatenate([sin2, jnp.zeros((tm, HEAD_DIM), F32)], 0)
    return cos2, sin2


def _window_bias(seq, g):
    bq = A_BLOCK
    nblk = seq // bq
    rel = np.arange(3 * bq)[None, :] - np.arange(bq)[:, None]
    band = (rel >= bq - A_WINDOW) & (rel <= bq + A_WINDOW)
    tabs = []
    for blk in (0, min(1, nblk - 1), nblk - 1):
        kpos = blk * bq - bq + np.arange(3 * bq)
        ok = band & ((kpos >= 0) & (kpos < seq))[None, :]
        tabs.append(np.tile(np.where(ok, 0.0, NEG_INF).astype(np.float32), (g, 1)))
    return jnp.asarray(np.stack(tabs)[None])


def _neighbourhood_bias(rpb, seq):
    n_heads = rpb.shape[0]
    rows = seq // GRID_W
    kh, kw = min(NA_ROWS, rows), NA_COLS
    tr = B_TILE_ROWS
    nblk = rows // tr
    col = np.arange(GRID_W)
    col_start = np.clip(col - kw // 2, 0, GRID_W - kw)
    col_ok = (col[None, :] >= col_start[:, None]) & (col[None, :] < col_start[:, None] + kw)
    assert np.all(np.abs(col[None, :] - col[:, None])[col_ok] <= NA_COLS - 1)
    masks = []
    for blk in (0, min(1, nblk - 1), nblk - 1):
        r = blk * tr + np.arange(tr)
        r0 = np.clip(r - kh // 2, 0, rows - kh)
        krow = (blk - 1) * tr + np.arange(3 * tr)
        row_ok = (krow[None, :] >= r0[:, None]) & (krow[None, :] < r0[:, None] + kh)
        ok = row_ok[:, None, :, None] & col_ok[None, :, None, :]
        masks.append(ok.reshape(tr * GRID_W, 3 * tr * GRID_W))
    ok = jnp.asarray(np.stack(masks))
    pad = GRID_W - NA_COLS
    rp = jnp.pad(rpb.astype(F32) * LOG2E, ((0, 0), (0, 0), (pad, pad)))
    cexp = jnp.stack([rp[:, :, GRID_W - 1 - cq:2 * GRID_W - 1 - cq] for cq in range(GRID_W)], axis=2)
    off = NA_ROWS - 1 - tr
    assert off - (tr - 1) >= 0 and off + 3 * tr - 1 <= 2 * NA_ROWS - 2
    t5 = jnp.stack([cexp[:, off - rq:off - rq + 3 * tr] for rq in range(tr)], axis=1)
    tab = t5.transpose(0, 1, 3, 2, 4).reshape(n_heads, tr * GRID_W, 3 * tr * GRID_W)
    return jnp.where(ok[None], tab[:, None], NEG_INF)


def _deinterleave_cols(n_heads):
    one = np.concatenate([np.arange(0, HEAD_DIM, 2), np.arange(1, HEAD_DIM, 2)])
    return np.concatenate([h * HEAD_DIM + one for h in range(n_heads)])


def kernel(x, c, ctx, c_ctx, w_ada, b_ada, ln1_g, ln1_b, ln2_g, ln2_b, w_in_even, w_out_even, sink_logits, na_rpb, w_in_odd, w_out_odd, q_norm_g, k_norm_g, lambda_q1, lambda_k1, lambda_q2, lambda_k2, subln_g, w_router, router_bias, w_exp_gate, w_exp_up, w_exp_down):
    n_batch, seq, d = x.shape
    ctx_len = ctx.shape[1]
    depth = w_ada.shape[0]
    assert depth == 2, "one even (A||B) layer followed by one odd (C||D) layer"
    n_lat = n_batch * seq
    n_ctx = n_batch * ctx_len
    n_all = n_lat + n_ctx
    n_groups = n_batch + 1
    alpha = float((2 * depth) ** 0.25)
    heads = d // HEAD_DIM
    qh = heads // 2
    kvh = qh // 4
    gq = qh // kvh
    bh = heads - qh
    dh = (heads - qh) // 2
    moe_tm = 512
    proj_tm = min(1024, seq)
    assert seq % GRID_W == 0 and (seq // GRID_W) % B_TILE_ROWS == 0 and seq // GRID_W >= NA_ROWS
    assert n_ctx % proj_tm == 0 and n_lat % ctx_len == 0 and (2 * n_all) % moe_tm == 0
    assert ctx_len % LANES == 0 and bh % B_HEADS_PER_STEP == 0
    assert n_all <= 1 << 16, "token ids are packed as 16-bit halves for the MoE row gather"

    pad_rows = -(-n_groups // 16) * 16
    c_rows = jnp.concatenate([c, c_ctx[None], jnp.zeros((pad_rows - n_groups, d), F32)], 0)
    mod = _adaln(c_rows, w_ada, b_ada)[:, :n_groups]
    mod = mod.reshape(depth, n_groups, 6, 1, d)
    mods = [[mod[l, :, j] for j in range(6)] for l in range(depth)]

    cos2, sin2 = _rope_tables(seq, proj_tm)
    tabs = (cos2, sin2)
    x_lat, x_ctx = x.reshape(n_lat, d), ctx.reshape(n_ctx, d)
    row2 = lambda v: v.reshape(1, -1)
    ones = lambda n: jnp.ones((1, n), F32)
    qscale = lambda n: jnp.full((1, n), Q_SCALE, F32)
    w_router_t = w_router.T.astype(F32)
    router_bias_col = router_bias.reshape(-1, 1).astype(F32)
    wg_all, wu_all, wd_all = (_cast_bf16(w) for w in (w_exp_gate, w_exp_up, w_exp_down))

    def moe(h2, idx, wts, layer):
        src_packed, dst_row, gw, tile_expert, n_valid = _dispatch(idx, wts, tm=moe_tm)
        return _moe_ffn(h2, src_packed, dst_row, gw, tile_expert, n_valid, wg_all, wu_all, wd_all,
                        layer=layer, tm=moe_tm)

    sh1, sc1, g1, sh2, sc2, g2 = mods[0]
    h = _modulate(x_lat, x_ctx, sc1, sh1, seq=seq, n_groups=n_groups)
    w_in = w_in_even[0]
    na, nk, nb = qh * HEAD_DIM, kvh * HEAD_DIM, bh * HEAD_DIM
    perm_q, perm_k = _deinterleave_cols(qh), _deinterleave_cols(kvh)
    w_aq = w_in[:, :na][:, perm_q].astype(BF16)
    w_ak = w_in[:, na:na + nk][:, perm_k].astype(BF16)
    w_rest = jnp.concatenate([w_in[:, na + 2 * nk:], w_in[:, na + nk:na + 2 * nk]], 1).astype(BF16)
    rest_scale = jnp.concatenate([qscale(nb), ones(2 * nb + nk)], 1)
    kw = dict(seq=seq, n_latent=n_lat)
    aq_rot, aq_nopos = _proj(h, w_aq, qscale(na), n_rows=n_all, rope_tabs=tabs, emit_nopos=True,
                             name="proj_aq", **kw)
    ak = _proj(h, w_ak, ones(nk), n_rows=n_all, rope_tabs=tabs, name="proj_ak", **kw)
    rest = _proj(h, w_rest, rest_scale, n_rows=n_all, tn=w_rest.shape[1] // 2, name="proj_even_rest", **kw)
    sink = sink_logits[0].astype(F32)
    akw = dict(n_batch=n_batch, seq=seq, ctx_len=ctx_len, n_latent=n_lat)
    o_a = _band_attn(aq_rot, aq_nopos, ak, rest, _window_bias(seq, gq), sink, n_kv=kvh, nh=kvh, g=gq,
                     bq=A_BLOCK, q_col=0, k_col=0, v_col=3 * nb // nk, name="attn_a", **akw)
    bw = B_HEADS_PER_STEP * HEAD_DIM
    o_b = _band_attn(rest, None, rest, rest, _neighbourhood_bias(na_rpb[0], seq), None, n_kv=bh,
                     nh=B_HEADS_PER_STEP, g=1, bq=B_TILE_ROWS * GRID_W, q_col=0, k_col=nb // bw,
                     v_col=2 * nb // bw, name="attn_b", **akw)
    ckw = dict(n_batch=n_batch, n_q=ctx_len, q_row0=n_lat, n_k=ctx_len, k_row0=n_lat, ctx_len=ctx_len,
               ctx_row0=n_lat)
    oc_a = _dense_attn(aq_rot, None, ak, rest, sink, n_kv=kvh, g=gq, q_col=0, k_col=0,
                       v_col=3 * nb // HEAD_DIM, name="ctx_attn_a", **ckw)
    oc_b = _dense_attn(rest, None, rest, rest, None, n_kv=bh, g=1, q_col=0, k_col=bh, v_col=2 * bh,
                       name="ctx_attn_b", **ckw)
    w_out = w_out_even[0].astype(BF16)
    x1, h2, idx, wts = _outproj_ln((o_a, oc_a), (o_b, oc_b), (x_lat, x_ctx), w_out[:na], w_out[na:], g1, sc2,
                                   sh2, row2(ln1_g[0]), row2(ln1_b[0]), w_router_t, router_bias_col,
                                   n_lat=n_lat, seq=seq, n_groups=n_groups, alpha=alpha)
    y = moe(h2, idx, wts, 0)
    nsh1, nsc1 = mods[1][0], mods[1][1]
    xt, h = _resid_ln(x1, y, g2, row2(ln2_g[0]), row2(ln2_b[0]), nsc1, nsh1, n_rows=n_all, seq=seq,
                      n_groups=n_groups, alpha=alpha)

    sh1, sc1, g1, sh2, sc2, g2 = mods[1]
    lambda_init = 0.8 - 0.6 * math.exp(-0.3 * 1)
    w_in = w_in_odd[0]
    nd = dh * 2 * HEAD_DIM
    o0 = na + 2 * nk
    perm_d = _deinterleave_cols(2 * dh)
    perm_one = _deinterleave_cols(1)
    w_cq = w_in[:, :na][:, perm_q].astype(BF16)
    w_ck = w_in[:, na:na + nk][:, perm_k].astype(BF16)
    w_dq = w_in[:, o0:o0 + nd][:, perm_d].astype(BF16)
    w_dk = w_in[:, o0 + nd:o0 + 2 * nd][:, perm_d].astype(BF16)
    w_v = jnp.concatenate([w_in[:, na + nk:o0], w_in[:, o0 + 2 * nd:]], 1).astype(BF16)
    qg = row2(q_norm_g[0][perm_one]).astype(F32)
    kg = row2(k_norm_g[0][perm_one]).astype(F32)
    cq_rot, cq_nopos = _proj(h, w_cq, qscale(na), n_rows=n_lat, gain=qg, rope_tabs=tabs, emit_nopos=True,
                             name="proj_cq", **kw)
    ck = _proj(h, w_ck, ones(nk), n_rows=n_all, gain=kg, rope_tabs=tabs, name="proj_ck", **kw)
    dq_rot, dq_nopos = _proj(h, w_dq, qscale(nd), n_rows=n_lat, rope_tabs=tabs, emit_nopos=True,
                             name="proj_dq", **kw)
    dk = _proj(h, w_dk, ones(nd), n_rows=n_all, rope_tabs=tabs, name="proj_dk", **kw)
    vv = _proj(h, w_v, ones(nk + nd), n_rows=n_all, name="proj_odd_v", **kw)
    o_c = _dense_attn(cq_rot, cq_nopos, ck, vv, None, n_batch=n_batch, n_q=seq, q_row0=0, n_k=seq, k_row0=0,
                      ctx_len=ctx_len, ctx_row0=n_lat, n_kv=kvh, g=gq, q_col=0, k_col=0, v_col=0,
                      name="attn_c")
    lam_vecs = jnp.stack([lambda_q1[0], lambda_k1[0], lambda_q2[0], lambda_k2[0]]).astype(F32)
    o_d = _diff_attn(dq_rot, dq_nopos, dk, vv, lam_vecs, row2(subln_g[0]).astype(F32), n_batch=n_batch,
                     seq=seq, ctx_len=ctx_len, n_latent=n_lat, n_heads=dh, v_col=nk // (2 * HEAD_DIM),
                     lambda_init=lambda_init)
    w_out = w_out_odd[0].astype(BF16)
    x1, h2, idx, wts = _outproj_ln((o_c, None), (o_d, None), (xt, None), w_out[:na], w_out[na:], g1, sc2,
                                   sh2, row2(ln1_g[1]), row2(ln1_b[1]), w_router_t, router_bias_col,
                                   n_lat=n_lat, seq=seq, n_groups=n_groups, alpha=alpha)
    y = moe(h2, idx, wts, 1)
    x2, _ = _resid_ln(x1, y, g2, row2(ln2_g[1]), row2(ln2_b[1]), None, None, n_rows=n_lat, seq=seq,
                      n_groups=n_groups, alpha=alpha)
    return x2.reshape(n_batch, seq, d)
```

```python
import functools
import math

import numpy as np
import jax
import jax.numpy as jnp
from jax import lax
from jax.experimental import pallas as pl
from jax.experimental.pallas import tpu as pltpu

F32 = jnp.float32
BF16 = jnp.bfloat16

HEAD_DIM = 128
GRID_W = 64
ROPE_THETA = 10000.0
SCALE = HEAD_DIM ** -0.5
NEG_INF = -1e30
A_WINDOW = 128
A_BLOCK = 128
NA_ROWS = 8
NA_COLS = 16
B_TILE_ROWS = 4
B_HEADS_PER_STEP = 8
A_BLOCKS_PER_STEP = 2
ROW_PARTS = 2
EPILOGUE_ROWS = 128
N_EXPERTS = 16
N_GROUPS = 4
EXPERTS_PER_GROUP = N_EXPERTS // N_GROUPS
LN_EPS = 1e-5
RMS_EPS = 1e-6
LOG2E = math.log2(math.e)
Q_SCALE = SCALE * LOG2E

V7X_VMEM_BYTES = 64 * 1024 * 1024
VMEM_LIMIT = V7X_VMEM_BYTES - 8 * 1024 * 1024
LANES = 128

NT_DIMS = (((1,), (1,)), ((), ()))


def _params(*sem):
    return pltpu.CompilerParams(dimension_semantics=sem, vmem_limit_bytes=VMEM_LIMIT)


def _adaln_kernel(c_ref, w_ref, b_ref, o_ref):
    c = c_ref[...]
    a = (c * (1.0 / (1.0 + jnp.exp(-c)))).astype(BF16)
    o_ref[...] = jnp.dot(a, w_ref[...].astype(BF16), preferred_element_type=F32) + b_ref[...]


def _adaln(c_rows, w_ada, b_ada):
    depth, d, n6 = w_ada.shape
    rows = c_rows.shape[0]
    tn = 1024
    return pl.pallas_call(
        _adaln_kernel,
        out_shape=jax.ShapeDtypeStruct((depth, rows, n6), F32),
        grid=(depth, n6 // tn),
        in_specs=[
            pl.BlockSpec((rows, d), lambda l, j: (0, 0)),
            pl.BlockSpec((None, d, tn), lambda l, j: (l, 0, j)),
            pl.BlockSpec((None, 1, tn), lambda l, j: (l, 0, j)),
        ],
        out_specs=pl.BlockSpec((None, rows, tn), lambda l, j: (l, 0, j)),
        compiler_params=_params("parallel", "parallel"),
        name="adaln",
    )(c_rows, w_ada, b_ada.reshape(depth, 1, n6))


def _two_stream_specs(tm, width, n_lat_tiles):
    return [pl.BlockSpec((tm, width), lambda i: (jnp.minimum(i, n_lat_tiles - 1), 0)),
            pl.BlockSpec((tm, width), lambda i: (jnp.maximum(i - n_lat_tiles, 0), 0))]


def _pick_stream(lat_ref, ctx_ref, n_lat_tiles):
    if ctx_ref is None:
        return lat_ref[...]
    return jnp.where(pl.program_id(0) >= n_lat_tiles, ctx_ref[...], lat_ref[...])


def _modulate_kernel(x_ref, c_ref, sc_ref, sh_ref, o_ref, *, n_lat_tiles):
    x = _pick_stream(x_ref, c_ref, n_lat_tiles)
    o_ref[...] = (x * (1.0 + sc_ref[0]) + sh_ref[0]).astype(o_ref.dtype)


def _modulate(x_lat, x_ctx, sc, sh, *, seq, n_groups):
    d = x_lat.shape[1]
    rows = x_lat.shape[0] + x_ctx.shape[0]
    tm = 512
    n_lat_tiles = x_lat.shape[0] // tm
    grp = lambda i: (jnp.minimum(i // (seq // tm), n_groups - 1), 0, 0)
    return pl.pallas_call(
        functools.partial(_modulate_kernel, n_lat_tiles=n_lat_tiles),
        out_shape=jax.ShapeDtypeStruct((rows, d), BF16),
        grid=(rows // tm,),
        in_specs=_two_stream_specs(tm, d, n_lat_tiles)
        + [pl.BlockSpec((1, 1, d), grp), pl.BlockSpec((1, 1, d), grp)],
        out_specs=pl.BlockSpec((tm, d), lambda i: (i, 0)),
        compiler_params=_params("parallel"),
        name="modulate",
    )(x_lat, x_ctx, sc, sh)


def _cast_kernel(x_ref, o_ref):
    o_ref[...] = x_ref[...].astype(o_ref.dtype)


def _cast_bf16(w):
    shape = w.shape
    w2 = w.reshape(-1, shape[-1])
    tm = min(w2.shape[0], (2 * 1024 * 1024) // shape[-1])
    out = pl.pallas_call(
        _cast_kernel,
        out_shape=jax.ShapeDtypeStruct(w2.shape, BF16),
        grid=(w2.shape[0] // tm,),
        in_specs=[pl.BlockSpec((tm, shape[-1]), lambda i: (i, 0))],
        out_specs=pl.BlockSpec((tm, shape[-1]), lambda i: (i, 0)),
        compiler_params=_params("parallel"),
        name="cast_bf16",
    )(w2)
    return out.reshape(shape)


def _proj_kernel(*refs, tn, norm, rope, emit_nopos):
    it = iter(refs)
    x_ref, w_ref, cs_ref = next(it), next(it), next(it)
    g_ref = next(it) if norm else None
    cos_ref, sin_ref = (next(it), next(it)) if rope else (None, None)
    o_ref = next(it)
    n_ref = next(it) if emit_nopos else None
    tm = x_ref.shape[0]
    rows_per_part = tm // ROW_PARTS
    for part in range(ROW_PARTS):
        rows = slice(part * rows_per_part, (part + 1) * rows_per_part)
        acc = jnp.dot(x_ref[rows, :], w_ref[...], preferred_element_type=F32)
        for hd in range(tn // HEAD_DIM):
            sl = slice(hd * HEAD_DIM, (hd + 1) * HEAD_DIM)
            xh = acc[:, sl]
            if norm:
                xh = xh * lax.rsqrt(jnp.mean(xh * xh, axis=-1, keepdims=True) + RMS_EPS) * g_ref[...]
            cs = cs_ref[:, sl]
            if rope:
                rot = xh * cos_ref[rows, :] + pltpu.roll(xh, HEAD_DIM // 2, 1) * sin_ref[rows, :]
                o_ref[rows, sl] = (rot * cs).astype(o_ref.dtype)
                if emit_nopos:
                    n_ref[rows, sl] = (xh * cs).astype(n_ref.dtype)
            else:
                o_ref[rows, sl] = (xh * cs).astype(o_ref.dtype)


def _proj(x, w, col_scale, *, n_rows, seq, n_latent, tn=None, gain=None, rope_tabs=None,
          emit_nopos=False, name="proj"):
    d = x.shape[1]
    nc = w.shape[1]
    tm = min(1024, seq)
    tn = nc if tn is None else tn
    norm, rope = gain is not None, rope_tabs is not None
    in_specs = [pl.BlockSpec((tm, d), lambda i, j: (i, 0)),
                pl.BlockSpec((d, tn), lambda i, j: (0, j)),
                pl.BlockSpec((1, tn), lambda i, j: (0, j))]
    args = [x, w, col_scale]
    if norm:
        in_specs.append(pl.BlockSpec((1, HEAD_DIM), lambda i, j: (0, 0)))
        args.append(gain)
    if rope:
        per_seq = seq // tm
        tab = lambda i, j: (jnp.where(i < n_latent // tm, i % per_seq, per_seq), 0)
        in_specs += [pl.BlockSpec((tm, HEAD_DIM), tab)] * 2
        args += list(rope_tabs)
    n_out = 2 if emit_nopos else 1
    out_shape = [jax.ShapeDtypeStruct((n_rows, nc), BF16)] * n_out
    out_specs = [pl.BlockSpec((tm, tn), lambda i, j: (i, j))] * n_out
    res = pl.pallas_call(
        functools.partial(_proj_kernel, tn=tn, norm=norm, rope=rope, emit_nopos=emit_nopos),
        out_shape=out_shape,
        grid=(n_rows // tm, nc // tn),
        in_specs=in_specs,
        out_specs=out_specs,
        compiler_params=_params("parallel", "parallel"),
        name=name,
    )(*args)
    return res if emit_nopos else res[0]


def _stack_heads(x, g):
    if g == 1:
        return x
    return jnp.concatenate([x[:, i * HEAD_DIM:(i + 1) * HEAD_DIM] for i in range(g)], axis=0)


def _unstack_store(o_ref, o, g, t, col0=0):
    for i in range(g):
        o_ref[:, col0 + i * HEAD_DIM:col0 + (i + 1) * HEAD_DIM] = o[i * t:(i + 1) * t].astype(o_ref.dtype)


def _sink_column(sink_ref, first_head, g, t):
    cols = [jnp.full((t, 1), sink_ref[first_head + i] * LOG2E, F32) for i in range(g)]
    return cols[0] if g == 1 else jnp.concatenate(cols, axis=0)


def _with_ones(v):
    return jnp.concatenate([v, jnp.ones_like(v)], axis=1)


def _row_max(pieces):
    cols = [p[:, c:c + LANES] for p in pieces for c in range(0, p.shape[1], LANES)]
    return jnp.max(functools.reduce(jnp.maximum, cols), axis=-1, keepdims=True)


def _online_update(s, v_aug, m, acc):
    m_new = jnp.maximum(m, jnp.max(s, axis=-1, keepdims=True))
    p = jnp.exp2(s - m_new)
    acc = jnp.exp2(m - m_new) * acc + jnp.dot(p.astype(BF16), v_aug, preferred_element_type=F32)
    return m_new, acc


def _band_kernel(*refs, nq, nh, g, bq, has_sink, same_qc, shared_bias):
    it = iter(refs)
    sink_ref = next(it) if has_sink else None
    q_ref = next(it)
    qc_ref = q_ref if same_qc else next(it)
    k_refs = [next(it) for _ in range(nq + 2)]
    v_refs = [next(it) for _ in range(nq + 2)]
    kx_ref, vx_ref = next(it), next(it)
    bias_refs = [next(it) for _ in range(nq)]
    o_ref = next(it)
    hw = g * HEAD_DIM
    for qb in range(nq):
        rows = slice(qb * bq, (qb + 1) * bq)
        for h in range(nh):
            qs = slice(h * hw, (h + 1) * hw)
            ks = slice(h * HEAD_DIM, (h + 1) * HEAD_DIM)
            q = _stack_heads(q_ref[rows, qs], g)
            qc = q if same_qc else _stack_heads(qc_ref[rows, qs], g)
            bias = bias_refs[qb].at[0 if shared_bias else h]
            s = [lax.dot_general(q, k_refs[qb + j][:, ks], NT_DIMS, preferred_element_type=F32)
                 + bias[:, j * bq:(j + 1) * bq] for j in range(3)]
            s.append(lax.dot_general(qc, kx_ref[:, ks], NT_DIMS, preferred_element_type=F32))
            vs = [v_refs[qb + j][:, ks] for j in range(3)] + [vx_ref[:, ks]]
            m = _row_max(s)
            if has_sink:
                sink = _sink_column(sink_ref, (pl.program_id(0) * nh + h) * g, g, bq)
                m = jnp.maximum(m, sink)
            o = functools.reduce(jnp.add, [
                jnp.dot(jnp.exp2(sj - m).astype(BF16), _with_ones(vj), preferred_element_type=F32)
                for sj, vj in zip(s, vs)])
            den = o[:, HEAD_DIM:]
            if has_sink:
                den = den + jnp.exp2(sink - m)
            _unstack_store(o_ref.at[rows, :], o[:, :HEAD_DIM] / den, g, bq, col0=h * hw)


def _band_attn(q, qc, k, v, bias, sink, *, n_batch, seq, ctx_len, n_latent, n_kv, nh, g, bq,
               q_col, k_col, v_col, name, nq=1):
    nblk = seq // bq
    nstep = nblk // nq
    ctx_blk = n_latent // ctx_len
    shared_bias = bias.shape[0] == 1
    has_sink, same_qc = sink is not None, qc is None
    qw, kw = nh * g * HEAD_DIM, nh * HEAD_DIM

    def q_map(hg, i, b):
        return (b * nstep + i, q_col + hg)

    def kv_map(col, off):
        return lambda hg, i, b: (b * nblk + jnp.clip(i * nq + off, 0, nblk - 1), col + hg)

    def ctx_map(col):
        return lambda hg, i, b: (ctx_blk + b, col + hg)

    def bias_map(qb):
        def index(hg, i, b):
            blk = i * nq + qb
            case = jnp.where(blk == 0, 0, jnp.where(blk == nblk - 1, 2, 1))
            return (0 if shared_bias else hg, case, 0, 0)
        return index

    in_specs, args = [], []
    if has_sink:
        in_specs.append(pl.BlockSpec(memory_space=pltpu.SMEM))
        args.append(sink)
    in_specs.append(pl.BlockSpec((nq * bq, qw), q_map))
    args.append(q)
    if not same_qc:
        in_specs.append(pl.BlockSpec((nq * bq, qw), q_map))
        args.append(qc)
    offsets = range(-1, nq + 1)
    in_specs += [pl.BlockSpec((bq, kw), kv_map(k_col, off)) for off in offsets]
    args += [k] * len(offsets)
    in_specs += [pl.BlockSpec((bq, kw), kv_map(v_col, off)) for off in offsets]
    args += [v] * len(offsets)
    in_specs += [pl.BlockSpec((ctx_len, kw), ctx_map(k_col)), pl.BlockSpec((ctx_len, kw), ctx_map(v_col))]
    args += [k, v]
    in_specs += [pl.BlockSpec((1 if shared_bias else nh, None, g * bq, 3 * bq), bias_map(qb)) for qb in range(nq)]
    args += [bias] * nq
    return pl.pallas_call(
        functools.partial(_band_kernel, nq=nq, nh=nh, g=g, bq=bq, has_sink=has_sink, same_qc=same_qc,
                          shared_bias=shared_bias),
        out_shape=jax.ShapeDtypeStruct((n_latent, n_kv * g * HEAD_DIM), BF16),
        grid=(n_kv // nh, nstep, n_batch),
        in_specs=in_specs,
        out_specs=pl.BlockSpec((nq * bq, qw), lambda hg, i, b: (b * nstep + i, hg)),
        compiler_params=_params("parallel", "parallel", "parallel"),
        name=name,
    )(*args)


def _dense_kernel(*refs, g, tq, tk, n_k, has_ctx, has_sink):
    it = iter(refs)
    sink_ref = next(it) if has_sink else None
    q_ref = next(it)
    qc_ref = next(it) if has_ctx else None
    k_ref, v_ref = next(it), next(it)
    kx_ref, vx_ref = (next(it), next(it)) if has_ctx else (None, None)
    o_ref = next(it)
    q = _stack_heads(q_ref[...], g)
    rows = g * tq
    m = jnp.full((rows, 1), NEG_INF, F32)
    acc = jnp.zeros((rows, 2 * HEAD_DIM), F32)
    for c in range(n_k // tk):
        s = lax.dot_general(q, k_ref[c * tk:(c + 1) * tk, :], NT_DIMS, preferred_element_type=F32)
        m, acc = _online_update(s, _with_ones(v_ref[c * tk:(c + 1) * tk, :]), m, acc)
    if has_ctx:
        qc = _stack_heads(qc_ref[...], g)
        s = lax.dot_general(qc, kx_ref[...], NT_DIMS, preferred_element_type=F32)
        m, acc = _online_update(s, _with_ones(vx_ref[...]), m, acc)
    num, den = acc[:, :HEAD_DIM], acc[:, HEAD_DIM:]
    if has_sink:
        sink = _sink_column(sink_ref, pl.program_id(1) * g, g, tq)
        m_new = jnp.maximum(m, sink)
        alpha = jnp.exp2(m - m_new)
        num = alpha * num
        den = alpha * den + jnp.exp2(sink - m_new)
    _unstack_store(o_ref, num / den, g, tq)


def _dense_attn(q, qc, k, v, sink, *, n_batch, n_q, q_row0, n_k, k_row0, ctx_len, ctx_row0,
                n_kv, g, q_col, k_col, v_col, name):
    tq = min(256, n_q)
    tk = min(512, n_k)
    nq_blk = n_q // tq
    has_ctx, has_sink = qc is not None, sink is not None

    def q_map(b, h, i):
        return (q_row0 // tq + b * nq_blk + i, q_col + h)

    in_specs, args = [], []
    if has_sink:
        in_specs.append(pl.BlockSpec(memory_space=pltpu.SMEM))
        args.append(sink)
    in_specs.append(pl.BlockSpec((tq, g * HEAD_DIM), q_map))
    args.append(q)
    if has_ctx:
        in_specs.append(pl.BlockSpec((tq, g * HEAD_DIM), q_map))
        args.append(qc)
    in_specs += [pl.BlockSpec((n_k, HEAD_DIM), lambda b, h, i: (k_row0 // n_k + b, k_col + h)),
                 pl.BlockSpec((n_k, HEAD_DIM), lambda b, h, i: (k_row0 // n_k + b, v_col + h))]
    args += [k, v]
    if has_ctx:
        in_specs += [pl.BlockSpec((ctx_len, HEAD_DIM), lambda b, h, i: (ctx_row0 // ctx_len + b, k_col + h)),
                     pl.BlockSpec((ctx_len, HEAD_DIM), lambda b, h, i: (ctx_row0 // ctx_len + b, v_col + h))]
        args += [k, v]
    return pl.pallas_call(
        functools.partial(_dense_kernel, g=g, tq=tq, tk=tk, n_k=n_k, has_ctx=has_ctx,
                          has_sink=has_sink),
        out_shape=jax.ShapeDtypeStruct((n_batch * n_q, n_kv * g * HEAD_DIM), BF16),
        grid=(n_batch, n_kv, nq_blk),
        in_specs=in_specs,
        out_specs=pl.BlockSpec((tq, g * HEAD_DIM), lambda b, h, i: (b * nq_blk + i, h)),
        compiler_params=_params("parallel", "parallel", "parallel"),
        name=name,
    )(*args)


def _diff_update(s, v, m, l, acc):
    m_new = jnp.maximum(m, jnp.max(s, axis=-1, keepdims=True))
    alpha = jnp.exp2(m - m_new)
    p = jnp.exp2(s - m_new)
    l = alpha * l + jnp.sum(p, axis=-1, keepdims=True)
    acc = alpha * acc + jnp.dot(p.astype(BF16), v, preferred_element_type=F32)
    return m_new, l, acc


def _diff_kernel(lam_ref, g_ref, q_ref, qc_ref, k_ref, v_ref, kx_ref, vx_ref, o_ref, *,
                 tq, tk, n_k, lambda_init):
    dv = 2 * HEAD_DIM
    lam = (jnp.exp(jnp.sum(lam_ref[0:1, :] * lam_ref[1:2, :], axis=1, keepdims=True))
           - jnp.exp(jnp.sum(lam_ref[2:3, :] * lam_ref[3:4, :], axis=1, keepdims=True)) + lambda_init)
    sls = [slice(t * HEAD_DIM, (t + 1) * HEAD_DIM) for t in range(2)]
    state = [(jnp.full((tq, 1), NEG_INF, F32), jnp.zeros((tq, 1), F32), jnp.zeros((tq, dv), F32))
             for _ in range(2)]
    for c in range(n_k // tk):
        rows = slice(c * tk, (c + 1) * tk)
        for t in range(2):
            s = lax.dot_general(q_ref[:, sls[t]], k_ref[rows, sls[t]], NT_DIMS, preferred_element_type=F32)
            state[t] = _diff_update(s, v_ref[rows, :], *state[t])
    outs = []
    for t in range(2):
        s = lax.dot_general(qc_ref[:, sls[t]], kx_ref[:, sls[t]], NT_DIMS, preferred_element_type=F32)
        _, l, acc = _diff_update(s, vx_ref[...], *state[t])
        outs.append(acc / l)
    o = outs[0] - lam * outs[1]
    o = o * lax.rsqrt(jnp.mean(o * o, axis=-1, keepdims=True) + RMS_EPS) * g_ref[...]
    o_ref[...] = (o * (1.0 - lambda_init)).astype(o_ref.dtype)


def _diff_attn(q, qc, k, v, lam_vecs, subln_g, *, n_batch, seq, ctx_len, n_latent, n_heads,
               v_col, lambda_init):
    tq = min(512, seq)
    tk = min(512, seq)
    dv = 2 * HEAD_DIM
    nq_blk = seq // tq
    ctx_blk = n_latent // ctx_len
    q_map = lambda b, h, i: (b * nq_blk + i, h)
    return pl.pallas_call(
        functools.partial(_diff_kernel, tq=tq, tk=tk, n_k=seq, lambda_init=lambda_init),
        out_shape=jax.ShapeDtypeStruct((n_latent, n_heads * dv), BF16),
        grid=(n_batch, n_heads, nq_blk),
        in_specs=[
            pl.BlockSpec((4, HEAD_DIM), lambda b, h, i: (0, 0)),
            pl.BlockSpec((1, dv), lambda b, h, i: (0, 0)),
            pl.BlockSpec((tq, dv), q_map),
            pl.BlockSpec((tq, dv), q_map),
            pl.BlockSpec((seq, dv), lambda b, h, i: (b, h)),
            pl.BlockSpec((seq, dv), lambda b, h, i: (b, v_col + h)),
            pl.BlockSpec((ctx_len, dv), lambda b, h, i: (ctx_blk + b, h)),
            pl.BlockSpec((ctx_len, dv), lambda b, h, i: (ctx_blk + b, v_col + h)),
        ],
        out_specs=pl.BlockSpec((tq, dv), q_map),
        compiler_params=_params("parallel", "parallel", "parallel"),
        name="diff_attn",
    )(lam_vecs, subln_g, q, qc, k, v, k, v)


def _layer_norm(z, g, b):
    mu = jnp.mean(z, axis=-1, keepdims=True)
    zc = z - mu
    var = jnp.mean(zc * zc, axis=-1, keepdims=True)
    return zc * lax.rsqrt(var + LN_EPS) * g + b


def _split_bf16(x):
    hi = x.astype(BF16)
    return hi, (x - hi.astype(F32)).astype(BF16)


def _first_argmax(vals, idx, width):
    m = jnp.max(vals, axis=0, keepdims=True)
    first = jnp.min(jnp.where(vals == m, idx, float(width)), axis=0, keepdims=True)
    return m, first


def _route(logits, bias):
    e, t = logits.shape
    scores = 1.0 / (1.0 + jnp.exp(-logits))
    biased = scores + bias
    row_i = lax.broadcasted_iota(jnp.int32, (e, t), 0)
    grp = lax.shift_right_logical(row_i, int(math.log2(EXPERTS_PER_GROUP)))
    row = row_i.astype(F32)
    neg = -jnp.inf
    best_score, best = None, None
    for gi in range(N_GROUPS):
        vg = jnp.where(grp == gi, biased, neg)
        m1, i1 = _first_argmax(vg, row, e)
        m2 = jnp.max(jnp.where(row == i1, neg, vg), axis=0, keepdims=True)
        gs = m1 + m2
        if gi == 0:
            best_score, best = gs, jnp.zeros((1, t), jnp.int32)
        else:
            upd = gs > best_score
            best = jnp.where(upd, gi, best)
            best_score = jnp.where(upd, gs, best_score)
    masked = jnp.where(grp == best, biased, neg)
    _, e1 = _first_argmax(masked, row, e)
    _, e2 = _first_argmax(jnp.where(row == e1, neg, masked), row, e)
    w1 = jnp.sum(jnp.where(row == e1, scores, 0.0), axis=0, keepdims=True)
    w2 = jnp.sum(jnp.where(row == e2, scores, 0.0), axis=0, keepdims=True)
    den = w1 + w2
    two = lax.broadcasted_iota(jnp.int32, (2, t), 0)
    return jnp.where(two == 0, e1, e2).astype(jnp.int32), jnp.where(two == 0, w1 / den, w2 / den)


def _outproj_kernel(*refs, alpha, two_streams, n_lat_tiles):
    it = iter(refs)
    xa_ref, xa_ctx = next(it), (next(it) if two_streams else None)
    xb_ref, xb_ctx = next(it), (next(it) if two_streams else None)
    xres_ref, xres_ctx = next(it), (next(it) if two_streams else None)
    (wa_ref, wb_ref, g1_ref, sc2_ref, sh2_ref, lng_ref, lnb_ref, wrt_ref, rb_ref,
     x1_ref, h2_ref, idx_ref, wts_ref) = it
    w_hi, w_lo = _split_bf16(wrt_ref[...])
    w_cat = jnp.concatenate([w_hi, w_lo], axis=0)
    tm = x1_ref.shape[0]
    rows_per_part = tm // ROW_PARTS
    for part in range(ROW_PARTS):
        rows = slice(part * rows_per_part, (part + 1) * rows_per_part)
        pick = lambda lat, ctx: _pick_stream(lat.at[rows, :], None if ctx is None else ctx.at[rows, :], n_lat_tiles)
        y_all = (jnp.dot(pick(xa_ref, xa_ctx), wa_ref[...], preferred_element_type=F32)
                 + jnp.dot(pick(xb_ref, xb_ctx), wb_ref[...], preferred_element_type=F32))
        xres_all = pick(xres_ref, xres_ctx)
        n_sub = rows_per_part // EPILOGUE_ROWS
        for sub in range(n_sub):
            sr = slice(sub * EPILOGUE_ROWS, (sub + 1) * EPILOGUE_ROWS)
            orow = slice(part * rows_per_part + sub * EPILOGUE_ROWS,
                         part * rows_per_part + (sub + 1) * EPILOGUE_ROWS)
            x1 = _layer_norm(alpha * xres_all[sr] + g1_ref[0] * y_all[sr], lng_ref[...], lnb_ref[...])
            x1_ref[orow, :] = x1
            h2 = x1 * (1.0 + sc2_ref[0]) + sh2_ref[0]
            h2_ref[orow, :] = h2
            h_hi, h_lo = _split_bf16(h2)
            both = lax.dot_general(w_cat, h_hi, NT_DIMS, preferred_element_type=F32)
            logits = (both[:N_EXPERTS] + both[N_EXPERTS:]
                      + lax.dot_general(w_hi, h_lo, NT_DIMS, preferred_element_type=F32))
            idx, wts = _route(logits, rb_ref[...])
            idx_ref[:, orow] = idx
            wts_ref[:, orow] = wts


def _outproj_ln(xa, xb, xres, wa, wb, g1, sc2, sh2, ln_g, ln_b, w_router_t, router_bias, *,
                n_lat, seq, n_groups, alpha):
    two_streams = xa[1] is not None
    d = xres[0].shape[1]
    ka, kb = xa[0].shape[1], xb[0].shape[1]
    tm = 256 * ROW_PARTS
    n_lat_tiles = n_lat // tm
    n_rows = n_lat + (xres[1].shape[0] if two_streams else 0)
    row = lambda i: (i, 0)
    col = lambda i: (0, i)
    const = lambda i: (0, 0)
    grp = lambda i: (jnp.minimum(i // (seq // tm), n_groups - 1), 0, 0)
    in_specs, args = [], []
    for pair, width in ((xa, ka), (xb, kb), (xres, d)):
        if two_streams:
            in_specs += _two_stream_specs(tm, width, n_lat_tiles)
            args += list(pair)
        else:
            in_specs.append(pl.BlockSpec((tm, width), row))
            args.append(pair[0])
    in_specs += [pl.BlockSpec((ka, d), const), pl.BlockSpec((kb, d), const),
                 pl.BlockSpec((1, 1, d), grp), pl.BlockSpec((1, 1, d), grp), pl.BlockSpec((1, 1, d), grp),
                 pl.BlockSpec((1, d), const), pl.BlockSpec((1, d), const),
                 pl.BlockSpec((N_EXPERTS, d), const), pl.BlockSpec((N_EXPERTS, 1), const)]
    args += [wa, wb, g1, sc2, sh2, ln_g, ln_b, w_router_t, router_bias]
    return pl.pallas_call(
        functools.partial(_outproj_kernel, alpha=alpha, two_streams=two_streams, n_lat_tiles=n_lat_tiles),
        out_shape=[jax.ShapeDtypeStruct((n_rows, d), F32), jax.ShapeDtypeStruct((n_rows, d), F32),
                   jax.ShapeDtypeStruct((2, n_rows), jnp.int32), jax.ShapeDtypeStruct((2, n_rows), F32)],
        grid=(n_rows // tm,),
        in_specs=in_specs,
        out_specs=[pl.BlockSpec((tm, d), row), pl.BlockSpec((tm, d), row),
                   pl.BlockSpec((2, tm), col), pl.BlockSpec((2, tm), col)],
        compiler_params=_params("parallel"),
        name="outproj_ln",
    )(*args)


def _pack_bf16_pairs(x):
    n = x.shape[1] // 2
    lo = pltpu.bitcast(x[:, :n].astype(BF16).astype(F32), jnp.uint32)
    hi = pltpu.bitcast(x[:, n:].astype(BF16).astype(F32), jnp.uint32)
    return jnp.bitwise_or(lax.shift_right_logical(lo, jnp.uint32(16)), hi)


def _unpack_bf16_pairs(w):
    lo = pltpu.bitcast(lax.shift_left(w, jnp.uint32(16)), F32)
    hi = pltpu.bitcast(jnp.bitwise_and(w, jnp.uint32(0xFFFF0000)), F32)
    return jnp.concatenate([lo, hi], axis=1)


def _moe_kernel(te_ref, nv_ref, src_ref, dst_ref, h_hbm, gw_ref, wg_ref, wu_ref, wd_ref, y_hbm,
                xbuf, obuf, gsem, ssem, *, tm, ff_chunks):
    i = pl.program_id(0)
    n = pl.num_programs(0)
    nv = nv_ref[0]

    def start_gather(tile, s):
        base = tile * (tm // 2)
        for r in range(tm):
            word = src_ref[base + r // 2]
            tok = jnp.bitwise_and(word, 0xFFFF) if r % 2 == 0 else lax.shift_right_logical(word, 16)
            pltpu.make_async_copy(h_hbm.at[pl.ds(tok, 1)], xbuf.at[s, pl.ds(r, 1)],
                                  gsem.at[s]).start(priority=r % 2)

    def wait_gather(s):
        pltpu.make_async_copy(h_hbm.at[pl.ds(0, tm)], xbuf.at[s], gsem.at[s]).wait()

    def start_scatter(tile, s):
        base = tile * tm
        for r in range(tm):
            pltpu.make_async_copy(obuf.at[s, pl.ds(r, 1)], y_hbm.at[pl.ds(dst_ref[base + r], 1)],
                                  ssem.at[s]).start(priority=r % 2)

    def wait_scatter(s):
        pltpu.make_async_copy(obuf.at[s], y_hbm.at[pl.ds(0, tm)], ssem.at[s]).wait()

    @pl.when(i == 0)
    def _():
        start_gather(0, 0)
        obuf[...] = jnp.zeros_like(obuf)
        pad0 = y_hbm.shape[0] - 2 * tm
        fills = [pltpu.make_async_copy(obuf.at[s], y_hbm.at[pl.ds(pad0 + s * tm, tm)], ssem.at[s])
                 for s in range(2)]
        for f in fills:
            f.start()
        for f in fills:
            f.wait()

    def step(slot):
        wait_gather(slot)

        @pl.when(i >= 2)
        def _():
            wait_scatter(slot)

        start_gather(jnp.minimum(i + 1, n - 1), 1 - slot)
        x = xbuf[slot].astype(BF16)
        fc = wg_ref.shape[1] // ff_chunks
        y = None
        for c in range(ff_chunks):
            cs = slice(c * fc, (c + 1) * fc)
            gate = jnp.dot(x, wg_ref[:, cs], preferred_element_type=F32)
            up = jnp.dot(x, wu_ref[:, cs], preferred_element_type=F32)
            act = (gate * (1.0 / (1.0 + jnp.exp(-gate))) * up).astype(BF16)
            part = jnp.dot(act, wd_ref[cs, :], preferred_element_type=F32)
            y = part if y is None else y + part
        obuf[slot] = _pack_bf16_pairs(y * gw_ref[...])
        start_scatter(i, slot)

    for slot in range(2):
        @pl.when(jnp.logical_and(i < nv, jnp.bitwise_and(i, 1) == slot))
        def _():
            step(slot)

    @pl.when(i == n - 1)
    def _():
        wait_gather(jnp.bitwise_and(nv, 1))

        @pl.when(nv >= 1)
        def _():
            wait_scatter(jnp.bitwise_and(nv - 1, 1))

        @pl.when(nv >= 2)
        def _():
            wait_scatter(jnp.bitwise_and(nv, 1))


def _moe_ffn(h2, src_packed, dst_row, gw, tile_expert, n_valid, w_gate, w_up, w_down, *, layer, tm):
    t, d = h2.shape
    ff = w_gate.shape[3]
    n_tiles = dst_row.shape[0] // tm
    return pl.pallas_call(
        functools.partial(_moe_kernel, tm=tm, ff_chunks=2),
        out_shape=jax.ShapeDtypeStruct((2 * t + 2 * tm, d // 2), jnp.uint32),
        grid_spec=pltpu.PrefetchScalarGridSpec(
            num_scalar_prefetch=4,
            grid=(n_tiles,),
            in_specs=[pl.BlockSpec(memory_space=pl.ANY),
                      pl.BlockSpec((tm, 1), lambda i, te, nv, src, dst: (i, 0)),
                      pl.BlockSpec((None, None, d, ff), lambda i, te, nv, src, dst: (layer, te[i], 0, 0)),
                      pl.BlockSpec((None, None, d, ff), lambda i, te, nv, src, dst: (layer, te[i], 0, 0)),
                      pl.BlockSpec((None, None, ff, d), lambda i, te, nv, src, dst: (layer, te[i], 0, 0))],
            out_specs=pl.BlockSpec(memory_space=pl.ANY),
            scratch_shapes=[pltpu.VMEM((2, tm, d), F32), pltpu.VMEM((2, tm, d // 2), jnp.uint32),
                            pltpu.SemaphoreType.DMA((2,)), pltpu.SemaphoreType.DMA((2,))],
        ),
        compiler_params=_params("arbitrary"),
        name="moe_ffn",
    )(tile_expert, n_valid, src_packed, dst_row, h2, gw, w_gate, w_up, w_down)


def _take(x, idx):
    return x.at[idx].get(mode="promise_in_bounds")


def _dispatch(idx, wts, *, tm):
    t = idx.shape[1]
    n_pairs = 2 * t
    n_tiles = n_pairs // tm + N_EXPERTS
    e_flat = idx.reshape(n_pairs)
    w_flat = wts.reshape(n_pairs)
    pair_bits = max(1, (n_pairs - 1).bit_length())
    keys = jnp.bitwise_or(jnp.left_shift(e_flat, pair_bits), jnp.arange(n_pairs, dtype=jnp.int32))
    order = jnp.bitwise_and(lax.sort(keys), (1 << pair_bits) - 1)
    experts = jnp.arange(N_EXPERTS, dtype=jnp.int32)
    counts = jnp.sum(e_flat[None, :] == experts[:, None], axis=1).astype(jnp.int32)
    start = jnp.cumsum(counts) - counts
    padded = ((counts + tm - 1) // tm) * tm
    pend = jnp.cumsum(padded)
    pstart = pend - padded
    tile_pos = jnp.arange(n_tiles, dtype=jnp.int32) * tm
    tile_expert = jnp.minimum(jnp.sum(tile_pos[:, None] >= pend[None, :], axis=1), N_EXPERTS - 1).astype(jnp.int32)
    local = (tile_pos - pstart[tile_expert])[:, None] + jnp.arange(tm, dtype=jnp.int32)[None, :]
    valid = local < counts[tile_expert][:, None]
    sorted_pos = jnp.clip(start[tile_expert][:, None] + local, 0, n_pairs - 1)
    src_pair = _take(order, sorted_pos)
    src_tok = jnp.where(valid, jnp.where(src_pair >= t, src_pair - t, src_pair), 0)
    pad_row = n_pairs + (tile_pos[:, None] + jnp.arange(tm, dtype=jnp.int32)[None, :]) % (2 * tm)
    dst_row = jnp.where(valid, src_pair, pad_row)
    gw = jnp.where(valid, _take(w_flat, src_pair), 0.0).astype(F32)
    src_packed = jnp.bitwise_or(src_tok[:, 0::2], jnp.left_shift(src_tok[:, 1::2], 16))
    n_valid = (pend[-1] // tm).astype(jnp.int32).reshape(1)
    return src_packed.reshape(-1), dst_row.reshape(-1), gw.reshape(-1, 1), tile_expert, n_valid


def _resid_ln_kernel(*refs, alpha, emit_h):
    it = iter(refs)
    x_ref, f0_ref, f1_ref, g_ref, lng_ref, lnb_ref = (next(it) for _ in range(6))
    sc_ref, sh_ref = (next(it), next(it)) if emit_h else (None, None)
    o_ref = next(it)
    f = _unpack_bf16_pairs(f0_ref[...]) + _unpack_bf16_pairs(f1_ref[...])
    x2 = _layer_norm(alpha * x_ref[...] + g_ref[0] * f, lng_ref[...], lnb_ref[...])
    o_ref[...] = x2
    if emit_h:
        h_ref = next(it)
        h_ref[...] = (x2 * (1.0 + sc_ref[0]) + sh_ref[0]).astype(h_ref.dtype)


def _resid_ln(x, y, gate, ln_g, ln_b, next_sc, next_sh, *, n_rows, seq, n_groups, alpha):
    d = x.shape[1]
    tm = 512
    emit_h = next_sc is not None
    row = lambda i: (i, 0)
    const = lambda i: (0, 0)
    grp = lambda i: (jnp.minimum(i // (seq // tm), n_groups - 1), 0, 0)
    in_specs = [pl.BlockSpec((tm, d), row), pl.BlockSpec((tm, d // 2), row),
                pl.BlockSpec((tm, d // 2), lambda i: (n_rows // tm + i, 0)), pl.BlockSpec((1, 1, d), grp),
                pl.BlockSpec((1, d), const), pl.BlockSpec((1, d), const)]
    args = [x, y, y, gate, ln_g, ln_b]
    out_shape = [jax.ShapeDtypeStruct((n_rows, d), F32)]
    out_specs = [pl.BlockSpec((tm, d), row)]
    if emit_h:
        in_specs += [pl.BlockSpec((1, 1, d), grp)] * 2
        args += [next_sc, next_sh]
        out_shape.append(jax.ShapeDtypeStruct((n_rows, d), BF16))
        out_specs.append(pl.BlockSpec((tm, d), row))
    res = pl.pallas_call(
        functools.partial(_resid_ln_kernel, alpha=alpha, emit_h=emit_h),
        out_shape=out_shape,
        grid=(n_rows // tm,),
        in_specs=in_specs,
        out_specs=out_specs,
        compiler_params=_params("parallel"),
        name="resid_ln",
    )(*args)
    return res if emit_h else (res[0], None)


def _rope_tables(seq, tm):
    t = jnp.arange(seq, dtype=jnp.int32)
    n_freq = HEAD_DIM // 4
    inv = ROPE_THETA ** (-jnp.arange(n_freq, dtype=F32) / n_freq)
    row = (t // GRID_W).astype(F32)
    col = (t % GRID_W).astype(F32)
    ang = jnp.concatenate([row[:, None] * inv[None], col[:, None] * inv[None]], -1)
    cos, sin = jnp.cos(ang), jnp.sin(ang)
    cos2 = jnp.concatenate([cos, cos], -1)
    sin2 = jnp.concatenate([-sin, sin], -1)
    cos2 = jnp.concatenate([cos2, jnp.ones((tm, HEAD_DIM), F32)], 0)
    sin2 = jnp.concatenate([sin2, jnp.zeros((tm, HEAD_DIM), F32)], 0)
    return cos2, sin2


def _window_bias(seq, g):
    bq = A_BLOCK
    nblk = seq // bq
    rel = np.arange(3 * bq)[None, :] - np.arange(bq)[:, None]
    band = (rel >= bq - A_WINDOW) & (rel <= bq + A_WINDOW)
    tabs = []
    for blk in (0, min(1, nblk - 1), nblk - 1):
        kpos = blk * bq - bq + np.arange(3 * bq)
        ok = band & ((kpos >= 0) & (kpos < seq))[None, :]
        tabs.append(np.tile(np.where(ok, 0.0, NEG_INF).astype(np.float32), (g, 1)))
    return jnp.asarray(np.stack(tabs)[None])


def _neighbourhood_bias(rpb, seq):
    n_heads = rpb.shape[0]
    rows = seq // GRID_W
    kh, kw = min(NA_ROWS, rows), NA_COLS
    tr = B_TILE_ROWS
    nblk = rows // tr
    col = np.arange(GRID_W)
    col_start = np.clip(col - kw // 2, 0, GRID_W - kw)
    col_ok = (col[None, :] >= col_start[:, None]) & (col[None, :] < col_start[:, None] + kw)
    assert np.all(np.abs(col[None, :] - col[:, None])[col_ok] <= NA_COLS - 1)
    masks = []
    for blk in (0, min(1, nblk - 1), nblk - 1):
        r = blk * tr + np.arange(tr)
        r0 = np.clip(r - kh // 2, 0, rows - kh)
        krow = (blk - 1) * tr + np.arange(3 * tr)
        row_ok = (krow[None, :] >= r0[:, None]) & (krow[None, :] < r0[:, None] + kh)
        ok = row_ok[:, None, :, None] & col_ok[None, :, None, :]
        masks.append(ok.reshape(tr * GRID_W, 3 * tr * GRID_W))
    ok = jnp.asarray(np.stack(masks))
    pad = GRID_W - NA_COLS
    rp = jnp.pad(rpb.astype(F32) * LOG2E, ((0, 0), (0, 0), (pad, pad)))
    cexp = jnp.stack([rp[:, :, GRID_W - 1 - cq:2 * GRID_W - 1 - cq] for cq in range(GRID_W)], axis=2)
    off = NA_ROWS - 1 - tr
    assert off - (tr - 1) >= 0 and off + 3 * tr - 1 <= 2 * NA_ROWS - 2
    t5 = jnp.stack([cexp[:, off - rq:off - rq + 3 * tr] for rq in range(tr)], axis=1)
    tab = t5.transpose(0, 1, 3, 2, 4).reshape(n_heads, tr * GRID_W, 3 * tr * GRID_W)
    return jnp.where(ok[None], tab[:, None], NEG_INF)


def _deinterleave_cols(n_heads):
    one = np.concatenate([np.arange(0, HEAD_DIM, 2), np.arange(1, HEAD_DIM, 2)])
    return np.concatenate([h * HEAD_DIM + one for h in range(n_heads)])


def kernel(x, c, ctx, c_ctx, w_ada, b_ada, ln1_g, ln1_b, ln2_g, ln2_b, w_in_even, w_out_even, sink_logits, na_rpb, w_in_odd, w_out_odd, q_norm_g, k_norm_g, lambda_q1, lambda_k1, lambda_q2, lambda_k2, subln_g, w_router, router_bias, w_exp_gate, w_exp_up, w_exp_down):
    n_batch, seq, d = x.shape
    ctx_len = ctx.shape[1]
    depth = w_ada.shape[0]
    assert depth == 2, "one even (A||B) layer followed by one odd (C||D) layer"
    n_lat = n_batch * seq
    n_ctx = n_batch * ctx_len
    n_all = n_lat + n_ctx
    n_groups = n_batch + 1
    alpha = float((2 * depth) ** 0.25)
    heads = d // HEAD_DIM
    qh = heads // 2
    kvh = qh // 4
    gq = qh // kvh
    bh = heads - qh
    dh = (heads - qh) // 2
    moe_tm = 512
    proj_tm = min(1024, seq)
    assert seq % GRID_W == 0 and (seq // GRID_W) % B_TILE_ROWS == 0 and seq // GRID_W >= NA_ROWS
    assert n_ctx % proj_tm == 0 and n_lat % ctx_len == 0 and (2 * n_all) % moe_tm == 0
    assert ctx_len % LANES == 0 and bh % B_HEADS_PER_STEP == 0
    assert n_all <= 1 << 16, "token ids are packed as 16-bit halves for the MoE row gather"

    pad_rows = -(-n_groups // 16) * 16
    c_rows = jnp.concatenate([c, c_ctx[None], jnp.zeros((pad_rows - n_groups, d), F32)], 0)
    mod = _adaln(c_rows, w_ada, b_ada)[:, :n_groups]
    mod = mod.reshape(depth, n_groups, 6, 1, d)
    mods = [[mod[l, :, j] for j in range(6)] for l in range(depth)]

    cos2, sin2 = _rope_tables(seq, proj_tm)
    tabs = (cos2, sin2)
    x_lat, x_ctx = x.reshape(n_lat, d), ctx.reshape(n_ctx, d)
    row2 = lambda v: v.reshape(1, -1)
    ones = lambda n: jnp.ones((1, n), F32)
    qscale = lambda n: jnp.full((1, n), Q_SCALE, F32)
    w_router_t = w_router.T.astype(F32)
    router_bias_col = router_bias.reshape(-1, 1).astype(F32)
    wg_all, wu_all, wd_all = (_cast_bf16(w) for w in (w_exp_gate, w_exp_up, w_exp_down))

    def moe(h2, idx, wts, layer):
        src_packed, dst_row, gw, tile_expert, n_valid = _dispatch(idx, wts, tm=moe_tm)
        return _moe_ffn(h2, src_packed, dst_row, gw, tile_expert, n_valid, wg_all, wu_all, wd_all,
                        layer=layer, tm=moe_tm)

    sh1, sc1, g1, sh2, sc2, g2 = mods[0]
    h = _modulate(x_lat, x_ctx, sc1, sh1, seq=seq, n_groups=n_groups)
    w_in = w_in_even[0]
    na, nk, nb = qh * HEAD_DIM, kvh * HEAD_DIM, bh * HEAD_DIM
    perm_q, perm_k = _deinterleave_cols(qh), _deinterleave_cols(kvh)
    w_aq = w_in[:, :na][:, perm_q].astype(BF16)
    w_ak = w_in[:, na:na + nk][:, perm_k].astype(BF16)
    w_rest = jnp.concatenate([w_in[:, na + 2 * nk:], w_in[:, na + nk:na + 2 * nk]], 1).astype(BF16)
    rest_scale = jnp.concatenate([qscale(nb), ones(2 * nb + nk)], 1)
    kw = dict(seq=seq, n_latent=n_lat)
    aq_rot, aq_nopos = _proj(h, w_aq, qscale(na), n_rows=n_all, rope_tabs=tabs, emit_nopos=True,
                             name="proj_aq", **kw)
    ak = _proj(h, w_ak, ones(nk), n_rows=n_all, rope_tabs=tabs, name="proj_ak", **kw)
    rest = _proj(h, w_rest, rest_scale, n_rows=n_all, tn=w_rest.shape[1] // 2, name="proj_even_rest", **kw)
    sink = sink_logits[0].astype(F32)
    akw = dict(n_batch=n_batch, seq=seq, ctx_len=ctx_len, n_latent=n_lat)
    o_a = _band_attn(aq_rot, aq_nopos, ak, rest, _window_bias(seq, gq), sink, n_kv=kvh, nh=kvh, g=gq,
                     bq=A_BLOCK, q_col=0, k_col=0, v_col=3 * nb // nk, nq=A_BLOCKS_PER_STEP, name="attn_a", **akw)
    bw = B_HEADS_PER_STEP * HEAD_DIM
    o_b = _band_attn(rest, None, rest, rest, _neighbourhood_bias(na_rpb[0], seq), None, n_kv=bh,
                     nh=B_HEADS_PER_STEP, g=1, bq=B_TILE_ROWS * GRID_W, q_col=0, k_col=nb // bw,
                     v_col=2 * nb // bw, name="attn_b", **akw)
    ckw = dict(n_batch=n_batch, n_q=ctx_len, q_row0=n_lat, n_k=ctx_len, k_row0=n_lat, ctx_len=ctx_len,
               ctx_row0=n_lat)
    oc_a = _dense_attn(aq_rot, None, ak, rest, sink, n_kv=kvh, g=gq, q_col=0, k_col=0,
                       v_col=3 * nb // HEAD_DIM, name="ctx_attn_a", **ckw)
    oc_b = _dense_attn(rest, None, rest, rest, None, n_kv=bh, g=1, q_col=0, k_col=bh, v_col=2 * bh,
                       name="ctx_attn_b", **ckw)
    w_out = w_out_even[0].astype(BF16)
    x1, h2, idx, wts = _outproj_ln((o_a, oc_a), (o_b, oc_b), (x_lat, x_ctx), w_out[:na], w_out[na:], g1, sc2,
                                   sh2, row2(ln1_g[0]), row2(ln1_b[0]), w_router_t, router_bias_col,
                                   n_lat=n_lat, seq=seq, n_groups=n_groups, alpha=alpha)
    y = moe(h2, idx, wts, 0)
    nsh1, nsc1 = mods[1][0], mods[1][1]
    xt, h = _resid_ln(x1, y, g2, row2(ln2_g[0]), row2(ln2_b[0]), nsc1, nsh1, n_rows=n_all, seq=seq,
                      n_groups=n_groups, alpha=alpha)

    sh1, sc1, g1, sh2, sc2, g2 = mods[1]
    lambda_init = 0.8 - 0.6 * math.exp(-0.3 * 1)
    w_in = w_in_odd[0]
    nd = dh * 2 * HEAD_DIM
    o0 = na + 2 * nk
    perm_d = _deinterleave_cols(2 * dh)
    perm_one = _deinterleave_cols(1)
    w_cq = w_in[:, :na][:, perm_q].astype(BF16)
    w_ck = w_in[:, na:na + nk][:, perm_k].astype(BF16)
    w_dq = w_in[:, o0:o0 + nd][:, perm_d].astype(BF16)
    w_dk = w_in[:, o0 + nd:o0 + 2 * nd][:, perm_d].astype(BF16)
    w_v = jnp.concatenate([w_in[:, na + nk:o0], w_in[:, o0 + 2 * nd:]], 1).astype(BF16)
    qg = row2(q_norm_g[0][perm_one]).astype(F32)
    kg = row2(k_norm_g[0][perm_one]).astype(F32)
    cq_rot, cq_nopos = _proj(h, w_cq, qscale(na), n_rows=n_lat, gain=qg, rope_tabs=tabs, emit_nopos=True,
                             name="proj_cq", **kw)
    ck = _proj(h, w_ck, ones(nk), n_rows=n_all, gain=kg, rope_tabs=tabs, name="proj_ck", **kw)
    dq_rot, dq_nopos = _proj(h, w_dq, qscale(nd), n_rows=n_lat, rope_tabs=tabs, emit_nopos=True,
                             name="proj_dq", **kw)
    dk = _proj(h, w_dk, ones(nd), n_rows=n_all, rope_tabs=tabs, name="proj_dk", **kw)
    vv = _proj(h, w_v, ones(nk + nd), n_rows=n_all, name="proj_odd_v", **kw)
    o_c = _dense_attn(cq_rot, cq_nopos, ck, vv, None, n_batch=n_batch, n_q=seq, q_row0=0, n_k=seq, k_row0=0,
                      ctx_len=ctx_len, ctx_row0=n_lat, n_kv=kvh, g=gq, q_col=0, k_col=0, v_col=0,
                      name="attn_c")
    lam_vecs = jnp.stack([lambda_q1[0], lambda_k1[0], lambda_q2[0], lambda_k2[0]]).astype(F32)
    o_d = _diff_attn(dq_rot, dq_nopos, dk, vv, lam_vecs, row2(subln_g[0]).astype(F32), n_batch=n_batch,
                     seq=seq, ctx_len=ctx_len, n_latent=n_lat, n_heads=dh, v_col=nk // (2 * HEAD_DIM),
                     lambda_init=lambda_init)
    w_out = w_out_odd[0].astype(BF16)
    x1, h2, idx, wts = _outproj_ln((o_c, None), (o_d, None), (xt, None), w_out[:na], w_out[na:], g1, sc2,
                                   sh2, row2(ln1_g[1]), row2(ln1_b[1]), w_router_t, router_bias_col,
                                   n_lat=n_lat, seq=seq, n_groups=n_groups, alpha=alpha)
    y = moe(h2, idx, wts, 1)
    x2, _ = _resid_ln(x1, y, g2, row2(ln2_g[1]), row2(ln2_b[1]), None, None, n_rows=n_lat, seq=seq,
                      n_groups=n_groups, alpha=alpha)
    return x2.reshape(n_batch, seq, d)
```

```python
import functools
import math

import numpy as np
import jax
import jax.numpy as jnp
from jax import lax
from jax.experimental import pallas as pl
from jax.experimental.pallas import tpu as pltpu

F32 = jnp.float32
BF16 = jnp.bfloat16

HEAD_DIM = 128
GRID_W = 64
ROPE_THETA = 10000.0
SCALE = HEAD_DIM ** -0.5
NEG_INF = -1e30
A_WINDOW = 128
A_BLOCK = 128
NA_ROWS = 8
NA_COLS = 16
B_TILE_ROWS = 4
B_HEADS_PER_STEP = 8
A_BLOCKS_PER_STEP = 4
D_HEADS_PER_STEP = 1
ROW_PARTS = 2
PROJ_ROW_PARTS = 4
EPILOGUE_ROWS = 128
N_EXPERTS = 16
N_GROUPS = 4
EXPERTS_PER_GROUP = N_EXPERTS // N_GROUPS
LN_EPS = 1e-5
RMS_EPS = 1e-6
LOG2E = math.log2(math.e)
Q_SCALE = SCALE * LOG2E

V7X_VMEM_BYTES = 64 * 1024 * 1024
VMEM_LIMIT = V7X_VMEM_BYTES - 8 * 1024 * 1024
LANES = 128

NT_DIMS = (((1,), (1,)), ((), ()))


def _params(*sem):
    return pltpu.CompilerParams(dimension_semantics=sem, vmem_limit_bytes=VMEM_LIMIT)


def _adaln_kernel(c_ref, w_ref, b_ref, o_ref):
    c = c_ref[...]
    a = (c * (1.0 / (1.0 + jnp.exp(-c)))).astype(BF16)
    o_ref[...] = jnp.dot(a, w_ref[...].astype(BF16), preferred_element_type=F32) + b_ref[...]


def _adaln(c_rows, w_ada, b_ada):
    depth, d, n6 = w_ada.shape
    rows = c_rows.shape[0]
    tn = 1024
    return pl.pallas_call(
        _adaln_kernel,
        out_shape=jax.ShapeDtypeStruct((depth, rows, n6), F32),
        grid=(depth, n6 // tn),
        in_specs=[
            pl.BlockSpec((rows, d), lambda l, j: (0, 0)),
            pl.BlockSpec((None, d, tn), lambda l, j: (l, 0, j)),
            pl.BlockSpec((None, 1, tn), lambda l, j: (l, 0, j)),
        ],
        out_specs=pl.BlockSpec((None, rows, tn), lambda l, j: (l, 0, j)),
        compiler_params=_params("parallel", "parallel"),
        name="adaln",
    )(c_rows, w_ada, b_ada.reshape(depth, 1, n6))


def _two_stream_specs(tm, width, n_lat_tiles):
    return [pl.BlockSpec((tm, width), lambda i: (jnp.minimum(i, n_lat_tiles - 1), 0)),
            pl.BlockSpec((tm, width), lambda i: (jnp.maximum(i - n_lat_tiles, 0), 0))]


def _pick_stream(lat_ref, ctx_ref, n_lat_tiles):
    if ctx_ref is None:
        return lat_ref[...]
    return jnp.where(pl.program_id(0) >= n_lat_tiles, ctx_ref[...], lat_ref[...])


def _modulate_kernel(x_ref, c_ref, sc_ref, sh_ref, o_ref, *, n_lat_tiles):
    x = _pick_stream(x_ref, c_ref, n_lat_tiles)
    o_ref[...] = (x * (1.0 + sc_ref[0]) + sh_ref[0]).astype(o_ref.dtype)


def _modulate(x_lat, x_ctx, sc, sh, *, seq, n_groups):
    d = x_lat.shape[1]
    rows = x_lat.shape[0] + x_ctx.shape[0]
    tm = 512
    n_lat_tiles = x_lat.shape[0] // tm
    grp = lambda i: (jnp.minimum(i // (seq // tm), n_groups - 1), 0, 0)
    return pl.pallas_call(
        functools.partial(_modulate_kernel, n_lat_tiles=n_lat_tiles),
        out_shape=jax.ShapeDtypeStruct((rows, d), BF16),
        grid=(rows // tm,),
        in_specs=_two_stream_specs(tm, d, n_lat_tiles)
        + [pl.BlockSpec((1, 1, d), grp), pl.BlockSpec((1, 1, d), grp)],
        out_specs=pl.BlockSpec((tm, d), lambda i: (i, 0)),
        compiler_params=_params("parallel"),
        name="modulate",
    )(x_lat, x_ctx, sc, sh)


def _cast_kernel(x_ref, o_ref):
    o_ref[...] = x_ref[...].astype(o_ref.dtype)


def _cast_bf16(w):
    shape = w.shape
    w2 = w.reshape(-1, shape[-1])
    tm = min(w2.shape[0], (2 * 1024 * 1024) // shape[-1])
    out = pl.pallas_call(
        _cast_kernel,
        out_shape=jax.ShapeDtypeStruct(w2.shape, BF16),
        grid=(w2.shape[0] // tm,),
        in_specs=[pl.BlockSpec((tm, shape[-1]), lambda i: (i, 0))],
        out_specs=pl.BlockSpec((tm, shape[-1]), lambda i: (i, 0)),
        compiler_params=_params("parallel"),
        name="cast_bf16",
    )(w2)
    return out.reshape(shape)


def _proj_kernel(*refs, tn, norm, rope, emit_nopos):
    it = iter(refs)
    x_ref, w_ref, cs_ref = next(it), next(it), next(it)
    g_ref = next(it) if norm else None
    cos_ref, sin_ref = (next(it), next(it)) if rope else (None, None)
    o_ref = next(it)
    n_ref = next(it) if emit_nopos else None
    tm = x_ref.shape[0]
    n_parts = PROJ_ROW_PARTS if tm % (PROJ_ROW_PARTS * 128) == 0 else 1
    rows_per_part = tm // n_parts
    part_rows = [slice(p * rows_per_part, (p + 1) * rows_per_part) for p in range(n_parts)]
    matmul = lambda p: jnp.dot(x_ref[part_rows[p], :], w_ref[...], preferred_element_type=F32)
    acc_next = matmul(0)
    for part in range(n_parts):
        rows = part_rows[part]
        acc = acc_next
        if part + 1 < n_parts:
            acc_next = matmul(part + 1)
        for hd in range(tn // HEAD_DIM):
            sl = slice(hd * HEAD_DIM, (hd + 1) * HEAD_DIM)
            xh = acc[:, sl]
            if norm:
                xh = xh * lax.rsqrt(jnp.mean(xh * xh, axis=-1, keepdims=True) + RMS_EPS) * g_ref[...]
            cs = cs_ref[:, sl]
            if rope:
                rot = xh * cos_ref[rows, :] + pltpu.roll(xh, HEAD_DIM // 2, 1) * sin_ref[rows, :]
                o_ref[rows, sl] = (rot * cs).astype(o_ref.dtype)
                if emit_nopos:
                    n_ref[rows, sl] = (xh * cs).astype(n_ref.dtype)
            else:
                o_ref[rows, sl] = (xh * cs).astype(o_ref.dtype)


def _proj(x, w, col_scale, *, n_rows, seq, n_latent, tn=None, gain=None, rope_tabs=None,
          emit_nopos=False, name="proj"):
    d = x.shape[1]
    nc = w.shape[1]
    tm = min(1024, seq)
    tn = nc if tn is None else tn
    norm, rope = gain is not None, rope_tabs is not None
    in_specs = [pl.BlockSpec((tm, d), lambda i, j: (i, 0)),
                pl.BlockSpec((d, tn), lambda i, j: (0, j)),
                pl.BlockSpec((1, tn), lambda i, j: (0, j))]
    args = [x, w, col_scale]
    if norm:
        in_specs.append(pl.BlockSpec((1, HEAD_DIM), lambda i, j: (0, 0)))
        args.append(gain)
    if rope:
        per_seq = seq // tm
        tab = lambda i, j: (jnp.where(i < n_latent // tm, i % per_seq, per_seq), 0)
        in_specs += [pl.BlockSpec((tm, HEAD_DIM), tab)] * 2
        args += list(rope_tabs)
    n_out = 2 if emit_nopos else 1
    out_shape = [jax.ShapeDtypeStruct((n_rows, nc), BF16)] * n_out
    out_specs = [pl.BlockSpec((tm, tn), lambda i, j: (i, j))] * n_out
    res = pl.pallas_call(
        functools.partial(_proj_kernel, tn=tn, norm=norm, rope=rope, emit_nopos=emit_nopos),
        out_shape=out_shape,
        grid=(n_rows // tm, nc // tn),
        in_specs=in_specs,
        out_specs=out_specs,
        compiler_params=_params("parallel", "parallel"),
        name=name,
    )(*args)
    return res if emit_nopos else res[0]


def _stack_heads(x, g):
    if g == 1:
        return x
    return jnp.concatenate([x[:, i * HEAD_DIM:(i + 1) * HEAD_DIM] for i in range(g)], axis=0)


def _unstack_store(o_ref, o, g, t, col0=0):
    for i in range(g):
        o_ref[:, col0 + i * HEAD_DIM:col0 + (i + 1) * HEAD_DIM] = o[i * t:(i + 1) * t].astype(o_ref.dtype)


def _sink_column(sink_ref, first_head, g, t):
    cols = [jnp.full((t, 1), sink_ref[first_head + i] * LOG2E, F32) for i in range(g)]
    return cols[0] if g == 1 else jnp.concatenate(cols, axis=0)


def _with_ones(v):
    return jnp.concatenate([v, jnp.ones_like(v)], axis=1)


def _row_max(pieces):
    cols = [p[:, c:c + LANES] for p in pieces for c in range(0, p.shape[1], LANES)]
    return jnp.max(functools.reduce(jnp.maximum, cols), axis=-1, keepdims=True)


def _online_update(s, v_aug, m, acc):
    m_new = jnp.maximum(m, jnp.max(s, axis=-1, keepdims=True))
    p = jnp.exp2(s - m_new)
    acc = jnp.exp2(m - m_new) * acc + jnp.dot(p.astype(BF16), v_aug, preferred_element_type=F32)
    return m_new, acc


def _band_kernel(*refs, nq, nh, g, bq, has_sink, same_qc, shared_bias):
    it = iter(refs)
    sink_ref = next(it) if has_sink else None
    q_ref = next(it)
    qc_ref = q_ref if same_qc else next(it)
    k_refs = [next(it) for _ in range(nq + 2)]
    v_refs = [next(it) for _ in range(nq + 2)]
    kx_ref, vx_ref = next(it), next(it)
    bias_refs = [next(it) for _ in range(nq)]
    o_ref = next(it)
    hw = g * HEAD_DIM
    chains = [(qb, h) for qb in range(nq) for h in range(nh)]

    def scores(qb, h):
        rows = slice(qb * bq, (qb + 1) * bq)
        qs = slice(h * hw, (h + 1) * hw)
        ks = slice(h * HEAD_DIM, (h + 1) * HEAD_DIM)
        q = _stack_heads(q_ref[rows, qs], g)
        qc = q if same_qc else _stack_heads(qc_ref[rows, qs], g)
        bias = bias_refs[qb].at[0 if shared_bias else h]
        s = [lax.dot_general(q, k_refs[qb + j][:, ks], NT_DIMS, preferred_element_type=F32)
             + bias[:, j * bq:(j + 1) * bq] for j in range(3)]
        s.append(lax.dot_general(qc, kx_ref[:, ks], NT_DIMS, preferred_element_type=F32))
        return s

    def probs(qb, h, s):
        m = _row_max(s)
        sink = None
        if has_sink:
            sink = _sink_column(sink_ref, (pl.program_id(0) * nh + h) * g, g, bq)
            m = jnp.maximum(m, sink)
        return [jnp.exp2(sj - m).astype(BF16) for sj in s], (None if sink is None else jnp.exp2(sink - m))

    def output(qb, h, p, sink_term):
        rows = slice(qb * bq, (qb + 1) * bq)
        ks = slice(h * HEAD_DIM, (h + 1) * HEAD_DIM)
        vs = [v_refs[qb + j][:, ks] for j in range(3)] + [vx_ref[:, ks]]
        o = functools.reduce(jnp.add, [jnp.dot(pj, _with_ones(vj), preferred_element_type=F32)
                                       for pj, vj in zip(p, vs)])
        den = o[:, HEAD_DIM:]
        if sink_term is not None:
            den = den + sink_term
        _unstack_store(o_ref.at[rows, :], o[:, :HEAD_DIM] / den, g, bq, col0=h * hw)

    s_of, p_of = {}, {}
    for step in range(len(chains) + 2):
        if step < len(chains):
            s_of[step] = scores(*chains[step])
        if 0 <= step - 1 < len(chains):
            p_of[step - 1] = probs(*chains[step - 1], s_of.pop(step - 1))
        if 0 <= step - 2 < len(chains):
            output(*chains[step - 2], *p_of.pop(step - 2))


def _band_attn(q, qc, k, v, bias, sink, *, n_batch, seq, ctx_len, n_latent, n_kv, nh, g, bq,
               q_col, k_col, v_col, name, nq=1):
    nblk = seq // bq
    nstep = nblk // nq
    ctx_blk = n_latent // ctx_len
    shared_bias = bias.shape[0] == 1
    has_sink, same_qc = sink is not None, qc is None
    qw, kw = nh * g * HEAD_DIM, nh * HEAD_DIM

    def q_map(hg, i, b):
        return (b * nstep + i, q_col + hg)

    def kv_map(col, off):
        return lambda hg, i, b: (b * nblk + jnp.clip(i * nq + off, 0, nblk - 1), col + hg)

    def ctx_map(col):
        return lambda hg, i, b: (ctx_blk + b, col + hg)

    def bias_map(qb):
        def index(hg, i, b):
            blk = i * nq + qb
            case = jnp.where(blk == 0, 0, jnp.where(blk == nblk - 1, 2, 1))
            return (0 if shared_bias else hg, case, 0, 0)
        return index

    in_specs, args = [], []
    if has_sink:
        in_specs.append(pl.BlockSpec(memory_space=pltpu.SMEM))
        args.append(sink)
    in_specs.append(pl.BlockSpec((nq * bq, qw), q_map))
    args.append(q)
    if not same_qc:
        in_specs.append(pl.BlockSpec((nq * bq, qw), q_map))
        args.append(qc)
    offsets = range(-1, nq + 1)
    in_specs += [pl.BlockSpec((bq, kw), kv_map(k_col, off)) for off in offsets]
    args += [k] * len(offsets)
    in_specs += [pl.BlockSpec((bq, kw), kv_map(v_col, off)) for off in offsets]
    args += [v] * len(offsets)
    in_specs += [pl.BlockSpec((ctx_len, kw), ctx_map(k_col)), pl.BlockSpec((ctx_len, kw), ctx_map(v_col))]
    args += [k, v]
    in_specs += [pl.BlockSpec((1 if shared_bias else nh, None, g * bq, 3 * bq), bias_map(qb)) for qb in range(nq)]
    args += [bias] * nq
    return pl.pallas_call(
        functools.partial(_band_kernel, nq=nq, nh=nh, g=g, bq=bq, has_sink=has_sink, same_qc=same_qc,
                          shared_bias=shared_bias),
        out_shape=jax.ShapeDtypeStruct((n_latent, n_kv * g * HEAD_DIM), BF16),
        grid=(n_kv // nh, nstep, n_batch),
        in_specs=in_specs,
        out_specs=pl.BlockSpec((nq * bq, qw), lambda hg, i, b: (b * nstep + i, hg)),
        compiler_params=_params("parallel", "parallel", "parallel"),
        name=name,
    )(*args)


def _dense_kernel(*refs, nh, g, tq, tk, n_k, has_ctx, has_sink):
    it = iter(refs)
    sink_ref = next(it) if has_sink else None
    q_ref = next(it)
    qc_ref = next(it) if has_ctx else None
    k_ref, v_ref = next(it), next(it)
    kx_ref, vx_ref = (next(it), next(it)) if has_ctx else (None, None)
    o_ref = next(it)
    hw = g * HEAD_DIM
    rows = g * tq
    for h in range(nh):
        qs = slice(h * hw, (h + 1) * hw)
        ks = slice(h * HEAD_DIM, (h + 1) * HEAD_DIM)
        q = _stack_heads(q_ref[:, qs], g)
        m = jnp.full((rows, 1), NEG_INF, F32)
        acc = jnp.zeros((rows, 2 * HEAD_DIM), F32)
        chunks = [(q, k_ref.at[c * tk:(c + 1) * tk, ks], v_ref.at[c * tk:(c + 1) * tk, ks])
                  for c in range(n_k // tk)]
        if has_ctx:
            chunks.append((_stack_heads(qc_ref[:, qs], g), kx_ref.at[:, ks], vx_ref.at[:, ks]))
        score = lambda ch: lax.dot_general(ch[0], ch[1][...], NT_DIMS, preferred_element_type=F32)
        s_next = score(chunks[0])
        for c, ch in enumerate(chunks):
            s = s_next
            if c + 1 < len(chunks):
                s_next = score(chunks[c + 1])
            m, acc = _online_update(s, _with_ones(ch[2][...]), m, acc)
        num, den = acc[:, :HEAD_DIM], acc[:, HEAD_DIM:]
        if has_sink:
            sink = _sink_column(sink_ref, (pl.program_id(1) * nh + h) * g, g, tq)
            m_new = jnp.maximum(m, sink)
            alpha = jnp.exp2(m - m_new)
            num = alpha * num
            den = alpha * den + jnp.exp2(sink - m_new)
        _unstack_store(o_ref, num / den, g, tq, col0=h * hw)


def _dense_attn(q, qc, k, v, sink, *, n_batch, n_q, q_row0, n_k, k_row0, ctx_len, ctx_row0,
                n_kv, g, q_col, k_col, v_col, name, nh=1):
    tq = min(256, n_q)
    tk = min(512, n_k)
    nq_blk = n_q // tq
    has_ctx, has_sink = qc is not None, sink is not None
    qw, kw = nh * g * HEAD_DIM, nh * HEAD_DIM

    def q_map(b, h, i):
        return (q_row0 // tq + b * nq_blk + i, q_col + h)

    in_specs, args = [], []
    if has_sink:
        in_specs.append(pl.BlockSpec(memory_space=pltpu.SMEM))
        args.append(sink)
    in_specs.append(pl.BlockSpec((tq, qw), q_map))
    args.append(q)
    if has_ctx:
        in_specs.append(pl.BlockSpec((tq, qw), q_map))
        args.append(qc)
    in_specs += [pl.BlockSpec((n_k, kw), lambda b, h, i: (k_row0 // n_k + b, k_col + h)),
                 pl.BlockSpec((n_k, kw), lambda b, h, i: (k_row0 // n_k + b, v_col + h))]
    args += [k, v]
    if has_ctx:
        in_specs += [pl.BlockSpec((ctx_len, kw), lambda b, h, i: (ctx_row0 // ctx_len + b, k_col + h)),
                     pl.BlockSpec((ctx_len, kw), lambda b, h, i: (ctx_row0 // ctx_len + b, v_col + h))]
        args += [k, v]
    return pl.pallas_call(
        functools.partial(_dense_kernel, nh=nh, g=g, tq=tq, tk=tk, n_k=n_k, has_ctx=has_ctx,
                          has_sink=has_sink),
        out_shape=jax.ShapeDtypeStruct((n_batch * n_q, n_kv * g * HEAD_DIM), BF16),
        grid=(n_batch, n_kv // nh, nq_blk),
        in_specs=in_specs,
        out_specs=pl.BlockSpec((tq, qw), lambda b, h, i: (b * nq_blk + i, h)),
        compiler_params=_params("parallel", "parallel", "parallel"),
        name=name,
    )(*args)


def _diff_update(s, v, m, l, acc):
    m_new = jnp.maximum(m, jnp.max(s, axis=-1, keepdims=True))
    alpha = jnp.exp2(m - m_new)
    p = jnp.exp2(s - m_new)
    l = alpha * l + jnp.sum(p, axis=-1, keepdims=True)
    acc = alpha * acc + jnp.dot(p.astype(BF16), v, preferred_element_type=F32)
    return m_new, l, acc


def _diff_kernel(lam_ref, g_ref, q_ref, qc_ref, k_ref, v_ref, kx_ref, vx_ref, o_ref, *,
                 nh, tq, tk, n_k, lambda_init):
    dv = 2 * HEAD_DIM
    lam = (jnp.exp(jnp.sum(lam_ref[0:1, :] * lam_ref[1:2, :], axis=1, keepdims=True))
           - jnp.exp(jnp.sum(lam_ref[2:3, :] * lam_ref[3:4, :], axis=1, keepdims=True)) + lambda_init)
    maps = [(h, t) for h in range(nh) for t in range(2)]
    qk_cols = {(h, t): slice(h * dv + t * HEAD_DIM, h * dv + (t + 1) * HEAD_DIM) for h, t in maps}
    v_cols = {h: slice(h * dv, (h + 1) * dv) for h in range(nh)}
    state = {mp: (jnp.full((tq, 1), NEG_INF, F32), jnp.zeros((tq, 1), F32), jnp.zeros((tq, dv), F32))
             for mp in maps}
    for c in range(n_k // tk):
        rows = slice(c * tk, (c + 1) * tk)
        for mp in maps:
            s = lax.dot_general(q_ref[:, qk_cols[mp]], k_ref[rows, qk_cols[mp]], NT_DIMS,
                                preferred_element_type=F32)
            state[mp] = _diff_update(s, v_ref[rows, v_cols[mp[0]]], *state[mp])
    for h in range(nh):
        outs = []
        for t in range(2):
            mp = (h, t)
            s = lax.dot_general(qc_ref[:, qk_cols[mp]], kx_ref[:, qk_cols[mp]], NT_DIMS,
                                preferred_element_type=F32)
            _, l, acc = _diff_update(s, vx_ref[:, v_cols[h]], *state[mp])
            outs.append(acc / l)
        o = outs[0] - lam * outs[1]
        o = o * lax.rsqrt(jnp.mean(o * o, axis=-1, keepdims=True) + RMS_EPS) * g_ref[...]
        o_ref[:, v_cols[h]] = (o * (1.0 - lambda_init)).astype(o_ref.dtype)


def _diff_attn(q, qc, k, v, lam_vecs, subln_g, *, n_batch, seq, ctx_len, n_latent, n_heads,
               lambda_init):
    nh = D_HEADS_PER_STEP
    tq = min(512, seq)
    tk = min(512, seq)
    dv = 2 * HEAD_DIM
    nq_blk = seq // tq
    ctx_blk = n_latent // ctx_len
    q_map = lambda b, h, i: (b * nq_blk + i, h)
    return pl.pallas_call(
        functools.partial(_diff_kernel, nh=nh, tq=tq, tk=tk, n_k=seq, lambda_init=lambda_init),
        out_shape=jax.ShapeDtypeStruct((n_latent, n_heads * dv), BF16),
        grid=(n_batch, n_heads // nh, nq_blk),
        in_specs=[
            pl.BlockSpec((4, HEAD_DIM), lambda b, h, i: (0, 0)),
            pl.BlockSpec((1, dv), lambda b, h, i: (0, 0)),
            pl.BlockSpec((tq, nh * dv), q_map),
            pl.BlockSpec((tq, nh * dv), q_map),
            pl.BlockSpec((seq, nh * dv), lambda b, h, i: (b, h)),
            pl.BlockSpec((seq, nh * dv), lambda b, h, i: (b, h)),
            pl.BlockSpec((ctx_len, nh * dv), lambda b, h, i: (ctx_blk + b, h)),
            pl.BlockSpec((ctx_len, nh * dv), lambda b, h, i: (ctx_blk + b, h)),
        ],
        out_specs=pl.BlockSpec((tq, nh * dv), q_map),
        compiler_params=_params("parallel", "parallel", "parallel"),
        name="diff_attn",
    )(lam_vecs, subln_g, q, qc, k, v, k, v)


def _layer_norm(z, g, b):
    mu = jnp.mean(z, axis=-1, keepdims=True)
    zc = z - mu
    var = jnp.mean(zc * zc, axis=-1, keepdims=True)
    return zc * lax.rsqrt(var + LN_EPS) * g + b


def _split_bf16(x):
    hi = x.astype(BF16)
    return hi, (x - hi.astype(F32)).astype(BF16)


def _first_argmax(vals, idx, width):
    m = jnp.max(vals, axis=0, keepdims=True)
    first = jnp.min(jnp.where(vals == m, idx, float(width)), axis=0, keepdims=True)
    return m, first


def _route(logits, bias):
    e, t = logits.shape
    scores = 1.0 / (1.0 + jnp.exp(-logits))
    biased = scores + bias
    row_i = lax.broadcasted_iota(jnp.int32, (e, t), 0)
    grp = lax.shift_right_logical(row_i, int(math.log2(EXPERTS_PER_GROUP)))
    row = row_i.astype(F32)
    neg = -jnp.inf
    best_score, best = None, None
    for gi in range(N_GROUPS):
        vg = jnp.where(grp == gi, biased, neg)
        m1, i1 = _first_argmax(vg, row, e)
        m2 = jnp.max(jnp.where(row == i1, neg, vg), axis=0, keepdims=True)
        gs = m1 + m2
        if gi == 0:
            best_score, best = gs, jnp.zeros((1, t), jnp.int32)
        else:
            upd = gs > best_score
            best = jnp.where(upd, gi, best)
            best_score = jnp.where(upd, gs, best_score)
    masked = jnp.where(grp == best, biased, neg)
    _, e1 = _first_argmax(masked, row, e)
    _, e2 = _first_argmax(jnp.where(row == e1, neg, masked), row, e)
    w1 = jnp.sum(jnp.where(row == e1, scores, 0.0), axis=0, keepdims=True)
    w2 = jnp.sum(jnp.where(row == e2, scores, 0.0), axis=0, keepdims=True)
    den = w1 + w2
    two = lax.broadcasted_iota(jnp.int32, (2, t), 0)
    return jnp.where(two == 0, e1, e2).astype(jnp.int32), jnp.where(two == 0, w1 / den, w2 / den)


def _outproj_kernel(*refs, alpha, two_streams, n_lat_tiles):
    it = iter(refs)
    xa_ref, xa_ctx = next(it), (next(it) if two_streams else None)
    xb_ref, xb_ctx = next(it), (next(it) if two_streams else None)
    xres_ref, xres_ctx = next(it), (next(it) if two_streams else None)
    (wa_ref, wb_ref, g1_ref, sc2_ref, sh2_ref, lng_ref, lnb_ref, wrt_ref, rb_ref,
     x1_ref, h2_ref, idx_ref, wts_ref) = it
    w_hi, w_lo = _split_bf16(wrt_ref[...])
    w_cat = jnp.concatenate([w_hi, w_lo], axis=0)
    tm = x1_ref.shape[0]
    rows_per_part = tm // ROW_PARTS

    def pick(lat, ctx, part):
        rows = slice(part * rows_per_part, (part + 1) * rows_per_part)
        return _pick_stream(lat.at[rows, :], None if ctx is None else ctx.at[rows, :], n_lat_tiles)

    def matmul(part):
        return (jnp.dot(pick(xa_ref, xa_ctx, part), wa_ref[...], preferred_element_type=F32)
                + jnp.dot(pick(xb_ref, xb_ctx, part), wb_ref[...], preferred_element_type=F32))

    lookahead = not two_streams
    y_next = matmul(0)
    for part in range(ROW_PARTS):
        y_all = y_next
        if part + 1 < ROW_PARTS and lookahead:
            y_next = matmul(part + 1)
        xres_all = pick(xres_ref, xres_ctx, part)
        n_sub = rows_per_part // EPILOGUE_ROWS
        for sub in range(n_sub):
            sr = slice(sub * EPILOGUE_ROWS, (sub + 1) * EPILOGUE_ROWS)
            orow = slice(part * rows_per_part + sub * EPILOGUE_ROWS,
                         part * rows_per_part + (sub + 1) * EPILOGUE_ROWS)
            x1 = _layer_norm(alpha * xres_all[sr] + g1_ref[0] * y_all[sr], lng_ref[...], lnb_ref[...])
            x1_ref[orow, :] = x1
            h2 = x1 * (1.0 + sc2_ref[0]) + sh2_ref[0]
            h2_ref[orow, :] = h2
            h_hi, h_lo = _split_bf16(h2)
            both = lax.dot_general(w_cat, h_hi, NT_DIMS, preferred_element_type=F32)
            logits = (both[:N_EXPERTS] + both[N_EXPERTS:]
                      + lax.dot_general(w_hi, h_lo, NT_DIMS, preferred_element_type=F32))
            idx, wts = _route(logits, rb_ref[...])
            idx_ref[:, orow] = idx
            wts_ref[:, orow] = wts
        if part + 1 < ROW_PARTS and not lookahead:
            y_next = matmul(part + 1)


def _outproj_ln(xa, xb, xres, wa, wb, g1, sc2, sh2, ln_g, ln_b, w_router_t, router_bias, *,
                n_lat, seq, n_groups, alpha):
    two_streams = xa[1] is not None
    d = xres[0].shape[1]
    ka, kb = xa[0].shape[1], xb[0].shape[1]
    tm = 256 * ROW_PARTS
    n_lat_tiles = n_lat // tm
    n_rows = n_lat + (xres[1].shape[0] if two_streams else 0)
    row = lambda i: (i, 0)
    col = lambda i: (0, i)
    const = lambda i: (0, 0)
    grp = lambda i: (jnp.minimum(i // (seq // tm), n_groups - 1), 0, 0)
    in_specs, args = [], []
    for pair, width in ((xa, ka), (xb, kb), (xres, d)):
        if two_streams:
            in_specs += _two_stream_specs(tm, width, n_lat_tiles)
            args += list(pair)
        else:
            in_specs.append(pl.BlockSpec((tm, width), row))
            args.append(pair[0])
    once = pl.Buffered(1)
    in_specs += [pl.BlockSpec((ka, d), const, pipeline_mode=once), pl.BlockSpec((kb, d), const, pipeline_mode=once),
                 pl.BlockSpec((1, 1, d), grp), pl.BlockSpec((1, 1, d), grp), pl.BlockSpec((1, 1, d), grp),
                 pl.BlockSpec((1, d), const), pl.BlockSpec((1, d), const),
                 pl.BlockSpec((N_EXPERTS, d), const), pl.BlockSpec((N_EXPERTS, 1), const)]
    args += [wa, wb, g1, sc2, sh2, ln_g, ln_b, w_router_t, router_bias]
    return pl.pallas_call(
        functools.partial(_outproj_kernel, alpha=alpha, two_streams=two_streams, n_lat_tiles=n_lat_tiles),
        out_shape=[jax.ShapeDtypeStruct((n_rows, d), F32), jax.ShapeDtypeStruct((n_rows, d), F32),
                   jax.ShapeDtypeStruct((2, n_rows), jnp.int32), jax.ShapeDtypeStruct((2, n_rows), F32)],
        grid=(n_rows // tm,),
        in_specs=in_specs,
        out_specs=[pl.BlockSpec((tm, d), row), pl.BlockSpec((tm, d), row),
                   pl.BlockSpec((2, tm), col), pl.BlockSpec((2, tm), col)],
        compiler_params=_params("parallel"),
        name="outproj_ln",
    )(*args)


def _pack_bf16_pairs(x):
    n = x.shape[1] // 2
    lo = pltpu.bitcast(x[:, :n].astype(BF16).astype(F32), jnp.uint32)
    hi = pltpu.bitcast(x[:, n:].astype(BF16).astype(F32), jnp.uint32)
    return jnp.bitwise_or(lax.shift_right_logical(lo, jnp.uint32(16)), hi)


def _unpack_bf16_pairs(w):
    lo = pltpu.bitcast(lax.shift_left(w, jnp.uint32(16)), F32)
    hi = pltpu.bitcast(jnp.bitwise_and(w, jnp.uint32(0xFFFF0000)), F32)
    return jnp.concatenate([lo, hi], axis=1)


def _moe_kernel(te_ref, nv_ref, src_ref, dst_ref, h_hbm, gw_ref, wg_ref, wu_ref, wd_ref, y_hbm,
                xbuf, obuf, gsem, ssem, *, tm, ff_chunks):
    i = pl.program_id(0)
    n = pl.num_programs(0)
    nv = nv_ref[0]

    def start_gather(tile, s):
        base = tile * (tm // 2)
        for r in range(tm):
            word = src_ref[base + r // 2]
            tok = jnp.bitwise_and(word, 0xFFFF) if r % 2 == 0 else lax.shift_right_logical(word, 16)
            pltpu.make_async_copy(h_hbm.at[pl.ds(tok, 1)], xbuf.at[s, pl.ds(r, 1)],
                                  gsem.at[s]).start(priority=r % 2)

    def wait_gather(s):
        pltpu.make_async_copy(h_hbm.at[pl.ds(0, tm)], xbuf.at[s], gsem.at[s]).wait()

    def start_scatter(tile, s):
        base = tile * tm
        for r in range(tm):
            pltpu.make_async_copy(obuf.at[s, pl.ds(r, 1)], y_hbm.at[pl.ds(dst_ref[base + r], 1)],
                                  ssem.at[s]).start(priority=r % 2)

    def wait_scatter(s):
        pltpu.make_async_copy(obuf.at[s], y_hbm.at[pl.ds(0, tm)], ssem.at[s]).wait()

    @pl.when(i == 0)
    def _():
        start_gather(0, 0)
        obuf[...] = jnp.zeros_like(obuf)
        pad0 = y_hbm.shape[0] - 2 * tm
        fills = [pltpu.make_async_copy(obuf.at[s], y_hbm.at[pl.ds(pad0 + s * tm, tm)], ssem.at[s])
                 for s in range(2)]
        for f in fills:
            f.start()
        for f in fills:
            f.wait()

    def step(slot):
        wait_gather(slot)

        @pl.when(i >= 2)
        def _():
            wait_scatter(slot)

        start_gather(jnp.minimum(i + 1, n - 1), 1 - slot)
        x = xbuf[slot].astype(BF16)
        fc = wg_ref.shape[1] // ff_chunks
        y = None
        for c in range(ff_chunks):
            cs = slice(c * fc, (c + 1) * fc)
            gate = jnp.dot(x, wg_ref[:, cs], preferred_element_type=F32)
            up = jnp.dot(x, wu_ref[:, cs], preferred_element_type=F32)
            act = (gate * (1.0 / (1.0 + jnp.exp(-gate))) * up).astype(BF16)
            part = jnp.dot(act, wd_ref[cs, :], preferred_element_type=F32)
            y = part if y is None else y + part
        obuf[slot] = _pack_bf16_pairs(y * gw_ref[...])
        start_scatter(i, slot)

    for slot in range(2):
        @pl.when(jnp.logical_and(i < nv, jnp.bitwise_and(i, 1) == slot))
        def _():
            step(slot)

    @pl.when(i == n - 1)
    def _():
        wait_gather(jnp.bitwise_and(nv, 1))

        @pl.when(nv >= 1)
        def _():
            wait_scatter(jnp.bitwise_and(nv - 1, 1))

        @pl.when(nv >= 2)
        def _():
            wait_scatter(jnp.bitwise_and(nv, 1))


def _moe_ffn(h2, src_packed, dst_row, gw, tile_expert, n_valid, w_gate, w_up, w_down, *, layer, tm):
    t, d = h2.shape
    ff = w_gate.shape[3]
    n_tiles = dst_row.shape[0] // tm
    return pl.pallas_call(
        functools.partial(_moe_kernel, tm=tm, ff_chunks=2),
        out_shape=jax.ShapeDtypeStruct((2 * t + 2 * tm, d // 2), jnp.uint32),
        grid_spec=pltpu.PrefetchScalarGridSpec(
            num_scalar_prefetch=4,
            grid=(n_tiles,),
            in_specs=[pl.BlockSpec(memory_space=pl.ANY),
                      pl.BlockSpec((tm, 1), lambda i, te, nv, src, dst: (i, 0)),
                      pl.BlockSpec((None, None, d, ff), lambda i, te, nv, src, dst: (layer, te[i], 0, 0)),
                      pl.BlockSpec((None, None, d, ff), lambda i, te, nv, src, dst: (layer, te[i], 0, 0)),
                      pl.BlockSpec((None, None, ff, d), lambda i, te, nv, src, dst: (layer, te[i], 0, 0))],
            out_specs=pl.BlockSpec(memory_space=pl.ANY),
            scratch_shapes=[pltpu.VMEM((2, tm, d), F32), pltpu.VMEM((2, tm, d // 2), jnp.uint32),
                            pltpu.SemaphoreType.DMA((2,)), pltpu.SemaphoreType.DMA((2,))],
        ),
        compiler_params=_params("arbitrary"),
        name="moe_ffn",
    )(tile_expert, n_valid, src_packed, dst_row, h2, gw, w_gate, w_up, w_down)


def _take(x, idx):
    return x.at[idx].get(mode="promise_in_bounds")


def _dispatch(idx, wts, *, tm):
    t = idx.shape[1]
    n_pairs = 2 * t
    n_tiles = n_pairs // tm + N_EXPERTS
    e_flat = idx.reshape(n_pairs)
    w_flat = wts.reshape(n_pairs)
    pair_bits = max(1, (n_pairs - 1).bit_length())
    keys = jnp.bitwise_or(jnp.left_shift(e_flat, pair_bits), jnp.arange(n_pairs, dtype=jnp.int32))
    order = jnp.bitwise_and(lax.sort(keys), (1 << pair_bits) - 1)
    experts = jnp.arange(N_EXPERTS, dtype=jnp.int32)
    counts = jnp.sum(e_flat[None, :] == experts[:, None], axis=1).astype(jnp.int32)
    start = jnp.cumsum(counts) - counts
    padded = ((counts + tm - 1) // tm) * tm
    pend = jnp.cumsum(padded)
    pstart = pend - padded
    tile_pos = jnp.arange(n_tiles, dtype=jnp.int32) * tm
    tile_expert = jnp.minimum(jnp.sum(tile_pos[:, None] >= pend[None, :], axis=1), N_EXPERTS - 1).astype(jnp.int32)
    local = (tile_pos - pstart[tile_expert])[:, None] + jnp.arange(tm, dtype=jnp.int32)[None, :]
    valid = local < counts[tile_expert][:, None]
    sorted_pos = jnp.clip(start[tile_expert][:, None] + local, 0, n_pairs - 1)
    src_pair = _take(order, sorted_pos)
    src_tok = jnp.where(valid, jnp.where(src_pair >= t, src_pair - t, src_pair), 0)
    pad_row = n_pairs + (tile_pos[:, None] + jnp.arange(tm, dtype=jnp.int32)[None, :]) % (2 * tm)
    dst_row = jnp.where(valid, src_pair, pad_row)
    gw = jnp.where(valid, _take(w_flat, src_pair), 0.0).astype(F32)
    src_packed = jnp.bitwise_or(src_tok[:, 0::2], jnp.left_shift(src_tok[:, 1::2], 16))
    n_valid = (pend[-1] // tm).astype(jnp.int32).reshape(1)
    return src_packed.reshape(-1), dst_row.reshape(-1), gw.reshape(-1, 1), tile_expert, n_valid


def _resid_ln_kernel(*refs, alpha, emit_h):
    it = iter(refs)
    x_ref, f0_ref, f1_ref, g_ref, lng_ref, lnb_ref = (next(it) for _ in range(6))
    sc_ref, sh_ref = (next(it), next(it)) if emit_h else (None, None)
    o_ref = next(it)
    f = _unpack_bf16_pairs(f0_ref[...]) + _unpack_bf16_pairs(f1_ref[...])
    x2 = _layer_norm(alpha * x_ref[...] + g_ref[0] * f, lng_ref[...], lnb_ref[...])
    o_ref[...] = x2
    if emit_h:
        h_ref = next(it)
        h_ref[...] = (x2 * (1.0 + sc_ref[0]) + sh_ref[0]).astype(h_ref.dtype)


def _resid_ln(x, y, gate, ln_g, ln_b, next_sc, next_sh, *, n_rows, seq, n_groups, alpha):
    d = x.shape[1]
    tm = 512
    emit_h = next_sc is not None
    row = lambda i: (i, 0)
    const = lambda i: (0, 0)
    grp = lambda i: (jnp.minimum(i // (seq // tm), n_groups - 1), 0, 0)
    in_specs = [pl.BlockSpec((tm, d), row), pl.BlockSpec((tm, d // 2), row),
                pl.BlockSpec((tm, d // 2), lambda i: (n_rows // tm + i, 0)), pl.BlockSpec((1, 1, d), grp),
                pl.BlockSpec((1, d), const), pl.BlockSpec((1, d), const)]
    args = [x, y, y, gate, ln_g, ln_b]
    out_shape = [jax.ShapeDtypeStruct((n_rows, d), F32)]
    out_specs = [pl.BlockSpec((tm, d), row)]
    if emit_h:
        in_specs += [pl.BlockSpec((1, 1, d), grp)] * 2
        args += [next_sc, next_sh]
        out_shape.append(jax.ShapeDtypeStruct((n_rows, d), BF16))
        out_specs.append(pl.BlockSpec((tm, d), row))
    res = pl.pallas_call(
        functools.partial(_resid_ln_kernel, alpha=alpha, emit_h=emit_h),
        out_shape=out_shape,
        grid=(n_rows // tm,),
        in_specs=in_specs,
        out_specs=out_specs,
        compiler_params=_params("parallel"),
        name="resid_ln",
    )(*args)
    return res if emit_h else (res[0], None)


def _rope_tables(seq, tm):
    t = jnp.arange(seq, dtype=jnp.int32)
    n_freq = HEAD_DIM // 4
    inv = ROPE_THETA ** (-jnp.arange(n_freq, dtype=F32) / n_freq)
    row = (t // GRID_W).astype(F32)
    col = (t % GRID_W).astype(F32)
    ang = jnp.concatenate([row[:, None] * inv[None], col[:, None] * inv[None]], -1)
    cos, sin = jnp.cos(ang), jnp.sin(ang)
    cos2 = jnp.concatenate([cos, cos], -1)
    sin2 = jnp.concatenate([-sin, sin], -1)
    cos2 = jnp.concatenate([cos2, jnp.ones((tm, HEAD_DIM), F32)], 0)
    sin2 = jnp.concatenate([sin2, jnp.zeros((tm, HEAD_DIM), F32)], 0)
    return cos2, sin2


def _window_bias(seq, g):
    bq = A_BLOCK
    nblk = seq // bq
    rel = np.arange(3 * bq)[None, :] - np.arange(bq)[:, None]
    band = (rel >= bq - A_WINDOW) & (rel <= bq + A_WINDOW)
    tabs = []
    for blk in (0, min(1, nblk - 1), nblk - 1):
        kpos = blk * bq - bq + np.arange(3 * bq)
        ok = band & ((kpos >= 0) & (kpos < seq))[None, :]
        tabs.append(np.tile(np.where(ok, 0.0, NEG_INF).astype(np.float32), (g, 1)))
    return jnp.asarray(np.stack(tabs)[None])


def _neighbourhood_bias(rpb, seq):
    n_heads = rpb.shape[0]
    rows = seq // GRID_W
    kh, kw = min(NA_ROWS, rows), NA_COLS
    tr = B_TILE_ROWS
    nblk = rows // tr
    col = np.arange(GRID_W)
    col_start = np.clip(col - kw // 2, 0, GRID_W - kw)
    col_ok = (col[None, :] >= col_start[:, None]) & (col[None, :] < col_start[:, None] + kw)
    assert np.all(np.abs(col[None, :] - col[:, None])[col_ok] <= NA_COLS - 1)
    masks = []
    for blk in (0, min(1, nblk - 1), nblk - 1):
        r = blk * tr + np.arange(tr)
        r0 = np.clip(r - kh // 2, 0, rows - kh)
        krow = (blk - 1) * tr + np.arange(3 * tr)
        row_ok = (krow[None, :] >= r0[:, None]) & (krow[None, :] < r0[:, None] + kh)
        ok = row_ok[:, None, :, None] & col_ok[None, :, None, :]
        masks.append(ok.reshape(tr * GRID_W, 3 * tr * GRID_W))
    ok = jnp.asarray(np.stack(masks))
    pad = GRID_W - NA_COLS
    rp = jnp.pad(rpb.astype(F32) * LOG2E, ((0, 0), (0, 0), (pad, pad)))
    cexp = jnp.stack([rp[:, :, GRID_W - 1 - cq:2 * GRID_W - 1 - cq] for cq in range(GRID_W)], axis=2)
    off = NA_ROWS - 1 - tr
    assert off - (tr - 1) >= 0 and off + 3 * tr - 1 <= 2 * NA_ROWS - 2
    t5 = jnp.stack([cexp[:, off - rq:off - rq + 3 * tr] for rq in range(tr)], axis=1)
    tab = t5.transpose(0, 1, 3, 2, 4).reshape(n_heads, tr * GRID_W, 3 * tr * GRID_W)
    return jnp.where(ok[None], tab[:, None], NEG_INF)


def _deinterleave_cols(n_heads):
    one = np.concatenate([np.arange(0, HEAD_DIM, 2), np.arange(1, HEAD_DIM, 2)])
    return np.concatenate([h * HEAD_DIM + one for h in range(n_heads)])


def kernel(x, c, ctx, c_ctx, w_ada, b_ada, ln1_g, ln1_b, ln2_g, ln2_b, w_in_even, w_out_even, sink_logits, na_rpb, w_in_odd, w_out_odd, q_norm_g, k_norm_g, lambda_q1, lambda_k1, lambda_q2, lambda_k2, subln_g, w_router, router_bias, w_exp_gate, w_exp_up, w_exp_down):
    n_batch, seq, d = x.shape
    ctx_len = ctx.shape[1]
    depth = w_ada.shape[0]
    assert depth == 2, "one even (A||B) layer followed by one odd (C||D) layer"
    n_lat = n_batch * seq
    n_ctx = n_batch * ctx_len
    n_all = n_lat + n_ctx
    n_groups = n_batch + 1
    alpha = float((2 * depth) ** 0.25)
    heads = d // HEAD_DIM
    qh = heads // 2
    kvh = qh // 4
    gq = qh // kvh
    bh = heads - qh
    dh = (heads - qh) // 2
    moe_tm = 512
    proj_tm = min(1024, seq)
    assert seq % GRID_W == 0 and (seq // GRID_W) % B_TILE_ROWS == 0 and seq // GRID_W >= NA_ROWS
    assert n_ctx % proj_tm == 0 and n_lat % ctx_len == 0 and (2 * n_all) % moe_tm == 0
    assert ctx_len % LANES == 0 and bh % B_HEADS_PER_STEP == 0
    assert n_all <= 1 << 16, "token ids are packed as 16-bit halves for the MoE row gather"

    pad_rows = -(-n_groups // 16) * 16
    c_rows = jnp.concatenate([c, c_ctx[None], jnp.zeros((pad_rows - n_groups, d), F32)], 0)
    mod = _adaln(c_rows, w_ada, b_ada)[:, :n_groups]
    mod = mod.reshape(depth, n_groups, 6, 1, d)
    mods = [[mod[l, :, j] for j in range(6)] for l in range(depth)]

    cos2, sin2 = _rope_tables(seq, proj_tm)
    tabs = (cos2, sin2)
    x_lat, x_ctx = x.reshape(n_lat, d), ctx.reshape(n_ctx, d)
    row2 = lambda v: v.reshape(1, -1)
    ones = lambda n: jnp.ones((1, n), F32)
    qscale = lambda n: jnp.full((1, n), Q_SCALE, F32)
    w_router_t = w_router.T.astype(F32)
    router_bias_col = router_bias.reshape(-1, 1).astype(F32)
    wg_all, wu_all, wd_all = (_cast_bf16(w) for w in (w_exp_gate, w_exp_up, w_exp_down))

    def moe(h2, idx, wts, layer):
        src_packed, dst_row, gw, tile_expert, n_valid = _dispatch(idx, wts, tm=moe_tm)
        return _moe_ffn(h2, src_packed, dst_row, gw, tile_expert, n_valid, wg_all, wu_all, wd_all,
                        layer=layer, tm=moe_tm)

    sh1, sc1, g1, sh2, sc2, g2 = mods[0]
    h = _modulate(x_lat, x_ctx, sc1, sh1, seq=seq, n_groups=n_groups)
    w_in = w_in_even[0]
    na, nk, nb = qh * HEAD_DIM, kvh * HEAD_DIM, bh * HEAD_DIM
    perm_q, perm_k = _deinterleave_cols(qh), _deinterleave_cols(kvh)
    w_aq = w_in[:, :na][:, perm_q].astype(BF16)
    w_ak = w_in[:, na:na + nk][:, perm_k].astype(BF16)
    w_rest = jnp.concatenate([w_in[:, na + 2 * nk:], w_in[:, na + nk:na + 2 * nk]], 1).astype(BF16)
    rest_scale = jnp.concatenate([qscale(nb), ones(2 * nb + nk)], 1)
    kw = dict(seq=seq, n_latent=n_lat)
    aq_rot, aq_nopos = _proj(h, w_aq, qscale(na), n_rows=n_all, rope_tabs=tabs, emit_nopos=True,
                             name="proj_aq", **kw)
    ak = _proj(h, w_ak, ones(nk), n_rows=n_all, rope_tabs=tabs, name="proj_ak", **kw)
    rest = _proj(h, w_rest, rest_scale, n_rows=n_all, tn=w_rest.shape[1] // 2, name="proj_even_rest", **kw)
    sink = sink_logits[0].astype(F32)
    akw = dict(n_batch=n_batch, seq=seq, ctx_len=ctx_len, n_latent=n_lat)
    o_a = _band_attn(aq_rot, aq_nopos, ak, rest, _window_bias(seq, gq), sink, n_kv=kvh, nh=kvh, g=gq,
                     bq=A_BLOCK, q_col=0, k_col=0, v_col=3 * nb // nk, nq=A_BLOCKS_PER_STEP, name="attn_a", **akw)
    bw = B_HEADS_PER_STEP * HEAD_DIM
    o_b = _band_attn(rest, None, rest, rest, _neighbourhood_bias(na_rpb[0], seq), None, n_kv=bh,
                     nh=B_HEADS_PER_STEP, g=1, bq=B_TILE_ROWS * GRID_W, q_col=0, k_col=nb // bw,
                     v_col=2 * nb // bw, name="attn_b", **akw)
    ckw = dict(n_batch=n_batch, n_q=ctx_len, q_row0=n_lat, n_k=ctx_len, k_row0=n_lat, ctx_len=ctx_len,
               ctx_row0=n_lat)
    oc_a = _dense_attn(aq_rot, None, ak, rest, sink, n_kv=kvh, nh=kvh, g=gq, q_col=0, k_col=0,
                       v_col=3 * nb // nk, name="ctx_attn_a", **ckw)
    oc_b = _dense_attn(rest, None, rest, rest, None, n_kv=bh, nh=bh, g=1, q_col=0, k_col=1, v_col=2,
                       name="ctx_attn_b", **ckw)
    w_out = w_out_even[0].astype(BF16)
    x1, h2, idx, wts = _outproj_ln((o_a, oc_a), (o_b, oc_b), (x_lat, x_ctx), w_out[:na], w_out[na:], g1, sc2,
                                   sh2, row2(ln1_g[0]), row2(ln1_b[0]), w_router_t, router_bias_col,
                                   n_lat=n_lat, seq=seq, n_groups=n_groups, alpha=alpha)
    y = moe(h2, idx, wts, 0)
    nsh1, nsc1 = mods[1][0], mods[1][1]
    xt, h = _resid_ln(x1, y, g2, row2(ln2_g[0]), row2(ln2_b[0]), nsc1, nsh1, n_rows=n_all, seq=seq,
                      n_groups=n_groups, alpha=alpha)

    sh1, sc1, g1, sh2, sc2, g2 = mods[1]
    lambda_init = 0.8 - 0.6 * math.exp(-0.3 * 1)
    w_in = w_in_odd[0]
    nd = dh * 2 * HEAD_DIM
    o0 = na + 2 * nk
    perm_d = _deinterleave_cols(2 * dh)
    perm_one = _deinterleave_cols(1)
    w_cq = w_in[:, :na][:, perm_q].astype(BF16)
    w_ck = w_in[:, na:na + nk][:, perm_k].astype(BF16)
    w_dq = w_in[:, o0:o0 + nd][:, perm_d].astype(BF16)
    w_dk = w_in[:, o0 + nd:o0 + 2 * nd][:, perm_d].astype(BF16)
    w_v = jnp.concatenate([w_in[:, o0 + 2 * nd:], w_in[:, na + nk:o0]], 1).astype(BF16)
    qg = row2(q_norm_g[0][perm_one]).astype(F32)
    kg = row2(k_norm_g[0][perm_one]).astype(F32)
    cq_rot, cq_nopos = _proj(h, w_cq, qscale(na), n_rows=n_lat, gain=qg, rope_tabs=tabs, emit_nopos=True,
                             name="proj_cq", **kw)
    ck = _proj(h, w_ck, ones(nk), n_rows=n_all, gain=kg, rope_tabs=tabs, name="proj_ck", **kw)
    dq_rot, dq_nopos = _proj(h, w_dq, qscale(nd), n_rows=n_lat, rope_tabs=tabs, emit_nopos=True,
                             name="proj_dq", **kw)
    dk = _proj(h, w_dk, ones(nd), n_rows=n_all, rope_tabs=tabs, name="proj_dk", **kw)
    vv = _proj(h, w_v, ones(nk + nd), n_rows=n_all, name="proj_odd_v", **kw)
    o_c = _dense_attn(cq_rot, cq_nopos, ck, vv, None, n_batch=n_batch, n_q=seq, q_row0=0, n_k=seq, k_row0=0,
                      ctx_len=ctx_len, ctx_row0=n_lat, n_kv=kvh, g=gq, q_col=0, k_col=0, v_col=nd // HEAD_DIM,
                      name="attn_c")
    lam_vecs = jnp.stack([lambda_q1[0], lambda_k1[0], lambda_q2[0], lambda_k2[0]]).astype(F32)
    o_d = _diff_attn(dq_rot, dq_nopos, dk, vv, lam_vecs, row2(subln_g[0]).astype(F32), n_batch=n_batch,
                     seq=seq, ctx_len=ctx_len, n_latent=n_lat, n_heads=dh, lambda_init=lambda_init)
    w_out = w_out_odd[0].astype(BF16)
    x1, h2, idx, wts = _outproj_ln((o_c, None), (o_d, None), (xt, None), w_out[:na], w_out[na:], g1, sc2,
                                   sh2, row2(ln1_g[1]), row2(ln1_b[1]), w_router_t, router_bias_col,
                                   n_lat=n_lat, seq=seq, n_groups=n_groups, alpha=alpha)
    y = moe(h2, idx, wts, 1)
    x2, _ = _resid_ln(x1, y, g2, row2(ln2_g[1]), row2(ln2_b[1]), None, None, n_rows=n_lat, seq=seq,
                      n_groups=n_groups, alpha=alpha)
    return x2.reshape(n_batch, seq, d)
```

```python
import functools
import math

import numpy as np
import jax
import jax.numpy as jnp
from jax import lax
from jax.experimental import pallas as pl
from jax.experimental.pallas import tpu as pltpu

F32 = jnp.float32
BF16 = jnp.bfloat16

HEAD_DIM = 128
GRID_W = 64
ROPE_THETA = 10000.0
SCALE = HEAD_DIM ** -0.5
NEG_INF = -1e30
A_WINDOW = 128
A_BLOCK = 128
NA_ROWS = 8
NA_COLS = 16
B_TILE_ROWS = 4
B_HEADS_PER_STEP = 8
A_BLOCKS_PER_STEP = 4
B_BLOCKS_PER_STEP = 2
D_HEADS_PER_STEP = 1
ROW_PARTS = 2
PROJ_ROW_PARTS = 4
EPILOGUE_ROWS = 128
N_EXPERTS = 16
N_GROUPS = 4
EXPERTS_PER_GROUP = N_EXPERTS // N_GROUPS
LN_EPS = 1e-5
RMS_EPS = 1e-6
LOG2E = math.log2(math.e)
Q_SCALE = SCALE * LOG2E

V7X_VMEM_BYTES = 64 * 1024 * 1024
VMEM_LIMIT = V7X_VMEM_BYTES - 8 * 1024 * 1024
LANES = 128

NT_DIMS = (((1,), (1,)), ((), ()))


def _params(*sem):
    return pltpu.CompilerParams(dimension_semantics=sem, vmem_limit_bytes=VMEM_LIMIT)


def _adaln_kernel(c_ref, w_ref, b_ref, o_ref):
    c = c_ref[...]
    a = (c * (1.0 / (1.0 + jnp.exp(-c)))).astype(BF16)
    o_ref[...] = jnp.dot(a, w_ref[...].astype(BF16), preferred_element_type=F32) + b_ref[...]


def _adaln(c_rows, w_ada, b_ada):
    depth, d, n6 = w_ada.shape
    rows = c_rows.shape[0]
    tn = 1024
    return pl.pallas_call(
        _adaln_kernel,
        out_shape=jax.ShapeDtypeStruct((depth, rows, n6), F32),
        grid=(depth, n6 // tn),
        in_specs=[
            pl.BlockSpec((rows, d), lambda l, j: (0, 0)),
            pl.BlockSpec((None, d, tn), lambda l, j: (l, 0, j)),
            pl.BlockSpec((None, 1, tn), lambda l, j: (l, 0, j)),
        ],
        out_specs=pl.BlockSpec((None, rows, tn), lambda l, j: (l, 0, j)),
        compiler_params=_params("parallel", "parallel"),
        name="adaln",
    )(c_rows, w_ada, b_ada.reshape(depth, 1, n6))


def _two_stream_specs(tm, width, n_lat_tiles):
    return [pl.BlockSpec((tm, width), lambda i: (jnp.minimum(i, n_lat_tiles - 1), 0)),
            pl.BlockSpec((tm, width), lambda i: (jnp.maximum(i - n_lat_tiles, 0), 0))]


def _pick_stream(lat_ref, ctx_ref, n_lat_tiles):
    if ctx_ref is None:
        return lat_ref[...]
    return jnp.where(pl.program_id(0) >= n_lat_tiles, ctx_ref[...], lat_ref[...])


def _modulate_kernel(x_ref, c_ref, sc_ref, sh_ref, o_ref, *, n_lat_tiles):
    x = _pick_stream(x_ref, c_ref, n_lat_tiles)
    o_ref[...] = (x * (1.0 + sc_ref[0]) + sh_ref[0]).astype(o_ref.dtype)


def _modulate(x_lat, x_ctx, sc, sh, *, seq, n_groups):
    d = x_lat.shape[1]
    rows = x_lat.shape[0] + x_ctx.shape[0]
    tm = 512
    n_lat_tiles = x_lat.shape[0] // tm
    grp = lambda i: (jnp.minimum(i // (seq // tm), n_groups - 1), 0, 0)
    return pl.pallas_call(
        functools.partial(_modulate_kernel, n_lat_tiles=n_lat_tiles),
        out_shape=jax.ShapeDtypeStruct((rows, d), BF16),
        grid=(rows // tm,),
        in_specs=_two_stream_specs(tm, d, n_lat_tiles)
        + [pl.BlockSpec((1, 1, d), grp), pl.BlockSpec((1, 1, d), grp)],
        out_specs=pl.BlockSpec((tm, d), lambda i: (i, 0)),
        compiler_params=_params("parallel"),
        name="modulate",
    )(x_lat, x_ctx, sc, sh)


def _cast_kernel(x_ref, o_ref):
    o_ref[...] = x_ref[...].astype(o_ref.dtype)


def _cast_bf16(w):
    shape = w.shape
    w2 = w.reshape(-1, shape[-1])
    tm = min(w2.shape[0], (2 * 1024 * 1024) // shape[-1])
    out = pl.pallas_call(
        _cast_kernel,
        out_shape=jax.ShapeDtypeStruct(w2.shape, BF16),
        grid=(w2.shape[0] // tm,),
        in_specs=[pl.BlockSpec((tm, shape[-1]), lambda i: (i, 0))],
        out_specs=pl.BlockSpec((tm, shape[-1]), lambda i: (i, 0)),
        compiler_params=_params("parallel"),
        name="cast_bf16",
    )(w2)
    return out.reshape(shape)


def _proj_kernel(*refs, tn, norm, rope, emit_nopos):
    it = iter(refs)
    x_ref, w_ref, cs_ref = next(it), next(it), next(it)
    g_ref = next(it) if norm else None
    cos_ref, sin_ref = (next(it), next(it)) if rope else (None, None)
    o_ref = next(it)
    n_ref = next(it) if emit_nopos else None
    tm = x_ref.shape[0]
    n_parts = PROJ_ROW_PARTS if tm % (PROJ_ROW_PARTS * 128) == 0 else 1
    rows_per_part = tm // n_parts
    part_rows = [slice(p * rows_per_part, (p + 1) * rows_per_part) for p in range(n_parts)]
    matmul = lambda p: jnp.dot(x_ref[part_rows[p], :], w_ref[...], preferred_element_type=F32)
    acc_next = matmul(0)
    for part in range(n_parts):
        rows = part_rows[part]
        acc = acc_next
        if part + 1 < n_parts:
            acc_next = matmul(part + 1)
        for hd in range(tn // HEAD_DIM):
            sl = slice(hd * HEAD_DIM, (hd + 1) * HEAD_DIM)
            xh = acc[:, sl]
            if norm:
                xh = xh * lax.rsqrt(jnp.mean(xh * xh, axis=-1, keepdims=True) + RMS_EPS) * g_ref[...]
            cs = cs_ref[:, sl]
            if rope:
                rot = xh * cos_ref[rows, :] + pltpu.roll(xh, HEAD_DIM // 2, 1) * sin_ref[rows, :]
                o_ref[rows, sl] = (rot * cs).astype(o_ref.dtype)
                if emit_nopos:
                    n_ref[rows, sl] = (xh * cs).astype(n_ref.dtype)
            else:
                o_ref[rows, sl] = (xh * cs).astype(o_ref.dtype)


def _proj(x, w, col_scale, *, n_rows, seq, n_latent, tn=None, gain=None, rope_tabs=None,
          emit_nopos=False, name="proj"):
    d = x.shape[1]
    nc = w.shape[1]
    tm = min(1024, seq)
    tn = nc if tn is None else tn
    norm, rope = gain is not None, rope_tabs is not None
    in_specs = [pl.BlockSpec((tm, d), lambda i, j: (i, 0)),
                pl.BlockSpec((d, tn), lambda i, j: (0, j)),
                pl.BlockSpec((1, tn), lambda i, j: (0, j))]
    args = [x, w, col_scale]
    if norm:
        in_specs.append(pl.BlockSpec((1, HEAD_DIM), lambda i, j: (0, 0)))
        args.append(gain)
    if rope:
        per_seq = seq // tm
        tab = lambda i, j: (jnp.where(i < n_latent // tm, i % per_seq, per_seq), 0)
        in_specs += [pl.BlockSpec((tm, HEAD_DIM), tab)] * 2
        args += list(rope_tabs)
    n_out = 2 if emit_nopos else 1
    out_shape = [jax.ShapeDtypeStruct((n_rows, nc), BF16)] * n_out
    out_specs = [pl.BlockSpec((tm, tn), lambda i, j: (i, j))] * n_out
    res = pl.pallas_call(
        functools.partial(_proj_kernel, tn=tn, norm=norm, rope=rope, emit_nopos=emit_nopos),
        out_shape=out_shape,
        grid=(n_rows // tm, nc // tn),
        in_specs=in_specs,
        out_specs=out_specs,
        compiler_params=_params("parallel", "parallel"),
        name=name,
    )(*args)
    return res if emit_nopos else res[0]


def _stack_heads(x, g):
    if g == 1:
        return x
    return jnp.concatenate([x[:, i * HEAD_DIM:(i + 1) * HEAD_DIM] for i in range(g)], axis=0)


def _unstack_store(o_ref, o, g, t, col0=0):
    for i in range(g):
        o_ref[:, col0 + i * HEAD_DIM:col0 + (i + 1) * HEAD_DIM] = o[i * t:(i + 1) * t].astype(o_ref.dtype)


def _sink_column(sink_ref, first_head, g, t):
    cols = [jnp.full((t, 1), sink_ref[first_head + i] * LOG2E, F32) for i in range(g)]
    return cols[0] if g == 1 else jnp.concatenate(cols, axis=0)


def _with_ones(v):
    return jnp.concatenate([v, jnp.ones_like(v)], axis=1)


def _row_max(pieces):
    cols = [p[:, c:c + LANES] for p in pieces for c in range(0, p.shape[1], LANES)]
    return jnp.max(functools.reduce(jnp.maximum, cols), axis=-1, keepdims=True)


def _online_update(s, v_aug, m, acc):
    m_new = jnp.maximum(m, jnp.max(s, axis=-1, keepdims=True))
    p = jnp.exp2(s - m_new)
    acc = jnp.exp2(m - m_new) * acc + jnp.dot(p.astype(BF16), v_aug, preferred_element_type=F32)
    return m_new, acc


def _band_kernel(*refs, nq, nh, g, bq, has_sink, same_qc, shared_bias):
    it = iter(refs)
    sink_ref = next(it) if has_sink else None
    q_ref = next(it)
    qc_ref = q_ref if same_qc else next(it)
    k_refs = [next(it) for _ in range(nq + 2)]
    v_refs = [next(it) for _ in range(nq + 2)]
    kx_ref, vx_ref = next(it), next(it)
    bias_refs = [next(it) for _ in range(nq)]
    o_ref = next(it)
    hw = g * HEAD_DIM
    chains = [(qb, h) for qb in range(nq) for h in range(nh)]

    def scores(qb, h):
        rows = slice(qb * bq, (qb + 1) * bq)
        qs = slice(h * hw, (h + 1) * hw)
        ks = slice(h * HEAD_DIM, (h + 1) * HEAD_DIM)
        q = _stack_heads(q_ref[rows, qs], g)
        qc = q if same_qc else _stack_heads(qc_ref[rows, qs], g)
        bias = bias_refs[qb].at[0 if shared_bias else h]
        s = [lax.dot_general(q, k_refs[qb + j][:, ks], NT_DIMS, preferred_element_type=F32)
             + bias[:, j * bq:(j + 1) * bq] for j in range(3)]
        s.append(lax.dot_general(qc, kx_ref[:, ks], NT_DIMS, preferred_element_type=F32))
        return s

    def probs(qb, h, s):
        m = _row_max(s)
        sink = None
        if has_sink:
            sink = _sink_column(sink_ref, (pl.program_id(0) * nh + h) * g, g, bq)
            m = jnp.maximum(m, sink)
        return [jnp.exp2(sj - m).astype(BF16) for sj in s], (None if sink is None else jnp.exp2(sink - m))

    def output(qb, h, p, sink_term):
        rows = slice(qb * bq, (qb + 1) * bq)
        ks = slice(h * HEAD_DIM, (h + 1) * HEAD_DIM)
        vs = [v_refs[qb + j][:, ks] for j in range(3)] + [vx_ref[:, ks]]
        o = functools.reduce(jnp.add, [jnp.dot(pj, _with_ones(vj), preferred_element_type=F32)
                                       for pj, vj in zip(p, vs)])
        den = o[:, HEAD_DIM:]
        if sink_term is not None:
            den = den + sink_term
        _unstack_store(o_ref.at[rows, :], o[:, :HEAD_DIM] / den, g, bq, col0=h * hw)

    s_of, p_of = {}, {}
    for step in range(len(chains) + 2):
        if step < len(chains):
            s_of[step] = scores(*chains[step])
        if 0 <= step - 1 < len(chains):
            p_of[step - 1] = probs(*chains[step - 1], s_of.pop(step - 1))
        if 0 <= step - 2 < len(chains):
            output(*chains[step - 2], *p_of.pop(step - 2))


def _band_attn(q, qc, k, v, bias, sink, *, n_batch, seq, ctx_len, n_latent, n_kv, nh, g, bq,
               q_col, k_col, v_col, name, nq=1):
    nblk = seq // bq
    assert nblk % nq == 0
    nstep = nblk // nq
    ctx_blk = n_latent // ctx_len
    shared_bias = bias.shape[0] == 1
    has_sink, same_qc = sink is not None, qc is None
    qw, kw = nh * g * HEAD_DIM, nh * HEAD_DIM

    def q_map(hg, i, b):
        return (b * nstep + i, q_col + hg)

    def kv_map(col, off):
        return lambda hg, i, b: (b * nblk + jnp.clip(i * nq + off, 0, nblk - 1), col + hg)

    def ctx_map(col):
        return lambda hg, i, b: (ctx_blk + b, col + hg)

    def bias_map(qb):
        def index(hg, i, b):
            blk = i * nq + qb
            case = jnp.where(blk == 0, 0, jnp.where(blk == nblk - 1, 2, 1))
            return (0 if shared_bias else hg, case, 0, 0)
        return index

    in_specs, args = [], []
    if has_sink:
        in_specs.append(pl.BlockSpec(memory_space=pltpu.SMEM))
        args.append(sink)
    in_specs.append(pl.BlockSpec((nq * bq, qw), q_map))
    args.append(q)
    if not same_qc:
        in_specs.append(pl.BlockSpec((nq * bq, qw), q_map))
        args.append(qc)
    offsets = range(-1, nq + 1)
    in_specs += [pl.BlockSpec((bq, kw), kv_map(k_col, off)) for off in offsets]
    args += [k] * len(offsets)
    in_specs += [pl.BlockSpec((bq, kw), kv_map(v_col, off)) for off in offsets]
    args += [v] * len(offsets)
    in_specs += [pl.BlockSpec((ctx_len, kw), ctx_map(k_col)), pl.BlockSpec((ctx_len, kw), ctx_map(v_col))]
    args += [k, v]
    in_specs += [pl.BlockSpec((1 if shared_bias else nh, None, g * bq, 3 * bq), bias_map(qb)) for qb in range(nq)]
    args += [bias] * nq
    return pl.pallas_call(
        functools.partial(_band_kernel, nq=nq, nh=nh, g=g, bq=bq, has_sink=has_sink, same_qc=same_qc,
                          shared_bias=shared_bias),
        out_shape=jax.ShapeDtypeStruct((n_latent, n_kv * g * HEAD_DIM), BF16),
        grid=(n_kv // nh, nstep, n_batch),
        in_specs=in_specs,
        out_specs=pl.BlockSpec((nq * bq, qw), lambda hg, i, b: (b * nstep + i, hg)),
        compiler_params=_params("parallel", "parallel", "parallel"),
        name=name,
    )(*args)


def _dense_kernel(*refs, nh, g, tq, tk, n_k, has_ctx, has_sink):
    it = iter(refs)
    sink_ref = next(it) if has_sink else None
    q_ref = next(it)
    qc_ref = next(it) if has_ctx else None
    k_ref, v_ref = next(it), next(it)
    kx_ref, vx_ref = (next(it), next(it)) if has_ctx else (None, None)
    o_ref = next(it)
    hw = g * HEAD_DIM
    rows = g * tq
    for h in range(nh):
        qs = slice(h * hw, (h + 1) * hw)
        ks = slice(h * HEAD_DIM, (h + 1) * HEAD_DIM)
        q = _stack_heads(q_ref[:, qs], g)
        m = jnp.full((rows, 1), NEG_INF, F32)
        acc = jnp.zeros((rows, 2 * HEAD_DIM), F32)
        chunks = [(q, k_ref.at[c * tk:(c + 1) * tk, ks], v_ref.at[c * tk:(c + 1) * tk, ks])
                  for c in range(n_k // tk)]
        if has_ctx:
            chunks.append((_stack_heads(qc_ref[:, qs], g), kx_ref.at[:, ks], vx_ref.at[:, ks]))
        score = lambda ch: lax.dot_general(ch[0], ch[1][...], NT_DIMS, preferred_element_type=F32)
        s_next = score(chunks[0])
        for c, ch in enumerate(chunks):
            s = s_next
            if c + 1 < len(chunks):
                s_next = score(chunks[c + 1])
            m, acc = _online_update(s, _with_ones(ch[2][...]), m, acc)
        num, den = acc[:, :HEAD_DIM], acc[:, HEAD_DIM:]
        if has_sink:
            sink = _sink_column(sink_ref, (pl.program_id(1) * nh + h) * g, g, tq)
            m_new = jnp.maximum(m, sink)
            alpha = jnp.exp2(m - m_new)
            num = alpha * num
            den = alpha * den + jnp.exp2(sink - m_new)
        _unstack_store(o_ref, num / den, g, tq, col0=h * hw)


def _dense_attn(q, qc, k, v, sink, *, n_batch, n_q, q_row0, n_k, k_row0, ctx_len, ctx_row0,
                n_kv, g, q_col, k_col, v_col, name, nh=1):
    tq = min(256, n_q)
    tk = min(512, n_k)
    nq_blk = n_q // tq
    has_ctx, has_sink = qc is not None, sink is not None
    qw, kw = nh * g * HEAD_DIM, nh * HEAD_DIM

    def q_map(b, h, i):
        return (q_row0 // tq + b * nq_blk + i, q_col + h)

    in_specs, args = [], []
    if has_sink:
        in_specs.append(pl.BlockSpec(memory_space=pltpu.SMEM))
        args.append(sink)
    in_specs.append(pl.BlockSpec((tq, qw), q_map))
    args.append(q)
    if has_ctx:
        in_specs.append(pl.BlockSpec((tq, qw), q_map))
        args.append(qc)
    in_specs += [pl.BlockSpec((n_k, kw), lambda b, h, i: (k_row0 // n_k + b, k_col + h)),
                 pl.BlockSpec((n_k, kw), lambda b, h, i: (k_row0 // n_k + b, v_col + h))]
    args += [k, v]
    if has_ctx:
        in_specs += [pl.BlockSpec((ctx_len, kw), lambda b, h, i: (ctx_row0 // ctx_len + b, k_col + h)),
                     pl.BlockSpec((ctx_len, kw), lambda b, h, i: (ctx_row0 // ctx_len + b, v_col + h))]
        args += [k, v]
    return pl.pallas_call(
        functools.partial(_dense_kernel, nh=nh, g=g, tq=tq, tk=tk, n_k=n_k, has_ctx=has_ctx,
                          has_sink=has_sink),
        out_shape=jax.ShapeDtypeStruct((n_batch * n_q, n_kv * g * HEAD_DIM), BF16),
        grid=(n_batch, n_kv // nh, nq_blk),
        in_specs=in_specs,
        out_specs=pl.BlockSpec((tq, qw), lambda b, h, i: (b * nq_blk + i, h)),
        compiler_params=_params("parallel", "parallel", "parallel"),
        name=name,
    )(*args)


def _diff_update(s, v, m, l, acc):
    m_new = jnp.maximum(m, jnp.max(s, axis=-1, keepdims=True))
    alpha = jnp.exp2(m - m_new)
    p = jnp.exp2(s - m_new)
    l = alpha * l + jnp.sum(p, axis=-1, keepdims=True)
    acc = alpha * acc + jnp.dot(p.astype(BF16), v, preferred_element_type=F32)
    return m_new, l, acc


def _diff_kernel(lam_ref, g_ref, q_ref, qc_ref, k_ref, v_ref, kx_ref, vx_ref, o_ref, *,
                 nh, tq, tk, n_k, lambda_init):
    dv = 2 * HEAD_DIM
    lam = (jnp.exp(jnp.sum(lam_ref[0:1, :] * lam_ref[1:2, :], axis=1, keepdims=True))
           - jnp.exp(jnp.sum(lam_ref[2:3, :] * lam_ref[3:4, :], axis=1, keepdims=True)) + lambda_init)
    maps = [(h, t) for h in range(nh) for t in range(2)]
    qk_cols = {(h, t): slice(h * dv + t * HEAD_DIM, h * dv + (t + 1) * HEAD_DIM) for h, t in maps}
    v_cols = {h: slice(h * dv, (h + 1) * dv) for h in range(nh)}
    state = {mp: (jnp.full((tq, 1), NEG_INF, F32), jnp.zeros((tq, 1), F32), jnp.zeros((tq, dv), F32))
             for mp in maps}
    for c in range(n_k // tk):
        rows = slice(c * tk, (c + 1) * tk)
        for mp in maps:
            s = lax.dot_general(q_ref[:, qk_cols[mp]], k_ref[rows, qk_cols[mp]], NT_DIMS,
                                preferred_element_type=F32)
            state[mp] = _diff_update(s, v_ref[rows, v_cols[mp[0]]], *state[mp])
    for h in range(nh):
        outs = []
        for t in range(2):
            mp = (h, t)
            s = lax.dot_general(qc_ref[:, qk_cols[mp]], kx_ref[:, qk_cols[mp]], NT_DIMS,
                                preferred_element_type=F32)
            _, l, acc = _diff_update(s, vx_ref[:, v_cols[h]], *state[mp])
            outs.append(acc / l)
        o = outs[0] - lam * outs[1]
        o = o * lax.rsqrt(jnp.mean(o * o, axis=-1, keepdims=True) + RMS_EPS) * g_ref[...]
        o_ref[:, v_cols[h]] = (o * (1.0 - lambda_init)).astype(o_ref.dtype)


def _diff_attn(q, qc, k, v, lam_vecs, subln_g, *, n_batch, seq, ctx_len, n_latent, n_heads,
               lambda_init):
    nh = D_HEADS_PER_STEP
    tq = min(512, seq)
    tk = min(512, seq)
    dv = 2 * HEAD_DIM
    nq_blk = seq // tq
    ctx_blk = n_latent // ctx_len
    q_map = lambda b, h, i: (b * nq_blk + i, h)
    return pl.pallas_call(
        functools.partial(_diff_kernel, nh=nh, tq=tq, tk=tk, n_k=seq, lambda_init=lambda_init),
        out_shape=jax.ShapeDtypeStruct((n_latent, n_heads * dv), BF16),
        grid=(n_batch, n_heads // nh, nq_blk),
        in_specs=[
            pl.BlockSpec((4, HEAD_DIM), lambda b, h, i: (0, 0)),
            pl.BlockSpec((1, dv), lambda b, h, i: (0, 0)),
            pl.BlockSpec((tq, nh * dv), q_map),
            pl.BlockSpec((tq, nh * dv), q_map),
            pl.BlockSpec((seq, nh * dv), lambda b, h, i: (b, h)),
            pl.BlockSpec((seq, nh * dv), lambda b, h, i: (b, h)),
            pl.BlockSpec((ctx_len, nh * dv), lambda b, h, i: (ctx_blk + b, h)),
            pl.BlockSpec((ctx_len, nh * dv), lambda b, h, i: (ctx_blk + b, h)),
        ],
        out_specs=pl.BlockSpec((tq, nh * dv), q_map),
        compiler_params=_params("parallel", "parallel", "parallel"),
        name="diff_attn",
    )(lam_vecs, subln_g, q, qc, k, v, k, v)


def _layer_norm(z, g, b):
    mu = jnp.mean(z, axis=-1, keepdims=True)
    zc = z - mu
    var = jnp.mean(zc * zc, axis=-1, keepdims=True)
    return zc * lax.rsqrt(var + LN_EPS) * g + b


def _split_bf16(x):
    hi = x.astype(BF16)
    return hi, (x - hi.astype(F32)).astype(BF16)


def _first_argmax(vals, idx, width):
    m = jnp.max(vals, axis=0, keepdims=True)
    first = jnp.min(jnp.where(vals == m, idx, float(width)), axis=0, keepdims=True)
    return m, first


def _route(logits, bias):
    e, t = logits.shape
    scores = 1.0 / (1.0 + jnp.exp(-logits))
    biased = scores + bias
    row_i = lax.broadcasted_iota(jnp.int32, (e, t), 0)
    grp = lax.shift_right_logical(row_i, int(math.log2(EXPERTS_PER_GROUP)))
    row = row_i.astype(F32)
    neg = -jnp.inf
    best_score, best = None, None
    for gi in range(N_GROUPS):
        vg = jnp.where(grp == gi, biased, neg)
        m1, i1 = _first_argmax(vg, row, e)
        m2 = jnp.max(jnp.where(row == i1, neg, vg), axis=0, keepdims=True)
        gs = m1 + m2
        if gi == 0:
            best_score, best = gs, jnp.zeros((1, t), jnp.int32)
        else:
            upd = gs > best_score
            best = jnp.where(upd, gi, best)
            best_score = jnp.where(upd, gs, best_score)
    masked = jnp.where(grp == best, biased, neg)
    _, e1 = _first_argmax(masked, row, e)
    _, e2 = _first_argmax(jnp.where(row == e1, neg, masked), row, e)
    w1 = jnp.sum(jnp.where(row == e1, scores, 0.0), axis=0, keepdims=True)
    w2 = jnp.sum(jnp.where(row == e2, scores, 0.0), axis=0, keepdims=True)
    den = w1 + w2
    two = lax.broadcasted_iota(jnp.int32, (2, t), 0)
    return jnp.where(two == 0, e1, e2).astype(jnp.int32), jnp.where(two == 0, w1 / den, w2 / den)


def _outproj_kernel(*refs, alpha, two_streams, n_lat_tiles):
    it = iter(refs)
    xa_ref, xa_ctx = next(it), (next(it) if two_streams else None)
    xb_ref, xb_ctx = next(it), (next(it) if two_streams else None)
    xres_ref, xres_ctx = next(it), (next(it) if two_streams else None)
    (wa_ref, wb_ref, g1_ref, sc2_ref, sh2_ref, lng_ref, lnb_ref, wrt_ref, rb_ref,
     x1_ref, h2_ref, idx_ref, wts_ref) = it
    w_hi, w_lo = _split_bf16(wrt_ref[...])
    w_cat = jnp.concatenate([w_hi, w_lo], axis=0)
    tm = x1_ref.shape[0]
    rows_per_part = tm // ROW_PARTS

    def pick(lat, ctx, part):
        rows = slice(part * rows_per_part, (part + 1) * rows_per_part)
        return _pick_stream(lat.at[rows, :], None if ctx is None else ctx.at[rows, :], n_lat_tiles)

    def matmul(part):
        return (jnp.dot(pick(xa_ref, xa_ctx, part), wa_ref[...], preferred_element_type=F32)
                + jnp.dot(pick(xb_ref, xb_ctx, part), wb_ref[...], preferred_element_type=F32))

    lookahead = not two_streams
    y_next = matmul(0)
    for part in range(ROW_PARTS):
        y_all = y_next
        if part + 1 < ROW_PARTS and lookahead:
            y_next = matmul(part + 1)
        xres_all = pick(xres_ref, xres_ctx, part)
        n_sub = rows_per_part // EPILOGUE_ROWS
        for sub in range(n_sub):
            sr = slice(sub * EPILOGUE_ROWS, (sub + 1) * EPILOGUE_ROWS)
            orow = slice(part * rows_per_part + sub * EPILOGUE_ROWS,
                         part * rows_per_part + (sub + 1) * EPILOGUE_ROWS)
            x1 = _layer_norm(alpha * xres_all[sr] + g1_ref[0] * y_all[sr], lng_ref[...], lnb_ref[...])
            x1_ref[orow, :] = x1
            h2 = x1 * (1.0 + sc2_ref[0]) + sh2_ref[0]
            h2_ref[orow, :] = h2
            h_hi, h_lo = _split_bf16(h2)
            both = lax.dot_general(w_cat, h_hi, NT_DIMS, preferred_element_type=F32)
            logits = (both[:N_EXPERTS] + both[N_EXPERTS:]
                      + lax.dot_general(w_hi, h_lo, NT_DIMS, preferred_element_type=F32))
            idx, wts = _route(logits, rb_ref[...])
            idx_ref[:, orow] = idx
            wts_ref[:, orow] = wts
        if part + 1 < ROW_PARTS and not lookahead:
            y_next = matmul(part + 1)


def _outproj_ln(xa, xb, xres, wa, wb, g1, sc2, sh2, ln_g, ln_b, w_router_t, router_bias, *,
                n_lat, seq, n_groups, alpha):
    two_streams = xa[1] is not None
    d = xres[0].shape[1]
    ka, kb = xa[0].shape[1], xb[0].shape[1]
    tm = 256 * ROW_PARTS
    n_lat_tiles = n_lat // tm
    n_rows = n_lat + (xres[1].shape[0] if two_streams else 0)
    row = lambda i: (i, 0)
    col = lambda i: (0, i)
    const = lambda i: (0, 0)
    grp = lambda i: (jnp.minimum(i // (seq // tm), n_groups - 1), 0, 0)
    in_specs, args = [], []
    for pair, width in ((xa, ka), (xb, kb), (xres, d)):
        if two_streams:
            in_specs += _two_stream_specs(tm, width, n_lat_tiles)
            args += list(pair)
        else:
            in_specs.append(pl.BlockSpec((tm, width), row))
            args.append(pair[0])
    once = pl.Buffered(1)
    in_specs += [pl.BlockSpec((ka, d), const, pipeline_mode=once), pl.BlockSpec((kb, d), const, pipeline_mode=once),
                 pl.BlockSpec((1, 1, d), grp), pl.BlockSpec((1, 1, d), grp), pl.BlockSpec((1, 1, d), grp),
                 pl.BlockSpec((1, d), const), pl.BlockSpec((1, d), const),
                 pl.BlockSpec((N_EXPERTS, d), const), pl.BlockSpec((N_EXPERTS, 1), const)]
    args += [wa, wb, g1, sc2, sh2, ln_g, ln_b, w_router_t, router_bias]
    return pl.pallas_call(
        functools.partial(_outproj_kernel, alpha=alpha, two_streams=two_streams, n_lat_tiles=n_lat_tiles),
        out_shape=[jax.ShapeDtypeStruct((n_rows, d), F32), jax.ShapeDtypeStruct((n_rows, d), F32),
                   jax.ShapeDtypeStruct((2, n_rows), jnp.int32), jax.ShapeDtypeStruct((2, n_rows), F32)],
        grid=(n_rows // tm,),
        in_specs=in_specs,
        out_specs=[pl.BlockSpec((tm, d), row), pl.BlockSpec((tm, d), row),
                   pl.BlockSpec((2, tm), col), pl.BlockSpec((2, tm), col)],
        compiler_params=_params("parallel"),
        name="outproj_ln",
    )(*args)


def _pack_bf16_pairs(x):
    n = x.shape[1] // 2
    lo = pltpu.bitcast(x[:, :n].astype(BF16).astype(F32), jnp.uint32)
    hi = pltpu.bitcast(x[:, n:].astype(BF16).astype(F32), jnp.uint32)
    return jnp.bitwise_or(lax.shift_right_logical(lo, jnp.uint32(16)), hi)


def _unpack_bf16_pairs(w):
    lo = pltpu.bitcast(lax.shift_left(w, jnp.uint32(16)), F32)
    hi = pltpu.bitcast(jnp.bitwise_and(w, jnp.uint32(0xFFFF0000)), F32)
    return jnp.concatenate([lo, hi], axis=1)


def _moe_kernel(te_ref, nv_ref, src_ref, dst_ref, h_hbm, gw_ref, wg_ref, wu_ref, wd_ref, y_hbm,
                xbuf, obuf, gsem, ssem, *, tm, ff_chunks):
    i = pl.program_id(0)
    n = pl.num_programs(0)
    nv = nv_ref[0]

    def start_gather(tile, s):
        base = tile * (tm // 2)
        for r in range(tm):
            word = src_ref[base + r // 2]
            tok = jnp.bitwise_and(word, 0xFFFF) if r % 2 == 0 else lax.shift_right_logical(word, 16)
            pltpu.make_async_copy(h_hbm.at[pl.ds(tok, 1)], xbuf.at[s, pl.ds(r, 1)],
                                  gsem.at[s]).start(priority=r % 2)

    def wait_gather(s):
        pltpu.make_async_copy(h_hbm.at[pl.ds(0, tm)], xbuf.at[s], gsem.at[s]).wait()

    def start_scatter(tile, s):
        base = tile * tm
        for r in range(tm):
            pltpu.make_async_copy(obuf.at[s, pl.ds(r, 1)], y_hbm.at[pl.ds(dst_ref[base + r], 1)],
                                  ssem.at[s]).start(priority=r % 2)

    def wait_scatter(s):
        pltpu.make_async_copy(obuf.at[s], y_hbm.at[pl.ds(0, tm)], ssem.at[s]).wait()

    @pl.when(i == 0)
    def _():
        start_gather(0, 0)
        obuf[...] = jnp.zeros_like(obuf)
        pad0 = y_hbm.shape[0] - 2 * tm
        fills = [pltpu.make_async_copy(obuf.at[s], y_hbm.at[pl.ds(pad0 + s * tm, tm)], ssem.at[s])
                 for s in range(2)]
        for f in fills:
            f.start()
        for f in fills:
            f.wait()

    def step(slot):
        wait_gather(slot)

        @pl.when(i >= 2)
        def _():
            wait_scatter(slot)

        start_gather(jnp.minimum(i + 1, n - 1), 1 - slot)
        x = xbuf[slot].astype(BF16)
        fc = wg_ref.shape[1] // ff_chunks
        y = None
        for c in range(ff_chunks):
            cs = slice(c * fc, (c + 1) * fc)
            gate = jnp.dot(x, wg_ref[:, cs], preferred_element_type=F32)
            up = jnp.dot(x, wu_ref[:, cs], preferred_element_type=F32)
            act = (gate * (1.0 / (1.0 + jnp.exp(-gate))) * up).astype(BF16)
            part = jnp.dot(act, wd_ref[cs, :], preferred_element_type=F32)
            y = part if y is None else y + part
        obuf[slot] = _pack_bf16_pairs(y * gw_ref[...])
        start_scatter(i, slot)

    for slot in range(2):
        @pl.when(jnp.logical_and(i < nv, jnp.bitwise_and(i, 1) == slot))
        def _():
            step(slot)

    @pl.when(i == n - 1)
    def _():
        wait_gather(jnp.bitwise_and(nv, 1))

        @pl.when(nv >= 1)
        def _():
            wait_scatter(jnp.bitwise_and(nv - 1, 1))

        @pl.when(nv >= 2)
        def _():
            wait_scatter(jnp.bitwise_and(nv, 1))


def _moe_ffn(h2, src_packed, dst_row, gw, tile_expert, n_valid, w_gate, w_up, w_down, *, layer, tm):
    t, d = h2.shape
    ff = w_gate.shape[3]
    n_tiles = dst_row.shape[0] // tm
    return pl.pallas_call(
        functools.partial(_moe_kernel, tm=tm, ff_chunks=2),
        out_shape=jax.ShapeDtypeStruct((2 * t + 2 * tm, d // 2), jnp.uint32),
        grid_spec=pltpu.PrefetchScalarGridSpec(
            num_scalar_prefetch=4,
            grid=(n_tiles,),
            in_specs=[pl.BlockSpec(memory_space=pl.ANY),
                      pl.BlockSpec((tm, 1), lambda i, te, nv, src, dst: (i, 0)),
                      pl.BlockSpec((None, None, d, ff), lambda i, te, nv, src, dst: (layer, te[i], 0, 0)),
                      pl.BlockSpec((None, None, d, ff), lambda i, te, nv, src, dst: (layer, te[i], 0, 0)),
                      pl.BlockSpec((None, None, ff, d), lambda i, te, nv, src, dst: (layer, te[i], 0, 0))],
            out_specs=pl.BlockSpec(memory_space=pl.ANY),
            scratch_shapes=[pltpu.VMEM((2, tm, d), F32), pltpu.VMEM((2, tm, d // 2), jnp.uint32),
                            pltpu.SemaphoreType.DMA((2,)), pltpu.SemaphoreType.DMA((2,))],
        ),
        compiler_params=_params("arbitrary"),
        name="moe_ffn",
    )(tile_expert, n_valid, src_packed, dst_row, h2, gw, w_gate, w_up, w_down)


def _take(x, idx):
    return x.at[idx].get(mode="promise_in_bounds")


def _dispatch(idx, wts, *, tm):
    t = idx.shape[1]
    n_pairs = 2 * t
    n_tiles = n_pairs // tm + N_EXPERTS
    e_flat = idx.reshape(n_pairs)
    w_flat = wts.reshape(n_pairs)
    pair_bits = max(1, (n_pairs - 1).bit_length())
    keys = jnp.bitwise_or(jnp.left_shift(e_flat, pair_bits), jnp.arange(n_pairs, dtype=jnp.int32))
    order = jnp.bitwise_and(lax.sort(keys), (1 << pair_bits) - 1)
    experts = jnp.arange(N_EXPERTS, dtype=jnp.int32)
    counts = jnp.sum(e_flat[None, :] == experts[:, None], axis=1).astype(jnp.int32)
    start = jnp.cumsum(counts) - counts
    padded = ((counts + tm - 1) // tm) * tm
    pend = jnp.cumsum(padded)
    pstart = pend - padded
    tile_pos = jnp.arange(n_tiles, dtype=jnp.int32) * tm
    tile_expert = jnp.minimum(jnp.sum(tile_pos[:, None] >= pend[None, :], axis=1), N_EXPERTS - 1).astype(jnp.int32)
    local = (tile_pos - pstart[tile_expert])[:, None] + jnp.arange(tm, dtype=jnp.int32)[None, :]
    valid = local < counts[tile_expert][:, None]
    sorted_pos = jnp.clip(start[tile_expert][:, None] + local, 0, n_pairs - 1)
    src_pair = _take(order, sorted_pos)
    src_tok = jnp.where(valid, jnp.where(src_pair >= t, src_pair - t, src_pair), 0)
    pad_row = n_pairs + (tile_pos[:, None] + jnp.arange(tm, dtype=jnp.int32)[None, :]) % (2 * tm)
    dst_row = jnp.where(valid, src_pair, pad_row)
    gw = jnp.where(valid, _take(w_flat, src_pair), 0.0).astype(F32)
    src_packed = jnp.bitwise_or(src_tok[:, 0::2], jnp.left_shift(src_tok[:, 1::2], 16))
    n_valid = (pend[-1] // tm).astype(jnp.int32).reshape(1)
    return src_packed.reshape(-1), dst_row.reshape(-1), gw.reshape(-1, 1), tile_expert, n_valid


def _resid_ln_kernel(*refs, alpha, emit_h):
    it = iter(refs)
    x_ref, f0_ref, f1_ref, g_ref, lng_ref, lnb_ref = (next(it) for _ in range(6))
    sc_ref, sh_ref = (next(it), next(it)) if emit_h else (None, None)
    o_ref = next(it)
    f = _unpack_bf16_pairs(f0_ref[...]) + _unpack_bf16_pairs(f1_ref[...])
    x2 = _layer_norm(alpha * x_ref[...] + g_ref[0] * f, lng_ref[...], lnb_ref[...])
    o_ref[...] = x2
    if emit_h:
        h_ref = next(it)
        h_ref[...] = (x2 * (1.0 + sc_ref[0]) + sh_ref[0]).astype(h_ref.dtype)


def _resid_ln(x, y, gate, ln_g, ln_b, next_sc, next_sh, *, n_rows, seq, n_groups, alpha):
    d = x.shape[1]
    tm = 512
    emit_h = next_sc is not None
    row = lambda i: (i, 0)
    const = lambda i: (0, 0)
    grp = lambda i: (jnp.minimum(i // (seq // tm), n_groups - 1), 0, 0)
    in_specs = [pl.BlockSpec((tm, d), row), pl.BlockSpec((tm, d // 2), row),
                pl.BlockSpec((tm, d // 2), lambda i: (n_rows // tm + i, 0)), pl.BlockSpec((1, 1, d), grp),
                pl.BlockSpec((1, d), const), pl.BlockSpec((1, d), const)]
    args = [x, y, y, gate, ln_g, ln_b]
    out_shape = [jax.ShapeDtypeStruct((n_rows, d), F32)]
    out_specs = [pl.BlockSpec((tm, d), row)]
    if emit_h:
        in_specs += [pl.BlockSpec((1, 1, d), grp)] * 2
        args += [next_sc, next_sh]
        out_shape.append(jax.ShapeDtypeStruct((n_rows, d), BF16))
        out_specs.append(pl.BlockSpec((tm, d), row))
    res = pl.pallas_call(
        functools.partial(_resid_ln_kernel, alpha=alpha, emit_h=emit_h),
        out_shape=out_shape,
        grid=(n_rows // tm,),
        in_specs=in_specs,
        out_specs=out_specs,
        compiler_params=_params("parallel"),
        name="resid_ln",
    )(*args)
    return res if emit_h else (res[0], None)


def _rope_tables(seq, tm):
    t = jnp.arange(seq, dtype=jnp.int32)
    n_freq = HEAD_DIM // 4
    inv = ROPE_THETA ** (-jnp.arange(n_freq, dtype=F32) / n_freq)
    row = (t // GRID_W).astype(F32)
    col = (t % GRID_W).astype(F32)
    ang = jnp.concatenate([row[:, None] * inv[None], col[:, None] * inv[None]], -1)
    cos, sin = jnp.cos(ang), jnp.sin(ang)
    cos2 = jnp.concatenate([cos, cos], -1)
    sin2 = jnp.concatenate([-sin, sin], -1)
    cos2 = jnp.concatenate([cos2, jnp.ones((tm, HEAD_DIM), F32)], 0)
    sin2 = jnp.concatenate([sin2, jnp.zeros((tm, HEAD_DIM), F32)], 0)
    return cos2, sin2


def _window_bias(seq, g):
    bq = A_BLOCK
    nblk = seq // bq
    rel = np.arange(3 * bq)[None, :] - np.arange(bq)[:, None]
    band = (rel >= bq - A_WINDOW) & (rel <= bq + A_WINDOW)
    tabs = []
    for blk in (0, min(1, nblk - 1), nblk - 1):
        kpos = blk * bq - bq + np.arange(3 * bq)
        ok = band & ((kpos >= 0) & (kpos < seq))[None, :]
        tabs.append(np.tile(np.where(ok, 0.0, NEG_INF).astype(np.float32), (g, 1)))
    return jnp.asarray(np.stack(tabs)[None])


def _neighbourhood_bias(rpb, seq):
    n_heads = rpb.shape[0]
    rows = seq // GRID_W
    kh, kw = min(NA_ROWS, rows), NA_COLS
    tr = B_TILE_ROWS
    nblk = rows // tr
    col = np.arange(GRID_W)
    col_start = np.clip(col - kw // 2, 0, GRID_W - kw)
    col_ok = (col[None, :] >= col_start[:, None]) & (col[None, :] < col_start[:, None] + kw)
    assert np.all(np.abs(col[None, :] - col[:, None])[col_ok] <= NA_COLS - 1)
    masks = []
    for blk in (0, min(1, nblk - 1), nblk - 1):
        r = blk * tr + np.arange(tr)
        r0 = np.clip(r - kh // 2, 0, rows - kh)
        krow = (blk - 1) * tr + np.arange(3 * tr)
        row_ok = (krow[None, :] >= r0[:, None]) & (krow[None, :] < r0[:, None] + kh)
        ok = row_ok[:, None, :, None] & col_ok[None, :, None, :]
        masks.append(ok.reshape(tr * GRID_W, 3 * tr * GRID_W))
    ok = jnp.asarray(np.stack(masks))
    pad = GRID_W - NA_COLS
    rp = jnp.pad(rpb.astype(F32) * LOG2E, ((0, 0), (0, 0), (pad, pad)))
    cexp = jnp.stack([rp[:, :, GRID_W - 1 - cq:2 * GRID_W - 1 - cq] for cq in range(GRID_W)], axis=2)
    off = NA_ROWS - 1 - tr
    assert off - (tr - 1) >= 0 and off + 3 * tr - 1 <= 2 * NA_ROWS - 2
    t5 = jnp.stack([cexp[:, off - rq:off - rq + 3 * tr] for rq in range(tr)], axis=1)
    tab = t5.transpose(0, 1, 3, 2, 4).reshape(n_heads, tr * GRID_W, 3 * tr * GRID_W)
    return jnp.where(ok[None], tab[:, None], NEG_INF)


def _deinterleave_cols(n_heads):
    one = np.concatenate([np.arange(0, HEAD_DIM, 2), np.arange(1, HEAD_DIM, 2)])
    return np.concatenate([h * HEAD_DIM + one for h in range(n_heads)])


def kernel(x, c, ctx, c_ctx, w_ada, b_ada, ln1_g, ln1_b, ln2_g, ln2_b, w_in_even, w_out_even, sink_logits, na_rpb, w_in_odd, w_out_odd, q_norm_g, k_norm_g, lambda_q1, lambda_k1, lambda_q2, lambda_k2, subln_g, w_router, router_bias, w_exp_gate, w_exp_up, w_exp_down):
    n_batch, seq, d = x.shape
    ctx_len = ctx.shape[1]
    depth = w_ada.shape[0]
    assert depth == 2, "one even (A||B) layer followed by one odd (C||D) layer"
    n_lat = n_batch * seq
    n_ctx = n_batch * ctx_len
    n_all = n_lat + n_ctx
    n_groups = n_batch + 1
    alpha = float((2 * depth) ** 0.25)
    heads = d // HEAD_DIM
    qh = heads // 2
    kvh = qh // 4
    gq = qh // kvh
    bh = heads - qh
    dh = (heads - qh) // 2
    moe_tm = 512
    proj_tm = min(1024, seq)
    assert seq % GRID_W == 0 and (seq // GRID_W) % B_TILE_ROWS == 0 and seq // GRID_W >= NA_ROWS
    assert n_ctx % proj_tm == 0 and n_lat % ctx_len == 0 and (2 * n_all) % moe_tm == 0
    assert ctx_len % LANES == 0 and bh % B_HEADS_PER_STEP == 0
    assert n_all <= 1 << 16, "token ids are packed as 16-bit halves for the MoE row gather"

    pad_rows = -(-n_groups // 16) * 16
    c_rows = jnp.concatenate([c, c_ctx[None], jnp.zeros((pad_rows - n_groups, d), F32)], 0)
    mod = _adaln(c_rows, w_ada, b_ada)[:, :n_groups]
    mod = mod.reshape(depth, n_groups, 6, 1, d)
    mods = [[mod[l, :, j] for j in range(6)] for l in range(depth)]

    cos2, sin2 = _rope_tables(seq, proj_tm)
    tabs = (cos2, sin2)
    x_lat, x_ctx = x.reshape(n_lat, d), ctx.reshape(n_ctx, d)
    row2 = lambda v: v.reshape(1, -1)
    ones = lambda n: jnp.ones((1, n), F32)
    qscale = lambda n: jnp.full((1, n), Q_SCALE, F32)
    w_router_t = w_router.T.astype(F32)
    router_bias_col = router_bias.reshape(-1, 1).astype(F32)
    wg_all, wu_all, wd_all = (_cast_bf16(w) for w in (w_exp_gate, w_exp_up, w_exp_down))

    def moe(h2, idx, wts, layer):
        src_packed, dst_row, gw, tile_expert, n_valid = _dispatch(idx, wts, tm=moe_tm)
        return _moe_ffn(h2, src_packed, dst_row, gw, tile_expert, n_valid, wg_all, wu_all, wd_all,
                        layer=layer, tm=moe_tm)

    sh1, sc1, g1, sh2, sc2, g2 = mods[0]
    h = _modulate(x_lat, x_ctx, sc1, sh1, seq=seq, n_groups=n_groups)
    w_in = w_in_even[0]
    na, nk, nb = qh * HEAD_DIM, kvh * HEAD_DIM, bh * HEAD_DIM
    perm_q, perm_k = _deinterleave_cols(qh), _deinterleave_cols(kvh)
    w_aq = w_in[:, :na][:, perm_q].astype(BF16)
    w_ak = w_in[:, na:na + nk][:, perm_k].astype(BF16)
    w_rest = jnp.concatenate([w_in[:, na + 2 * nk:], w_in[:, na + nk:na + 2 * nk]], 1).astype(BF16)
    rest_scale = jnp.concatenate([qscale(nb), ones(2 * nb + nk)], 1)
    kw = dict(seq=seq, n_latent=n_lat)
    aq_rot, aq_nopos = _proj(h, w_aq, qscale(na), n_rows=n_all, rope_tabs=tabs, emit_nopos=True,
                             name="proj_aq", **kw)
    ak = _proj(h, w_ak, ones(nk), n_rows=n_all, rope_tabs=tabs, name="proj_ak", **kw)
    rest = _proj(h, w_rest, rest_scale, n_rows=n_all, tn=w_rest.shape[1] // 2, name="proj_even_rest", **kw)
    sink = sink_logits[0].astype(F32)
    akw = dict(n_batch=n_batch, seq=seq, ctx_len=ctx_len, n_latent=n_lat)
    o_a = _band_attn(aq_rot, aq_nopos, ak, rest, _window_bias(seq, gq), sink, n_kv=kvh, nh=kvh, g=gq,
                     bq=A_BLOCK, q_col=0, k_col=0, v_col=3 * nb // nk, nq=A_BLOCKS_PER_STEP, name="attn_a", **akw)
    bw = B_HEADS_PER_STEP * HEAD_DIM
    o_b = _band_attn(rest, None, rest, rest, _neighbourhood_bias(na_rpb[0], seq), None, n_kv=bh,
                     nh=B_HEADS_PER_STEP, g=1, bq=B_TILE_ROWS * GRID_W, q_col=0, k_col=nb // bw,
                     v_col=2 * nb // bw, nq=B_BLOCKS_PER_STEP, name="attn_b", **akw)
    ckw = dict(n_batch=n_batch, n_q=ctx_len, q_row0=n_lat, n_k=ctx_len, k_row0=n_lat, ctx_len=ctx_len,
               ctx_row0=n_lat)
    oc_a = _dense_attn(aq_rot, None, ak, rest, sink, n_kv=kvh, nh=kvh, g=gq, q_col=0, k_col=0,
                       v_col=3 * nb // nk, name="ctx_attn_a", **ckw)
    oc_b = _dense_attn(rest, None, rest, rest, None, n_kv=bh, nh=bh, g=1, q_col=0, k_col=1, v_col=2,
                       name="ctx_attn_b", **ckw)
    w_out = w_out_even[0].astype(BF16)
    x1, h2, idx, wts = _outproj_ln((o_a, oc_a), (o_b, oc_b), (x_lat, x_ctx), w_out[:na], w_out[na:], g1, sc2,
                                   sh2, row2(ln1_g[0]), row2(ln1_b[0]), w_router_t, router_bias_col,
                                   n_lat=n_lat, seq=seq, n_groups=n_groups, alpha=alpha)
    y = moe(h2, idx, wts, 0)
    nsh1, nsc1 = mods[1][0], mods[1][1]
    xt, h = _resid_ln(x1, y, g2, row2(ln2_g[0]), row2(ln2_b[0]), nsc1, nsh1, n_rows=n_all, seq=seq,
                      n_groups=n_groups, alpha=alpha)

    sh1, sc1, g1, sh2, sc2, g2 = mods[1]
    lambda_init = 0.8 - 0.6 * math.exp(-0.3 * 1)
    w_in = w_in_odd[0]
    nd = dh * 2 * HEAD_DIM
    o0 = na + 2 * nk
    perm_d = _deinterleave_cols(2 * dh)
    perm_one = _deinterleave_cols(1)
    w_cq = w_in[:, :na][:, perm_q].astype(BF16)
    w_ck = w_in[:, na:na + nk][:, perm_k].astype(BF16)
    w_dq = w_in[:, o0:o0 + nd][:, perm_d].astype(BF16)
    w_dk = w_in[:, o0 + nd:o0 + 2 * nd][:, perm_d].astype(BF16)
    w_v = jnp.concatenate([w_in[:, o0 + 2 * nd:], w_in[:, na + nk:o0]], 1).astype(BF16)
    qg = row2(q_norm_g[0][perm_one]).astype(F32)
    kg = row2(k_norm_g[0][perm_one]).astype(F32)
    cq_rot, cq_nopos = _proj(h, w_cq, qscale(na), n_rows=n_lat, gain=qg, rope_tabs=tabs, emit_nopos=True,
                             name="proj_cq", **kw)
    ck = _proj(h, w_ck, ones(nk), n_rows=n_all, gain=kg, rope_tabs=tabs, name="proj_ck", **kw)
    dq_rot, dq_nopos = _proj(h, w_dq, qscale(nd), n_rows=n_lat, rope_tabs=tabs, emit_nopos=True,
                             name="proj_dq", **kw)
    dk = _proj(h, w_dk, ones(nd), n_rows=n_all, rope_tabs=tabs, name="proj_dk", **kw)
    vv = _proj(h, w_v, ones(nk + nd), n_rows=n_all, name="proj_odd_v", **kw)
    o_c = _dense_attn(cq_rot, cq_nopos, ck, vv, None, n_batch=n_batch, n_q=seq, q_row0=0, n_k=seq, k_row0=0,
                      ctx_len=ctx_len, ctx_row0=n_lat, n_kv=kvh, g=gq, q_col=0, k_col=0, v_col=nd // HEAD_DIM,
                      name="attn_c")
    lam_vecs = jnp.stack([lambda_q1[0], lambda_k1[0], lambda_q2[0], lambda_k2[0]]).astype(F32)
    o_d = _diff_attn(dq_rot, dq_nopos, dk, vv, lam_vecs, row2(subln_g[0]).astype(F32), n_batch=n_batch,
                     seq=seq, ctx_len=ctx_len, n_latent=n_lat, n_heads=dh, lambda_init=lambda_init)
    w_out = w_out_odd[0].astype(BF16)
    x1, h2, idx, wts = _outproj_ln((o_c, None), (o_d, None), (xt, None), w_out[:na], w_out[na:], g1, sc2,
                                   sh2, row2(ln1_g[1]), row2(ln1_b[1]), w_router_t, router_bias_col,
                                   n_lat=n_lat, seq=seq, n_groups=n_groups, alpha=alpha)
    y = moe(h2, idx, wts, 1)
    x2, _ = _resid_ln(x1, y, g2, row2(ln2_g[1]), row2(ln2_b[1]), None, None, n_rows=n_lat, seq=seq,
                      n_groups=n_groups, alpha=alpha)
    return x2.reshape(n_batch, seq, d)
```

```python
import functools
import math

import numpy as np
import jax
import jax.numpy as jnp
from jax import lax
from jax.experimental import pallas as pl
from jax.experimental.pallas import tpu as pltpu

F32 = jnp.float32
BF16 = jnp.bfloat16

HEAD_DIM = 128
GRID_W = 64
ROPE_THETA = 10000.0
SCALE = HEAD_DIM ** -0.5
NEG_INF = -1e30
A_WINDOW = 128
A_BLOCK = 128
NA_ROWS = 8
NA_COLS = 16
B_TILE_ROWS = 4
B_HEADS_PER_STEP = 8
A_BLOCKS_PER_STEP = 4
B_BLOCKS_PER_STEP = 2
STREAM_BUFFERS = 3
D_HEADS_PER_STEP = 1
ROW_PARTS = 2
PROJ_ROW_PARTS = 4
EPILOGUE_ROWS = 128
N_EXPERTS = 16
N_GROUPS = 4
EXPERTS_PER_GROUP = N_EXPERTS // N_GROUPS
LN_EPS = 1e-5
RMS_EPS = 1e-6
LOG2E = math.log2(math.e)
Q_SCALE = SCALE * LOG2E

V7X_VMEM_BYTES = 64 * 1024 * 1024
VMEM_LIMIT = V7X_VMEM_BYTES - 8 * 1024 * 1024
LANES = 128

NT_DIMS = (((1,), (1,)), ((), ()))


def _params(*sem):
    return pltpu.CompilerParams(dimension_semantics=sem, vmem_limit_bytes=VMEM_LIMIT)


def _adaln_kernel(c_ref, w_ref, b_ref, o_ref):
    c = c_ref[...]
    a = (c * (1.0 / (1.0 + jnp.exp(-c)))).astype(BF16)
    o_ref[...] = jnp.dot(a, w_ref[...].astype(BF16), preferred_element_type=F32) + b_ref[...]


def _adaln(c_rows, w_ada, b_ada):
    depth, d, n6 = w_ada.shape
    rows = c_rows.shape[0]
    tn = 1024
    return pl.pallas_call(
        _adaln_kernel,
        out_shape=jax.ShapeDtypeStruct((depth, rows, n6), F32),
        grid=(depth, n6 // tn),
        in_specs=[
            pl.BlockSpec((rows, d), lambda l, j: (0, 0)),
            pl.BlockSpec((None, d, tn), lambda l, j: (l, 0, j)),
            pl.BlockSpec((None, 1, tn), lambda l, j: (l, 0, j)),
        ],
        out_specs=pl.BlockSpec((None, rows, tn), lambda l, j: (l, 0, j)),
        compiler_params=_params("parallel", "parallel"),
        name="adaln",
    )(c_rows, w_ada, b_ada.reshape(depth, 1, n6))


def _two_stream_specs(tm, width, n_lat_tiles):
    return [pl.BlockSpec((tm, width), lambda i: (jnp.minimum(i, n_lat_tiles - 1), 0)),
            pl.BlockSpec((tm, width), lambda i: (jnp.maximum(i - n_lat_tiles, 0), 0))]


def _pick_stream(lat_ref, ctx_ref, n_lat_tiles):
    if ctx_ref is None:
        return lat_ref[...]
    return jnp.where(pl.program_id(0) >= n_lat_tiles, ctx_ref[...], lat_ref[...])


def _modulate_kernel(x_ref, c_ref, sc_ref, sh_ref, o_ref, *, n_lat_tiles):
    x = _pick_stream(x_ref, c_ref, n_lat_tiles)
    o_ref[...] = (x * (1.0 + sc_ref[0]) + sh_ref[0]).astype(o_ref.dtype)


def _modulate(x_lat, x_ctx, sc, sh, *, seq, n_groups):
    d = x_lat.shape[1]
    rows = x_lat.shape[0] + x_ctx.shape[0]
    tm = 512
    n_lat_tiles = x_lat.shape[0] // tm
    grp = lambda i: (jnp.minimum(i // (seq // tm), n_groups - 1), 0, 0)
    return pl.pallas_call(
        functools.partial(_modulate_kernel, n_lat_tiles=n_lat_tiles),
        out_shape=jax.ShapeDtypeStruct((rows, d), BF16),
        grid=(rows // tm,),
        in_specs=_two_stream_specs(tm, d, n_lat_tiles)
        + [pl.BlockSpec((1, 1, d), grp), pl.BlockSpec((1, 1, d), grp)],
        out_specs=pl.BlockSpec((tm, d), lambda i: (i, 0)),
        compiler_params=_params("parallel"),
        name="modulate",
    )(x_lat, x_ctx, sc, sh)


def _cast_kernel(x_ref, o_ref):
    o_ref[...] = x_ref[...].astype(o_ref.dtype)


def _cast_bf16(w):
    shape = w.shape
    w2 = w.reshape(-1, shape[-1])
    tm = min(w2.shape[0], (1024 * 1024) // shape[-1])
    block = (tm, shape[-1])

    def streamed(x_hbm, o_hbm):
        pltpu.emit_pipeline(
            _cast_kernel, grid=(w2.shape[0] // tm,),
            in_specs=[pl.BlockSpec(block, lambda i: (i, 0), pipeline_mode=pl.Buffered(STREAM_BUFFERS))],
            out_specs=[pl.BlockSpec(block, lambda i: (i, 0))],
        )(x_hbm, o_hbm)

    out = pl.pallas_call(
        streamed,
        out_shape=jax.ShapeDtypeStruct(w2.shape, BF16),
        in_specs=[pl.BlockSpec(memory_space=pl.ANY)],
        out_specs=pl.BlockSpec(memory_space=pl.ANY),
        compiler_params=pltpu.CompilerParams(vmem_limit_bytes=VMEM_LIMIT),
        name="cast_bf16",
    )(w2)
    return out.reshape(shape)


def _proj_kernel(*refs, tn, norm, rope, emit_nopos):
    it = iter(refs)
    x_ref, w_ref, cs_ref = next(it), next(it), next(it)
    g_ref = next(it) if norm else None
    cos_ref, sin_ref = (next(it), next(it)) if rope else (None, None)
    o_ref = next(it)
    n_ref = next(it) if emit_nopos else None
    tm = x_ref.shape[0]
    n_parts = PROJ_ROW_PARTS if tm % (PROJ_ROW_PARTS * 128) == 0 else 1
    rows_per_part = tm // n_parts
    part_rows = [slice(p * rows_per_part, (p + 1) * rows_per_part) for p in range(n_parts)]
    matmul = lambda p: jnp.dot(x_ref[part_rows[p], :], w_ref[...], preferred_element_type=F32)
    acc_next = matmul(0)
    for part in range(n_parts):
        rows = part_rows[part]
        acc = acc_next
        if part + 1 < n_parts:
            acc_next = matmul(part + 1)
        for hd in range(tn // HEAD_DIM):
            sl = slice(hd * HEAD_DIM, (hd + 1) * HEAD_DIM)
            xh = acc[:, sl]
            if norm:
                xh = xh * lax.rsqrt(jnp.mean(xh * xh, axis=-1, keepdims=True) + RMS_EPS) * g_ref[...]
            cs = cs_ref[:, sl]
            if rope:
                rot = xh * cos_ref[rows, :] + pltpu.roll(xh, HEAD_DIM // 2, 1) * sin_ref[rows, :]
                o_ref[rows, sl] = (rot * cs).astype(o_ref.dtype)
                if emit_nopos:
                    n_ref[rows, sl] = (xh * cs).astype(n_ref.dtype)
            else:
                o_ref[rows, sl] = (xh * cs).astype(o_ref.dtype)


def _proj(x, w, col_scale, *, n_rows, seq, n_latent, tn=None, gain=None, rope_tabs=None,
          emit_nopos=False, name="proj"):
    d = x.shape[1]
    nc = w.shape[1]
    tm = min(1024, seq)
    tn = nc if tn is None else tn
    norm, rope = gain is not None, rope_tabs is not None
    in_specs = [pl.BlockSpec((tm, d), lambda i, j: (i, 0)),
                pl.BlockSpec((d, tn), lambda i, j: (0, j)),
                pl.BlockSpec((1, tn), lambda i, j: (0, j))]
    args = [x, w, col_scale]
    if norm:
        in_specs.append(pl.BlockSpec((1, HEAD_DIM), lambda i, j: (0, 0)))
        args.append(gain)
    if rope:
        per_seq = seq // tm
        tab = lambda i, j: (jnp.where(i < n_latent // tm, i % per_seq, per_seq), 0)
        in_specs += [pl.BlockSpec((tm, HEAD_DIM), tab)] * 2
        args += list(rope_tabs)
    n_out = 2 if emit_nopos else 1
    out_shape = [jax.ShapeDtypeStruct((n_rows, nc), BF16)] * n_out
    out_specs = [pl.BlockSpec((tm, tn), lambda i, j: (i, j))] * n_out
    res = pl.pallas_call(
        functools.partial(_proj_kernel, tn=tn, norm=norm, rope=rope, emit_nopos=emit_nopos),
        out_shape=out_shape,
        grid=(n_rows // tm, nc // tn),
        in_specs=in_specs,
        out_specs=out_specs,
        compiler_params=_params("parallel", "parallel"),
        name=name,
    )(*args)
    return res if emit_nopos else res[0]


def _stack_heads(x, g):
    if g == 1:
        return x
    return jnp.concatenate([x[:, i * HEAD_DIM:(i + 1) * HEAD_DIM] for i in range(g)], axis=0)


def _unstack_store(o_ref, o, g, t, col0=0):
    for i in range(g):
        o_ref[:, col0 + i * HEAD_DIM:col0 + (i + 1) * HEAD_DIM] = o[i * t:(i + 1) * t].astype(o_ref.dtype)


def _sink_column(sink_ref, first_head, g, t):
    cols = [jnp.full((t, 1), sink_ref[first_head + i] * LOG2E, F32) for i in range(g)]
    return cols[0] if g == 1 else jnp.concatenate(cols, axis=0)


def _with_ones(v):
    return jnp.concatenate([v, jnp.ones_like(v)], axis=1)


def _row_max(pieces):
    cols = [p[:, c:c + LANES] for p in pieces for c in range(0, p.shape[1], LANES)]
    return jnp.max(functools.reduce(jnp.maximum, cols), axis=-1, keepdims=True)


def _online_update(s, v_aug, m, acc):
    m_new = jnp.maximum(m, jnp.max(s, axis=-1, keepdims=True))
    p = jnp.exp2(s - m_new)
    acc = jnp.exp2(m - m_new) * acc + jnp.dot(p.astype(BF16), v_aug, preferred_element_type=F32)
    return m_new, acc


def _band_kernel(*refs, nq, nh, g, bq, has_sink, same_qc, shared_bias):
    it = iter(refs)
    sink_ref = next(it) if has_sink else None
    q_ref = next(it)
    qc_ref = q_ref if same_qc else next(it)
    k_refs = [next(it) for _ in range(nq + 2)]
    v_refs = [next(it) for _ in range(nq + 2)]
    kx_ref, vx_ref = next(it), next(it)
    bias_refs = [next(it) for _ in range(nq)]
    o_ref = next(it)
    hw = g * HEAD_DIM
    chains = [(qb, h) for qb in range(nq) for h in range(nh)]

    def scores(qb, h):
        rows = slice(qb * bq, (qb + 1) * bq)
        qs = slice(h * hw, (h + 1) * hw)
        ks = slice(h * HEAD_DIM, (h + 1) * HEAD_DIM)
        q = _stack_heads(q_ref[rows, qs], g)
        qc = q if same_qc else _stack_heads(qc_ref[rows, qs], g)
        bias = bias_refs[qb].at[0 if shared_bias else h]
        s = [lax.dot_general(q, k_refs[qb + j][:, ks], NT_DIMS, preferred_element_type=F32)
             + bias[:, j * bq:(j + 1) * bq] for j in range(3)]
        s.append(lax.dot_general(qc, kx_ref[:, ks], NT_DIMS, preferred_element_type=F32))
        return s

    def probs(qb, h, s):
        m = _row_max(s)
        sink = None
        if has_sink:
            sink = _sink_column(sink_ref, (pl.program_id(0) * nh + h) * g, g, bq)
            m = jnp.maximum(m, sink)
        return [jnp.exp2(sj - m).astype(BF16) for sj in s], (None if sink is None else jnp.exp2(sink - m))

    def output(qb, h, p, sink_term):
        rows = slice(qb * bq, (qb + 1) * bq)
        ks = slice(h * HEAD_DIM, (h + 1) * HEAD_DIM)
        vs = [v_refs[qb + j][:, ks] for j in range(3)] + [vx_ref[:, ks]]
        o = functools.reduce(jnp.add, [jnp.dot(pj, _with_ones(vj), preferred_element_type=F32)
                                       for pj, vj in zip(p, vs)])
        den = o[:, HEAD_DIM:]
        if sink_term is not None:
            den = den + sink_term
        _unstack_store(o_ref.at[rows, :], o[:, :HEAD_DIM] / den, g, bq, col0=h * hw)

    s_of, p_of = {}, {}
    for step in range(len(chains) + 2):
        if step < len(chains):
            s_of[step] = scores(*chains[step])
        if 0 <= step - 1 < len(chains):
            p_of[step - 1] = probs(*chains[step - 1], s_of.pop(step - 1))
        if 0 <= step - 2 < len(chains):
            output(*chains[step - 2], *p_of.pop(step - 2))


def _band_attn(q, qc, k, v, bias, sink, *, n_batch, seq, ctx_len, n_latent, n_kv, nh, g, bq,
               q_col, k_col, v_col, name, nq=1):
    nblk = seq // bq
    assert nblk % nq == 0
    nstep = nblk // nq
    ctx_blk = n_latent // ctx_len
    shared_bias = bias.shape[0] == 1
    has_sink, same_qc = sink is not None, qc is None
    qw, kw = nh * g * HEAD_DIM, nh * HEAD_DIM

    def q_map(hg, i, b):
        return (b * nstep + i, q_col + hg)

    def kv_map(col, off):
        return lambda hg, i, b: (b * nblk + jnp.clip(i * nq + off, 0, nblk - 1), col + hg)

    def ctx_map(col):
        return lambda hg, i, b: (ctx_blk + b, col + hg)

    def bias_map(qb):
        def index(hg, i, b):
            blk = i * nq + qb
            case = jnp.where(blk == 0, 0, jnp.where(blk == nblk - 1, 2, 1))
            return (0 if shared_bias else hg, case, 0, 0)
        return index

    in_specs, args = [], []
    if has_sink:
        in_specs.append(pl.BlockSpec(memory_space=pltpu.SMEM))
        args.append(sink)
    in_specs.append(pl.BlockSpec((nq * bq, qw), q_map))
    args.append(q)
    if not same_qc:
        in_specs.append(pl.BlockSpec((nq * bq, qw), q_map))
        args.append(qc)
    offsets = range(-1, nq + 1)
    in_specs += [pl.BlockSpec((bq, kw), kv_map(k_col, off)) for off in offsets]
    args += [k] * len(offsets)
    in_specs += [pl.BlockSpec((bq, kw), kv_map(v_col, off)) for off in offsets]
    args += [v] * len(offsets)
    in_specs += [pl.BlockSpec((ctx_len, kw), ctx_map(k_col)), pl.BlockSpec((ctx_len, kw), ctx_map(v_col))]
    args += [k, v]
    in_specs += [pl.BlockSpec((1 if shared_bias else nh, None, g * bq, 3 * bq), bias_map(qb)) for qb in range(nq)]
    args += [bias] * nq
    return pl.pallas_call(
        functools.partial(_band_kernel, nq=nq, nh=nh, g=g, bq=bq, has_sink=has_sink, same_qc=same_qc,
                          shared_bias=shared_bias),
        out_shape=jax.ShapeDtypeStruct((n_latent, n_kv * g * HEAD_DIM), BF16),
        grid=(n_kv // nh, nstep, n_batch),
        in_specs=in_specs,
        out_specs=pl.BlockSpec((nq * bq, qw), lambda hg, i, b: (b * nstep + i, hg)),
        compiler_params=_params("parallel", "parallel", "parallel"),
        name=name,
    )(*args)


def _dense_kernel(*refs, nh, g, tq, tk, n_k, has_ctx, has_sink):
    it = iter(refs)
    sink_ref = next(it) if has_sink else None
    q_ref = next(it)
    qc_ref = next(it) if has_ctx else None
    k_ref, v_ref = next(it), next(it)
    kx_ref, vx_ref = (next(it), next(it)) if has_ctx else (None, None)
    o_ref = next(it)
    hw = g * HEAD_DIM
    rows = g * tq
    for h in range(nh):
        qs = slice(h * hw, (h + 1) * hw)
        ks = slice(h * HEAD_DIM, (h + 1) * HEAD_DIM)
        q = _stack_heads(q_ref[:, qs], g)
        m = jnp.full((rows, 1), NEG_INF, F32)
        acc = jnp.zeros((rows, 2 * HEAD_DIM), F32)
        chunks = [(q, k_ref.at[c * tk:(c + 1) * tk, ks], v_ref.at[c * tk:(c + 1) * tk, ks])
                  for c in range(n_k // tk)]
        if has_ctx:
            chunks.append((_stack_heads(qc_ref[:, qs], g), kx_ref.at[:, ks], vx_ref.at[:, ks]))
        score = lambda ch: lax.dot_general(ch[0], ch[1][...], NT_DIMS, preferred_element_type=F32)
        s_next = score(chunks[0])
        for c, ch in enumerate(chunks):
            s = s_next
            if c + 1 < len(chunks):
                s_next = score(chunks[c + 1])
            m, acc = _online_update(s, _with_ones(ch[2][...]), m, acc)
        num, den = acc[:, :HEAD_DIM], acc[:, HEAD_DIM:]
        if has_sink:
            sink = _sink_column(sink_ref, (pl.program_id(1) * nh + h) * g, g, tq)
            m_new = jnp.maximum(m, sink)
            alpha = jnp.exp2(m - m_new)
            num = alpha * num
            den = alpha * den + jnp.exp2(sink - m_new)
        _unstack_store(o_ref, num / den, g, tq, col0=h * hw)


def _dense_attn(q, qc, k, v, sink, *, n_batch, n_q, q_row0, n_k, k_row0, ctx_len, ctx_row0,
                n_kv, g, q_col, k_col, v_col, name, nh=1):
    tq = min(256, n_q)
    tk = min(512, n_k)
    nq_blk = n_q // tq
    has_ctx, has_sink = qc is not None, sink is not None
    qw, kw = nh * g * HEAD_DIM, nh * HEAD_DIM

    def q_map(b, h, i):
        return (q_row0 // tq + b * nq_blk + i, q_col + h)

    in_specs, args = [], []
    if has_sink:
        in_specs.append(pl.BlockSpec(memory_space=pltpu.SMEM))
        args.append(sink)
    in_specs.append(pl.BlockSpec((tq, qw), q_map))
    args.append(q)
    if has_ctx:
        in_specs.append(pl.BlockSpec((tq, qw), q_map))
        args.append(qc)
    in_specs += [pl.BlockSpec((n_k, kw), lambda b, h, i: (k_row0 // n_k + b, k_col + h)),
                 pl.BlockSpec((n_k, kw), lambda b, h, i: (k_row0 // n_k + b, v_col + h))]
    args += [k, v]
    if has_ctx:
        in_specs += [pl.BlockSpec((ctx_len, kw), lambda b, h, i: (ctx_row0 // ctx_len + b, k_col + h)),
                     pl.BlockSpec((ctx_len, kw), lambda b, h, i: (ctx_row0 // ctx_len + b, v_col + h))]
        args += [k, v]
    return pl.pallas_call(
        functools.partial(_dense_kernel, nh=nh, g=g, tq=tq, tk=tk, n_k=n_k, has_ctx=has_ctx,
                          has_sink=has_sink),
        out_shape=jax.ShapeDtypeStruct((n_batch * n_q, n_kv * g * HEAD_DIM), BF16),
        grid=(n_batch, n_kv // nh, nq_blk),
        in_specs=in_specs,
        out_specs=pl.BlockSpec((tq, qw), lambda b, h, i: (b * nq_blk + i, h)),
        compiler_params=_params("parallel", "parallel", "parallel"),
        name=name,
    )(*args)


def _diff_update(s, v, m, l, acc):
    m_new = jnp.maximum(m, jnp.max(s, axis=-1, keepdims=True))
    alpha = jnp.exp2(m - m_new)
    p = jnp.exp2(s - m_new)
    l = alpha * l + jnp.sum(p, axis=-1, keepdims=True)
    acc = alpha * acc + jnp.dot(p.astype(BF16), v, preferred_element_type=F32)
    return m_new, l, acc


def _diff_kernel(lam_ref, g_ref, q_ref, qc_ref, k_ref, v_ref, kx_ref, vx_ref, o_ref, *,
                 nh, tq, tk, n_k, lambda_init):
    dv = 2 * HEAD_DIM
    lam = (jnp.exp(jnp.sum(lam_ref[0:1, :] * lam_ref[1:2, :], axis=1, keepdims=True))
           - jnp.exp(jnp.sum(lam_ref[2:3, :] * lam_ref[3:4, :], axis=1, keepdims=True)) + lambda_init)
    maps = [(h, t) for h in range(nh) for t in range(2)]
    qk_cols = {(h, t): slice(h * dv + t * HEAD_DIM, h * dv + (t + 1) * HEAD_DIM) for h, t in maps}
    v_cols = {h: slice(h * dv, (h + 1) * dv) for h in range(nh)}
    state = {mp: (jnp.full((tq, 1), NEG_INF, F32), jnp.zeros((tq, 1), F32), jnp.zeros((tq, dv), F32))
             for mp in maps}
    for c in range(n_k // tk):
        rows = slice(c * tk, (c + 1) * tk)
        for mp in maps:
            s = lax.dot_general(q_ref[:, qk_cols[mp]], k_ref[rows, qk_cols[mp]], NT_DIMS,
                                preferred_element_type=F32)
            state[mp] = _diff_update(s, v_ref[rows, v_cols[mp[0]]], *state[mp])
    for h in range(nh):
        outs = []
        for t in range(2):
            mp = (h, t)
            s = lax.dot_general(qc_ref[:, qk_cols[mp]], kx_ref[:, qk_cols[mp]], NT_DIMS,
                                preferred_element_type=F32)
            _, l, acc = _diff_update(s, vx_ref[:, v_cols[h]], *state[mp])
            outs.append(acc / l)
        o = outs[0] - lam * outs[1]
        o = o * lax.rsqrt(jnp.mean(o * o, axis=-1, keepdims=True) + RMS_EPS) * g_ref[...]
        o_ref[:, v_cols[h]] = (o * (1.0 - lambda_init)).astype(o_ref.dtype)


def _diff_attn(q, qc, k, v, lam_vecs, subln_g, *, n_batch, seq, ctx_len, n_latent, n_heads,
               lambda_init):
    nh = D_HEADS_PER_STEP
    tq = min(512, seq)
    tk = min(512, seq)
    dv = 2 * HEAD_DIM
    nq_blk = seq // tq
    ctx_blk = n_latent // ctx_len
    q_map = lambda b, h, i: (b * nq_blk + i, h)
    return pl.pallas_call(
        functools.partial(_diff_kernel, nh=nh, tq=tq, tk=tk, n_k=seq, lambda_init=lambda_init),
        out_shape=jax.ShapeDtypeStruct((n_latent, n_heads * dv), BF16),
        grid=(n_batch, n_heads // nh, nq_blk),
        in_specs=[
            pl.BlockSpec((4, HEAD_DIM), lambda b, h, i: (0, 0)),
            pl.BlockSpec((1, dv), lambda b, h, i: (0, 0)),
            pl.BlockSpec((tq, nh * dv), q_map),
            pl.BlockSpec((tq, nh * dv), q_map),
            pl.BlockSpec((seq, nh * dv), lambda b, h, i: (b, h)),
            pl.BlockSpec((seq, nh * dv), lambda b, h, i: (b, h)),
            pl.BlockSpec((ctx_len, nh * dv), lambda b, h, i: (ctx_blk + b, h)),
            pl.BlockSpec((ctx_len, nh * dv), lambda b, h, i: (ctx_blk + b, h)),
        ],
        out_specs=pl.BlockSpec((tq, nh * dv), q_map),
        compiler_params=_params("parallel", "parallel", "parallel"),
        name="diff_attn",
    )(lam_vecs, subln_g, q, qc, k, v, k, v)


def _layer_norm(z, g, b):
    mu = jnp.mean(z, axis=-1, keepdims=True)
    zc = z - mu
    var = jnp.mean(zc * zc, axis=-1, keepdims=True)
    return zc * lax.rsqrt(var + LN_EPS) * g + b


def _split_bf16(x):
    hi = x.astype(BF16)
    return hi, (x - hi.astype(F32)).astype(BF16)


def _first_argmax(vals, idx, width):
    m = jnp.max(vals, axis=0, keepdims=True)
    first = jnp.min(jnp.where(vals == m, idx, float(width)), axis=0, keepdims=True)
    return m, first


def _route(logits, bias):
    e, t = logits.shape
    scores = 1.0 / (1.0 + jnp.exp(-logits))
    biased = scores + bias
    row_i = lax.broadcasted_iota(jnp.int32, (e, t), 0)
    grp = lax.shift_right_logical(row_i, int(math.log2(EXPERTS_PER_GROUP)))
    row = row_i.astype(F32)
    neg = -jnp.inf
    best_score, best = None, None
    for gi in range(N_GROUPS):
        vg = jnp.where(grp == gi, biased, neg)
        m1, i1 = _first_argmax(vg, row, e)
        m2 = jnp.max(jnp.where(row == i1, neg, vg), axis=0, keepdims=True)
        gs = m1 + m2
        if gi == 0:
            best_score, best = gs, jnp.zeros((1, t), jnp.int32)
        else:
            upd = gs > best_score
            best = jnp.where(upd, gi, best)
            best_score = jnp.where(upd, gs, best_score)
    masked = jnp.where(grp == best, biased, neg)
    _, e1 = _first_argmax(masked, row, e)
    _, e2 = _first_argmax(jnp.where(row == e1, neg, masked), row, e)
    w1 = jnp.sum(jnp.where(row == e1, scores, 0.0), axis=0, keepdims=True)
    w2 = jnp.sum(jnp.where(row == e2, scores, 0.0), axis=0, keepdims=True)
    den = w1 + w2
    two = lax.broadcasted_iota(jnp.int32, (2, t), 0)
    return jnp.where(two == 0, e1, e2).astype(jnp.int32), jnp.where(two == 0, w1 / den, w2 / den)


def _outproj_kernel(*refs, alpha, two_streams, n_lat_tiles):
    it = iter(refs)
    xa_ref, xa_ctx = next(it), (next(it) if two_streams else None)
    xb_ref, xb_ctx = next(it), (next(it) if two_streams else None)
    xres_ref, xres_ctx = next(it), (next(it) if two_streams else None)
    (wa_ref, wb_ref, g1_ref, sc2_ref, sh2_ref, lng_ref, lnb_ref, wrt_ref, rb_ref,
     x1_ref, h2_ref, idx_ref, wts_ref) = it
    w_hi, w_lo = _split_bf16(wrt_ref[...])
    w_cat = jnp.concatenate([w_hi, w_lo], axis=0)
    tm = x1_ref.shape[0]
    rows_per_part = tm // ROW_PARTS

    def pick(lat, ctx, part):
        rows = slice(part * rows_per_part, (part + 1) * rows_per_part)
        return _pick_stream(lat.at[rows, :], None if ctx is None else ctx.at[rows, :], n_lat_tiles)

    def matmul(part):
        return (jnp.dot(pick(xa_ref, xa_ctx, part), wa_ref[...], preferred_element_type=F32)
                + jnp.dot(pick(xb_ref, xb_ctx, part), wb_ref[...], preferred_element_type=F32))

    lookahead = not two_streams
    y_next = matmul(0)
    for part in range(ROW_PARTS):
        y_all = y_next
        if part + 1 < ROW_PARTS and lookahead:
            y_next = matmul(part + 1)
        xres_all = pick(xres_ref, xres_ctx, part)
        n_sub = rows_per_part // EPILOGUE_ROWS
        for sub in range(n_sub):
            sr = slice(sub * EPILOGUE_ROWS, (sub + 1) * EPILOGUE_ROWS)
            orow = slice(part * rows_per_part + sub * EPILOGUE_ROWS,
                         part * rows_per_part + (sub + 1) * EPILOGUE_ROWS)
            x1 = _layer_norm(alpha * xres_all[sr] + g1_ref[0] * y_all[sr], lng_ref[...], lnb_ref[...])
            x1_ref[orow, :] = x1
            h2 = x1 * (1.0 + sc2_ref[0]) + sh2_ref[0]
            h2_ref[orow, :] = h2
            h_hi, h_lo = _split_bf16(h2)
            both = lax.dot_general(w_cat, h_hi, NT_DIMS, preferred_element_type=F32)
            logits = (both[:N_EXPERTS] + both[N_EXPERTS:]
                      + lax.dot_general(w_hi, h_lo, NT_DIMS, preferred_element_type=F32))
            idx, wts = _route(logits, rb_ref[...])
            idx_ref[:, orow] = idx
            wts_ref[:, orow] = wts
        if part + 1 < ROW_PARTS and not lookahead:
            y_next = matmul(part + 1)


def _outproj_ln(xa, xb, xres, wa, wb, g1, sc2, sh2, ln_g, ln_b, w_router_t, router_bias, *,
                n_lat, seq, n_groups, alpha):
    two_streams = xa[1] is not None
    d = xres[0].shape[1]
    ka, kb = xa[0].shape[1], xb[0].shape[1]
    tm = 256 * ROW_PARTS
    n_lat_tiles = n_lat // tm
    n_rows = n_lat + (xres[1].shape[0] if two_streams else 0)
    row = lambda i: (i, 0)
    col = lambda i: (0, i)
    const = lambda i: (0, 0)
    grp = lambda i: (jnp.minimum(i // (seq // tm), n_groups - 1), 0, 0)
    in_specs, args = [], []
    for pair, width in ((xa, ka), (xb, kb), (xres, d)):
        if two_streams:
            in_specs += _two_stream_specs(tm, width, n_lat_tiles)
            args += list(pair)
        else:
            in_specs.append(pl.BlockSpec((tm, width), row))
            args.append(pair[0])
    once = pl.Buffered(1)
    in_specs += [pl.BlockSpec((ka, d), const, pipeline_mode=once), pl.BlockSpec((kb, d), const, pipeline_mode=once),
                 pl.BlockSpec((1, 1, d), grp), pl.BlockSpec((1, 1, d), grp), pl.BlockSpec((1, 1, d), grp),
                 pl.BlockSpec((1, d), const), pl.BlockSpec((1, d), const),
                 pl.BlockSpec((N_EXPERTS, d), const), pl.BlockSpec((N_EXPERTS, 1), const)]
    args += [wa, wb, g1, sc2, sh2, ln_g, ln_b, w_router_t, router_bias]
    return pl.pallas_call(
        functools.partial(_outproj_kernel, alpha=alpha, two_streams=two_streams, n_lat_tiles=n_lat_tiles),
        out_shape=[jax.ShapeDtypeStruct((n_rows, d), F32), jax.ShapeDtypeStruct((n_rows, d), F32),
                   jax.ShapeDtypeStruct((2, n_rows), jnp.int32), jax.ShapeDtypeStruct((2, n_rows), F32)],
        grid=(n_rows // tm,),
        in_specs=in_specs,
        out_specs=[pl.BlockSpec((tm, d), row), pl.BlockSpec((tm, d), row),
                   pl.BlockSpec((2, tm), col), pl.BlockSpec((2, tm), col)],
        compiler_params=_params("parallel"),
        name="outproj_ln",
    )(*args)


def _pack_bf16_pairs(x):
    n = x.shape[1] // 2
    lo = pltpu.bitcast(x[:, :n].astype(BF16).astype(F32), jnp.uint32)
    hi = pltpu.bitcast(x[:, n:].astype(BF16).astype(F32), jnp.uint32)
    return jnp.bitwise_or(lax.shift_right_logical(lo, jnp.uint32(16)), hi)


def _unpack_bf16_pairs(w):
    lo = pltpu.bitcast(lax.shift_left(w, jnp.uint32(16)), F32)
    hi = pltpu.bitcast(jnp.bitwise_and(w, jnp.uint32(0xFFFF0000)), F32)
    return jnp.concatenate([lo, hi], axis=1)


def _moe_kernel(te_ref, nv_ref, src_ref, dst_ref, h_hbm, gw_ref, wg_ref, wu_ref, wd_ref, y_hbm,
                xbuf, obuf, gsem, ssem, *, tm, ff_chunks):
    i = pl.program_id(0)
    n = pl.num_programs(0)
    nv = nv_ref[0]

    def start_gather(tile, s):
        base = tile * (tm // 2)
        for r in range(tm):
            word = src_ref[base + r // 2]
            tok = jnp.bitwise_and(word, 0xFFFF) if r % 2 == 0 else lax.shift_right_logical(word, 16)
            pltpu.make_async_copy(h_hbm.at[pl.ds(tok, 1)], xbuf.at[s, pl.ds(r, 1)],
                                  gsem.at[s]).start(priority=r % 2)

    def wait_gather(s):
        pltpu.make_async_copy(h_hbm.at[pl.ds(0, tm)], xbuf.at[s], gsem.at[s]).wait()

    def start_scatter(tile, s):
        base = tile * tm
        for r in range(tm):
            pltpu.make_async_copy(obuf.at[s, pl.ds(r, 1)], y_hbm.at[pl.ds(dst_ref[base + r], 1)],
                                  ssem.at[s]).start(priority=r % 2)

    def wait_scatter(s):
        pltpu.make_async_copy(obuf.at[s], y_hbm.at[pl.ds(0, tm)], ssem.at[s]).wait()

    @pl.when(i == 0)
    def _():
        start_gather(0, 0)
        obuf[...] = jnp.zeros_like(obuf)
        pad0 = y_hbm.shape[0] - 2 * tm
        fills = [pltpu.make_async_copy(obuf.at[s], y_hbm.at[pl.ds(pad0 + s * tm, tm)], ssem.at[s])
                 for s in range(2)]
        for f in fills:
            f.start()
        for f in fills:
            f.wait()

    def step(slot):
        wait_gather(slot)

        @pl.when(i >= 2)
        def _():
            wait_scatter(slot)

        start_gather(jnp.minimum(i + 1, n - 1), 1 - slot)
        x = xbuf[slot].astype(BF16)
        fc = wg_ref.shape[1] // ff_chunks
        y = None
        for c in range(ff_chunks):
            cs = slice(c * fc, (c + 1) * fc)
            gate = jnp.dot(x, wg_ref[:, cs], preferred_element_type=F32)
            up = jnp.dot(x, wu_ref[:, cs], preferred_element_type=F32)
            act = (gate * (1.0 / (1.0 + jnp.exp(-gate))) * up).astype(BF16)
            part = jnp.dot(act, wd_ref[cs, :], preferred_element_type=F32)
            y = part if y is None else y + part
        obuf[slot] = _pack_bf16_pairs(y * gw_ref[...])
        start_scatter(i, slot)

    for slot in range(2):
        @pl.when(jnp.logical_and(i < nv, jnp.bitwise_and(i, 1) == slot))
        def _():
            step(slot)

    @pl.when(i == n - 1)
    def _():
        wait_gather(jnp.bitwise_and(nv, 1))

        @pl.when(nv >= 1)
        def _():
            wait_scatter(jnp.bitwise_and(nv - 1, 1))

        @pl.when(nv >= 2)
        def _():
            wait_scatter(jnp.bitwise_and(nv, 1))


def _moe_ffn(h2, src_packed, dst_row, gw, tile_expert, n_valid, w_gate, w_up, w_down, *, layer, tm):
    t, d = h2.shape
    ff = w_gate.shape[3]
    n_tiles = dst_row.shape[0] // tm
    return pl.pallas_call(
        functools.partial(_moe_kernel, tm=tm, ff_chunks=2),
        out_shape=jax.ShapeDtypeStruct((2 * t + 2 * tm, d // 2), jnp.uint32),
        grid_spec=pltpu.PrefetchScalarGridSpec(
            num_scalar_prefetch=4,
            grid=(n_tiles,),
            in_specs=[pl.BlockSpec(memory_space=pl.ANY),
                      pl.BlockSpec((tm, 1), lambda i, te, nv, src, dst: (i, 0)),
                      pl.BlockSpec((None, None, d, ff), lambda i, te, nv, src, dst: (layer, te[i], 0, 0)),
                      pl.BlockSpec((None, None, d, ff), lambda i, te, nv, src, dst: (layer, te[i], 0, 0)),
                      pl.BlockSpec((None, None, ff, d), lambda i, te, nv, src, dst: (layer, te[i], 0, 0))],
            out_specs=pl.BlockSpec(memory_space=pl.ANY),
            scratch_shapes=[pltpu.VMEM((2, tm, d), F32), pltpu.VMEM((2, tm, d // 2), jnp.uint32),
                            pltpu.SemaphoreType.DMA((2,)), pltpu.SemaphoreType.DMA((2,))],
        ),
        compiler_params=_params("arbitrary"),
        name="moe_ffn",
    )(tile_expert, n_valid, src_packed, dst_row, h2, gw, w_gate, w_up, w_down)


def _take(x, idx):
    return x.at[idx].get(mode="promise_in_bounds")


def _dispatch(idx, wts, *, tm):
    t = idx.shape[1]
    n_pairs = 2 * t
    n_tiles = n_pairs // tm + N_EXPERTS
    e_flat = idx.reshape(n_pairs)
    w_flat = wts.reshape(n_pairs)
    pair_bits = max(1, (n_pairs - 1).bit_length())
    keys = jnp.bitwise_or(jnp.left_shift(e_flat, pair_bits), jnp.arange(n_pairs, dtype=jnp.int32))
    order = jnp.bitwise_and(lax.sort(keys), (1 << pair_bits) - 1)
    experts = jnp.arange(N_EXPERTS, dtype=jnp.int32)
    counts = jnp.sum(e_flat[None, :] == experts[:, None], axis=1).astype(jnp.int32)
    start = jnp.cumsum(counts) - counts
    padded = ((counts + tm - 1) // tm) * tm
    pend = jnp.cumsum(padded)
    pstart = pend - padded
    tile_pos = jnp.arange(n_tiles, dtype=jnp.int32) * tm
    tile_expert = jnp.minimum(jnp.sum(tile_pos[:, None] >= pend[None, :], axis=1), N_EXPERTS - 1).astype(jnp.int32)
    local = (tile_pos - pstart[tile_expert])[:, None] + jnp.arange(tm, dtype=jnp.int32)[None, :]
    valid = local < counts[tile_expert][:, None]
    sorted_pos = jnp.clip(start[tile_expert][:, None] + local, 0, n_pairs - 1)
    src_pair = _take(order, sorted_pos)
    src_tok = jnp.where(valid, jnp.where(src_pair >= t, src_pair - t, src_pair), 0)
    pad_row = n_pairs + (tile_pos[:, None] + jnp.arange(tm, dtype=jnp.int32)[None, :]) % (2 * tm)
    dst_row = jnp.where(valid, src_pair, pad_row)
    gw = jnp.where(valid, _take(w_flat, src_pair), 0.0).astype(F32)
    src_packed = jnp.bitwise_or(src_tok[:, 0::2], jnp.left_shift(src_tok[:, 1::2], 16))
    n_valid = (pend[-1] // tm).astype(jnp.int32).reshape(1)
    return src_packed.reshape(-1), dst_row.reshape(-1), gw.reshape(-1, 1), tile_expert, n_valid


def _resid_ln_kernel(*refs, alpha, emit_h):
    it = iter(refs)
    x_ref, f0_ref, f1_ref, g_ref, lng_ref, lnb_ref = (next(it) for _ in range(6))
    sc_ref, sh_ref = (next(it), next(it)) if emit_h else (None, None)
    o_ref = next(it)
    f = _unpack_bf16_pairs(f0_ref[...]) + _unpack_bf16_pairs(f1_ref[...])
    x2 = _layer_norm(alpha * x_ref[...] + g_ref[0] * f, lng_ref[...], lnb_ref[...])
    o_ref[...] = x2
    if emit_h:
        h_ref = next(it)
        h_ref[...] = (x2 * (1.0 + sc_ref[0]) + sh_ref[0]).astype(h_ref.dtype)


def _resid_ln(x, y, gate, ln_g, ln_b, next_sc, next_sh, *, n_rows, seq, n_groups, alpha):
    d = x.shape[1]
    tm = 512
    emit_h = next_sc is not None
    row = lambda i: (i, 0)
    const = lambda i: (0, 0)
    grp = lambda i: (jnp.minimum(i // (seq // tm), n_groups - 1), 0, 0)
    in_specs = [pl.BlockSpec((tm, d), row), pl.BlockSpec((tm, d // 2), row),
                pl.BlockSpec((tm, d // 2), lambda i: (n_rows // tm + i, 0)), pl.BlockSpec((1, 1, d), grp),
                pl.BlockSpec((1, d), const), pl.BlockSpec((1, d), const)]
    args = [x, y, y, gate, ln_g, ln_b]
    out_shape = [jax.ShapeDtypeStruct((n_rows, d), F32)]
    out_specs = [pl.BlockSpec((tm, d), row)]
    if emit_h:
        in_specs += [pl.BlockSpec((1, 1, d), grp)] * 2
        args += [next_sc, next_sh]
        out_shape.append(jax.ShapeDtypeStruct((n_rows, d), BF16))
        out_specs.append(pl.BlockSpec((tm, d), row))
    res = pl.pallas_call(
        functools.partial(_resid_ln_kernel, alpha=alpha, emit_h=emit_h),
        out_shape=out_shape,
        grid=(n_rows // tm,),
        in_specs=in_specs,
        out_specs=out_specs,
        compiler_params=_params("parallel"),
        name="resid_ln",
    )(*args)
    return res if emit_h else (res[0], None)


def _rope_tables(seq, tm):
    t = jnp.arange(seq, dtype=jnp.int32)
    n_freq = HEAD_DIM // 4
    inv = ROPE_THETA ** (-jnp.arange(n_freq, dtype=F32) / n_freq)
    row = (t // GRID_W).astype(F32)
    col = (t % GRID_W).astype(F32)
    ang = jnp.concatenate([row[:, None] * inv[None], col[:, None] * inv[None]], -1)
    cos, sin = jnp.cos(ang), jnp.sin(ang)
    cos2 = jnp.concatenate([cos, cos], -1)
    sin2 = jnp.concatenate([-sin, sin], -1)
    cos2 = jnp.concatenate([cos2, jnp.ones((tm, HEAD_DIM), F32)], 0)
    sin2 = jnp.concatenate([sin2, jnp.zeros((tm, HEAD_DIM), F32)], 0)
    return cos2, sin2


def _window_bias(seq, g):
    bq = A_BLOCK
    nblk = seq // bq
    rel = np.arange(3 * bq)[None, :] - np.arange(bq)[:, None]
    band = (rel >= bq - A_WINDOW) & (rel <= bq + A_WINDOW)
    tabs = []
    for blk in (0, min(1, nblk - 1), nblk - 1):
        kpos = blk * bq - bq + np.arange(3 * bq)
        ok = band & ((kpos >= 0) & (kpos < seq))[None, :]
        tabs.append(np.tile(np.where(ok, 0.0, NEG_INF).astype(np.float32), (g, 1)))
    return jnp.asarray(np.stack(tabs)[None])


def _neighbourhood_bias(rpb, seq):
    n_heads = rpb.shape[0]
    rows = seq // GRID_W
    kh, kw = min(NA_ROWS, rows), NA_COLS
    tr = B_TILE_ROWS
    nblk = rows // tr
    col = np.arange(GRID_W)
    col_start = np.clip(col - kw // 2, 0, GRID_W - kw)
    col_ok = (col[None, :] >= col_start[:, None]) & (col[None, :] < col_start[:, None] + kw)
    assert np.all(np.abs(col[None, :] - col[:, None])[col_ok] <= NA_COLS - 1)
    masks = []
    for blk in (0, min(1, nblk - 1), nblk - 1):
        r = blk * tr + np.arange(tr)
        r0 = np.clip(r - kh // 2, 0, rows - kh)
        krow = (blk - 1) * tr + np.arange(3 * tr)
        row_ok = (krow[None, :] >= r0[:, None]) & (krow[None, :] < r0[:, None] + kh)
        ok = row_ok[:, None, :, None] & col_ok[None, :, None, :]
        masks.append(ok.reshape(tr * GRID_W, 3 * tr * GRID_W))
    ok = jnp.asarray(np.stack(masks))
    pad = GRID_W - NA_COLS
    rp = jnp.pad(rpb.astype(F32) * LOG2E, ((0, 0), (0, 0), (pad, pad)))
    cexp = jnp.stack([rp[:, :, GRID_W - 1 - cq:2 * GRID_W - 1 - cq] for cq in range(GRID_W)], axis=2)
    off = NA_ROWS - 1 - tr
    assert off - (tr - 1) >= 0 and off + 3 * tr - 1 <= 2 * NA_ROWS - 2
    t5 = jnp.stack([cexp[:, off - rq:off - rq + 3 * tr] for rq in range(tr)], axis=1)
    tab = t5.transpose(0, 1, 3, 2, 4).reshape(n_heads, tr * GRID_W, 3 * tr * GRID_W)
    return jnp.where(ok[None], tab[:, None], NEG_INF)


def _deinterleave_cols(n_heads):
    one = np.concatenate([np.arange(0, HEAD_DIM, 2), np.arange(1, HEAD_DIM, 2)])
    return np.concatenate([h * HEAD_DIM + one for h in range(n_heads)])


def kernel(x, c, ctx, c_ctx, w_ada, b_ada, ln1_g, ln1_b, ln2_g, ln2_b, w_in_even, w_out_even, sink_logits, na_rpb, w_in_odd, w_out_odd, q_norm_g, k_norm_g, lambda_q1, lambda_k1, lambda_q2, lambda_k2, subln_g, w_router, router_bias, w_exp_gate, w_exp_up, w_exp_down):
    n_batch, seq, d = x.shape
    ctx_len = ctx.shape[1]
    depth = w_ada.shape[0]
    assert depth == 2, "one even (A||B) layer followed by one odd (C||D) layer"
    n_lat = n_batch * seq
    n_ctx = n_batch * ctx_len
    n_all = n_lat + n_ctx
    n_groups = n_batch + 1
    alpha = float((2 * depth) ** 0.25)
    heads = d // HEAD_DIM
    qh = heads // 2
    kvh = qh // 4
    gq = qh // kvh
    bh = heads - qh
    dh = (heads - qh) // 2
    moe_tm = 512
    proj_tm = min(1024, seq)
    assert seq % GRID_W == 0 and (seq // GRID_W) % B_TILE_ROWS == 0 and seq // GRID_W >= NA_ROWS
    assert n_ctx % proj_tm == 0 and n_lat % ctx_len == 0 and (2 * n_all) % moe_tm == 0
    assert ctx_len % LANES == 0 and bh % B_HEADS_PER_STEP == 0
    assert n_all <= 1 << 16, "token ids are packed as 16-bit halves for the MoE row gather"

    pad_rows = -(-n_groups // 16) * 16
    c_rows = jnp.concatenate([c, c_ctx[None], jnp.zeros((pad_rows - n_groups, d), F32)], 0)
    mod = _adaln(c_rows, w_ada, b_ada)[:, :n_groups]
    mod = mod.reshape(depth, n_groups, 6, 1, d)
    mods = [[mod[l, :, j] for j in range(6)] for l in range(depth)]

    cos2, sin2 = _rope_tables(seq, proj_tm)
    tabs = (cos2, sin2)
    x_lat, x_ctx = x.reshape(n_lat, d), ctx.reshape(n_ctx, d)
    row2 = lambda v: v.reshape(1, -1)
    ones = lambda n: jnp.ones((1, n), F32)
    qscale = lambda n: jnp.full((1, n), Q_SCALE, F32)
    w_router_t = w_router.T.astype(F32)
    router_bias_col = router_bias.reshape(-1, 1).astype(F32)
    wg_all, wu_all, wd_all = (_cast_bf16(w) for w in (w_exp_gate, w_exp_up, w_exp_down))

    def moe(h2, idx, wts, layer):
        src_packed, dst_row, gw, tile_expert, n_valid = _dispatch(idx, wts, tm=moe_tm)
        return _moe_ffn(h2, src_packed, dst_row, gw, tile_expert, n_valid, wg_all, wu_all, wd_all,
                        layer=layer, tm=moe_tm)

    sh1, sc1, g1, sh2, sc2, g2 = mods[0]
    h = _modulate(x_lat, x_ctx, sc1, sh1, seq=seq, n_groups=n_groups)
    w_in = w_in_even[0]
    na, nk, nb = qh * HEAD_DIM, kvh * HEAD_DIM, bh * HEAD_DIM
    perm_q, perm_k = _deinterleave_cols(qh), _deinterleave_cols(kvh)
    w_aq = w_in[:, :na][:, perm_q].astype(BF16)
    w_ak = w_in[:, na:na + nk][:, perm_k].astype(BF16)
    w_rest = jnp.concatenate([w_in[:, na + 2 * nk:], w_in[:, na + nk:na + 2 * nk]], 1).astype(BF16)
    rest_scale = jnp.concatenate([qscale(nb), ones(2 * nb + nk)], 1)
    kw = dict(seq=seq, n_latent=n_lat)
    aq_rot, aq_nopos = _proj(h, w_aq, qscale(na), n_rows=n_all, rope_tabs=tabs, emit_nopos=True,
                             name="proj_aq", **kw)
    ak = _proj(h, w_ak, ones(nk), n_rows=n_all, rope_tabs=tabs, name="proj_ak", **kw)
    rest = _proj(h, w_rest, rest_scale, n_rows=n_all, tn=w_rest.shape[1] // 2, name="proj_even_rest", **kw)
    sink = sink_logits[0].astype(F32)
    akw = dict(n_batch=n_batch, seq=seq, ctx_len=ctx_len, n_latent=n_lat)
    o_a = _band_attn(aq_rot, aq_nopos, ak, rest, _window_bias(seq, gq), sink, n_kv=kvh, nh=kvh, g=gq,
                     bq=A_BLOCK, q_col=0, k_col=0, v_col=3 * nb // nk, nq=A_BLOCKS_PER_STEP, name="attn_a", **akw)
    bw = B_HEADS_PER_STEP * HEAD_DIM
    o_b = _band_attn(rest, None, rest, rest, _neighbourhood_bias(na_rpb[0], seq), None, n_kv=bh,
                     nh=B_HEADS_PER_STEP, g=1, bq=B_TILE_ROWS * GRID_W, q_col=0, k_col=nb // bw,
                     v_col=2 * nb // bw, nq=B_BLOCKS_PER_STEP, name="attn_b", **akw)
    ckw = dict(n_batch=n_batch, n_q=ctx_len, q_row0=n_lat, n_k=ctx_len, k_row0=n_lat, ctx_len=ctx_len,
               ctx_row0=n_lat)
    oc_a = _dense_attn(aq_rot, None, ak, rest, sink, n_kv=kvh, nh=kvh, g=gq, q_col=0, k_col=0,
                       v_col=3 * nb // nk, name="ctx_attn_a", **ckw)
    oc_b = _dense_attn(rest, None, rest, rest, None, n_kv=bh, nh=bh, g=1, q_col=0, k_col=1, v_col=2,
                       name="ctx_attn_b", **ckw)
    w_out = w_out_even[0].astype(BF16)
    x1, h2, idx, wts = _outproj_ln((o_a, oc_a), (o_b, oc_b), (x_lat, x_ctx), w_out[:na], w_out[na:], g1, sc2,
                                   sh2, row2(ln1_g[0]), row2(ln1_b[0]), w_router_t, router_bias_col,
                                   n_lat=n_lat, seq=seq, n_groups=n_groups, alpha=alpha)
    y = moe(h2, idx, wts, 0)
    nsh1, nsc1 = mods[1][0], mods[1][1]
    xt, h = _resid_ln(x1, y, g2, row2(ln2_g[0]), row2(ln2_b[0]), nsc1, nsh1, n_rows=n_all, seq=seq,
                      n_groups=n_groups, alpha=alpha)

    sh1, sc1, g1, sh2, sc2, g2 = mods[1]
    lambda_init = 0.8 - 0.6 * math.exp(-0.3 * 1)
    w_in = w_in_odd[0]
    nd = dh * 2 * HEAD_DIM
    o0 = na + 2 * nk
    perm_d = _deinterleave_cols(2 * dh)
    perm_one = _deinterleave_cols(1)
    w_cq = w_in[:, :na][:, perm_q].astype(BF16)
    w_ck = w_in[:, na:na + nk][:, perm_k].astype(BF16)
    w_dq = w_in[:, o0:o0 + nd][:, perm_d].astype(BF16)
    w_dk = w_in[:, o0 + nd:o0 + 2 * nd][:, perm_d].astype(BF16)
    w_v = jnp.concatenate([w_in[:, o0 + 2 * nd:], w_in[:, na + nk:o0]], 1).astype(BF16)
    qg = row2(q_norm_g[0][perm_one]).astype(F32)
    kg = row2(k_norm_g[0][perm_one]).astype(F32)
    cq_rot, cq_nopos = _proj(h, w_cq, qscale(na), n_rows=n_lat, gain=qg, rope_tabs=tabs, emit_nopos=True,
                             name="proj_cq", **kw)
    ck = _proj(h, w_ck, ones(nk), n_rows=n_all, gain=kg, rope_tabs=tabs, name="proj_ck", **kw)
    dq_rot, dq_nopos = _proj(h, w_dq, qscale(nd), n_rows=n_lat, rope_tabs=tabs, emit_nopos=True,
                             name="proj_dq", **kw)
    dk = _proj(h, w_dk, ones(nd), n_rows=n_all, rope_tabs=tabs, name="proj_dk", **kw)
    vv = _proj(h, w_v, ones(nk + nd), n_rows=n_all, name="proj_odd_v", **kw)
    o_c = _dense_attn(cq_rot, cq_nopos, ck, vv, None, n_batch=n_batch, n_q=seq, q_row0=0, n_k=seq, k_row0=0,
                      ctx_len=ctx_len, ctx_row0=n_lat, n_kv=kvh, g=gq, q_col=0, k_col=0, v_col=nd // HEAD_DIM,
                      name="attn_c")
    lam_vecs = jnp.stack([lambda_q1[0], lambda_k1[0], lambda_q2[0], lambda_k2[0]]).astype(F32)
    o_d = _diff_attn(dq_rot, dq_nopos, dk, vv, lam_vecs, row2(subln_g[0]).astype(F32), n_batch=n_batch,
                     seq=seq, ctx_len=ctx_len, n_latent=n_lat, n_heads=dh, lambda_init=lambda_init)
    w_out = w_out_odd[0].astype(BF16)
    x1, h2, idx, wts = _outproj_ln((o_c, None), (o_d, None), (xt, None), w_out[:na], w_out[na:], g1, sc2,
                                   sh2, row2(ln1_g[1]), row2(ln1_b[1]), w_router_t, router_bias_col,
                                   n_lat=n_lat, seq=seq, n_groups=n_groups, alpha=alpha)
    y = moe(h2, idx, wts, 1)
    x2, _ = _resid_ln(x1, y, g2, row2(ln2_g[1]), row2(ln2_b[1]), None, None, n_rows=n_lat, seq=seq,
                      n_groups=n_groups, alpha=alpha)
    return x2.reshape(n_batch, seq, d)
```
